```python
import math
import jax, jax.numpy as jnp
from jax import lax
import numpy as np

D_MODEL = 1024
BATCH = 4
SEQ = 8192
DEPTH = 2

HEAD_DIM = 64
A_GROUPS = ((128, 1), (512, 4), (2048, 16))
A_N_GROUPS = 3
A_HEADS_PER_GROUP = 8
A_HEADS = A_N_GROUPS * A_HEADS_PER_GROUP
A_QKV_WIDTH = A_HEADS * HEAD_DIM
A_WIDTH = A_HEADS_PER_GROUP * HEAD_DIM
A_Q_BLOCK = 64
B_Q_HEADS = 8
B_KV_HEADS = 2
B_WIDTH = B_Q_HEADS * HEAD_DIM
B_KV_WIDTH = B_KV_HEADS * HEAD_DIM
B_Q_BLOCK = 128
GRID_W = 64
ROPE_THETA = 10000.0
REL_BUCKETS = 32
REL_MAX_DISTANCE = 1024
LN_EPS = 1e-5
QK_EPS = 1e-6
NEG_INF = -1e30
DEEPNORM_ALPHA = float((2 * DEPTH) ** 0.25)
DEEPNORM_BETA = float((8 * DEPTH) ** -0.25)
SPLITS = (A_QKV_WIDTH, A_QKV_WIDTH, A_QKV_WIDTH, A_WIDTH,
          B_WIDTH, B_KV_WIDTH, B_KV_WIDTH, B_WIDTH, 2 * D_MODEL)
SPLIT_POINTS = tuple(int(v) for v in np.cumsum(SPLITS)[:-1])
IN_WIDTH = int(sum(SPLITS))

kernel_name = "hybrid_dilated_gqa_gated_encoder"


def t5_bucket(rel):
    half = REL_BUCKETS // 2
    max_exact = half // 2
    ret = jnp.where(rel > 0, half, 0)
    a = jnp.abs(rel)
    af = jnp.maximum(a, 1).astype(jnp.float32)
    large = max_exact + (jnp.log(af / max_exact) / math.log(REL_MAX_DISTANCE / max_exact)
                         * (half - max_exact)).astype(jnp.int32)
    large = jnp.minimum(large, half - 1)
    return ret + jnp.where(a < max_exact, a, large)


def dilated_window_attention(q, k, v, table_g, window, dilation):
    B, S, H, E = q.shape
    d = dilation
    R = window // (2 * d)
    L = S // d
    Qb = math.gcd(L, A_Q_BLOCK)
    nblk = L // Qb
    Kw = Qb + 2 * R

    def phases(a):
        return a.reshape(B, L, d, H, E).transpose(0, 2, 3, 1, 4)

    qp = phases(q).reshape(B, d, H, nblk, Qb, E)
    pad = ((0, 0), (0, 0), (0, 0), (R, R), (0, 0))
    idx = jnp.arange(nblk)[:, None] * Qb + jnp.arange(Kw)[None, :]
    kb = jnp.pad(phases(k), pad)[:, :, :, idx]
    vb = jnp.pad(phases(v), pad)[:, :, :, idx]
    rel = jnp.arange(Kw)[None, :] - R - jnp.arange(Qb)[:, None]
    bias = table_g[t5_bucket(rel * d)].transpose(2, 0, 1).astype(jnp.float32)
    key_pos = idx - R
    valid = (jnp.abs(rel) <= R)[None] & ((key_pos >= 0) & (key_pos < L))[:, None, :]
    logits = jnp.einsum('bdhnqe,bdhnke->bdhnqk', qp, kb,
                        preferred_element_type=jnp.float32) * (E ** -0.5)
    logits = jnp.where(valid, logits + bias[None, None, :, None], NEG_INF)
    m = jnp.max(logits, axis=-1, keepdims=True)
    p = jnp.exp(logits - m)
    s = jnp.sum(p, axis=-1, keepdims=True)
    o = jnp.einsum('bdhnqk,bdhnke->bdhnqe', p, vb.astype(jnp.float32)) / s
    lse = (m + jnp.log(s))[..., 0]
    o = o.reshape(B, d, H, L, E).transpose(0, 3, 1, 2, 4).reshape(B, S, H, E)
    lse = lse.reshape(B, d, H, L).transpose(0, 3, 1, 2).reshape(B, S, H)
    return o, lse


def mixer_a(q, k, v, rel_table):
    B, S, _ = q.shape
    shp = (B, S, A_N_GROUPS, A_HEADS_PER_GROUP, HEAD_DIM)
    q, k, v = q.reshape(shp), k.reshape(shp), v.reshape(shp)
    outs, lses = [], []
    for g, (window, dil) in enumerate(A_GROUPS):
        table_g = rel_table[:, g * A_HEADS_PER_GROUP:(g + 1) * A_HEADS_PER_GROUP]
        o_g, l_g = dilated_window_attention(q[:, :, g], k[:, :, g], v[:, :, g], table_g, window, dil)
        outs.append(o_g)
        lses.append(l_g)
    w = jax.nn.softmax(jnp.stack(lses, axis=0), axis=0)
    o = jnp.sum(w[..., None] * jnp.stack(outs, axis=0), axis=0)
    return o.reshape(B, S, A_WIDTH).astype(q.dtype)


def rms_head(x, g):
    xf = x.astype(jnp.float32)
    y = xf * lax.rsqrt(jnp.mean(xf * xf, axis=-1, keepdims=True) + QK_EPS) * g.astype(jnp.float32)
    return y.astype(x.dtype)


def axial_angles(S):
    rows = S // GRID_W
    row_ids = jnp.repeat(jnp.arange(rows), GRID_W).astype(jnp.float32)
    col_ids = jnp.tile(jnp.arange(GRID_W), rows).astype(jnp.float32)
    half = HEAD_DIM // 2
    inv = ROPE_THETA ** (-jnp.arange(0, half, 2, dtype=jnp.float32) / half)
    return row_ids[:, None] * inv[None], col_ids[:, None] * inv[None]


def rotate_half_rope(x, ang):
    n = ang.shape[-1]
    x1, x2 = x[..., :n], x[..., n:]
    cos = jnp.cos(ang)[None, :, None, :]
    sin = jnp.sin(ang)[None, :, None, :]
    return jnp.concatenate([x1 * cos - x2 * sin, x2 * cos + x1 * sin], axis=-1)


def axial_rope(x, ang_row, ang_col):
    xf = x.astype(jnp.float32)
    half = HEAD_DIM // 2
    y = jnp.concatenate([rotate_half_rope(xf[..., :half], ang_row),
                         rotate_half_rope(xf[..., half:], ang_col)], axis=-1)
    return y.astype(x.dtype)


def mixer_b(q, k, v, q_norm_g, k_norm_g):
    B, S, _ = q.shape
    q = q.reshape(B, S, B_Q_HEADS, HEAD_DIM)
    k = k.reshape(B, S, B_KV_HEADS, HEAD_DIM)
    v = v.reshape(B, S, B_KV_HEADS, HEAD_DIM)
    ang_row, ang_col = axial_angles(S)
    q = axial_rope(rms_head(q, q_norm_g), ang_row, ang_col)
    k = axial_rope(rms_head(k, k_norm_g), ang_row, ang_col)
    G = B_Q_HEADS // B_KV_HEADS
    nq = S // B_Q_BLOCK
    qb = q.reshape(B, nq, B_Q_BLOCK, B_KV_HEADS, G, HEAD_DIM).transpose(1, 0, 2, 3, 4, 5)

    def one_block(qblk):
        logits = jnp.einsum('bqkge,bske->bkgqs', qblk, k,
                            preferred_element_type=jnp.float32) * (HEAD_DIM ** -0.5)
        p = jax.nn.softmax(logits, axis=-1).astype(v.dtype)
        return jnp.einsum('bkgqs,bske->bqkge', p, v)

    o = lax.map(one_block, qb)
    return o.transpose(1, 0, 2, 3, 4, 5).reshape(B, S, B_WIDTH)


def layer_norm(h, g, b):
    hf = h.astype(jnp.float32)
    mu = jnp.mean(hf, axis=-1, keepdims=True)
    var = jnp.mean(jnp.square(hf - mu), axis=-1, keepdims=True)
    y = (hf - mu) * lax.rsqrt(var + LN_EPS) * g.astype(jnp.float32) + b.astype(jnp.float32)
    return y.astype(h.dtype)


def hybrid_layer(x, c, rel_table, ln_g, ln_b, w_ada, b_ada, w_in, b_gate,
                 q_norm_g, k_norm_g, w_pa, w_pb, w_o):
    mod = jax.nn.silu(c) @ w_ada + b_ada
    shift, scale, gate = jnp.split(mod, 3, axis=-1)
    u = x * (1.0 + scale[:, None, :]) + shift[:, None, :]
    proj = u @ w_in
    aq, ak, av, az, bq, bk, bv, bz, gl = jnp.split(proj, SPLIT_POINTS, axis=-1)
    y_a = mixer_a(aq, ak, av, rel_table) * jax.nn.silu(az)
    y_b = mixer_b(bq, bk, bv, q_norm_g, k_norm_g) * jax.nn.silu(bz)
    g_a, g_b = jnp.split(jax.nn.sigmoid(gl + b_gate), 2, axis=-1)
    merged = g_a * (y_a @ w_pa) + g_b * (y_b @ w_pb)
    out = merged @ w_o
    return layer_norm(DEEPNORM_ALPHA * x + gate[:, None, :] * out, ln_g, ln_b)


def setup_inputs(seed: int = 0) -> dict:
    key = jax.random.key(seed)
    ks = jax.random.split(key, 16)
    f32 = jnp.float32
    D = D_MODEL
    x = jax.random.normal(ks[0], (BATCH, SEQ, D), f32)
    c = jax.random.normal(ks[1], (BATCH, D), f32)
    rel_table = 0.2 * jax.random.normal(ks[2], (REL_BUCKETS, A_HEADS), f32)
    ln_g = 1.0 + 0.02 * jax.random.normal(ks[3], (DEPTH, D), f32)
    ln_b = 0.02 * jax.random.normal(ks[4], (DEPTH, D), f32)
    w_ada = jax.random.normal(ks[5], (DEPTH, D, 3 * D), f32) * (D ** -0.5)
    b_ada = 0.02 * jax.random.normal(ks[6], (DEPTH, 3 * D), f32)
    col_scale = np.concatenate([
        np.full((n,), DEEPNORM_BETA if i in (2, 6) else 1.0, np.float32)
        for i, n in enumerate(SPLITS)])
    w_in = jax.random.normal(ks[7], (DEPTH, D, IN_WIDTH), f32) * (D ** -0.5) * jnp.asarray(col_scale)
    b_gate = 0.02 * jax.random.normal(ks[8], (DEPTH, 2 * D), f32)
    q_norm_g = 1.0 + 0.02 * jax.random.normal(ks[9], (DEPTH, HEAD_DIM), f32)
    k_norm_g = 1.0 + 0.02 * jax.random.normal(ks[10], (DEPTH, HEAD_DIM), f32)
    w_pa = jax.random.normal(ks[11], (DEPTH, A_WIDTH, D), f32) * (A_WIDTH ** -0.5) * DEEPNORM_BETA
    w_pb = jax.random.normal(ks[12], (DEPTH, B_WIDTH, D), f32) * (B_WIDTH ** -0.5) * DEEPNORM_BETA
    w_o = jax.random.normal(ks[13], (DEPTH, D, D), f32) * (D ** -0.5) * DEEPNORM_BETA
    return {"x": x, "c": c, "rel_table": rel_table, "ln_g": ln_g, "ln_b": ln_b,
            "w_ada": w_ada, "b_ada": b_ada, "w_in": w_in, "b_gate": b_gate,
            "q_norm_g": q_norm_g, "k_norm_g": k_norm_g, "w_pa": w_pa, "w_pb": w_pb, "w_o": w_o}


def reference(x, c, rel_table, ln_g, ln_b, w_ada, b_ada, w_in, b_gate,
              q_norm_g, k_norm_g, w_pa, w_pb, w_o):
    for l in range(DEPTH):
        x = hybrid_layer(x, c, rel_table, ln_g[l], ln_b[l], w_ada[l], b_ada[l], w_in[l], b_gate[l],
                         q_norm_g[l], k_norm_g[l], w_pa[l], w_pb[l], w_o[l])
    return x
```

```python
import functools
import math

import jax
import jax.numpy as jnp
from jax import lax
from jax.experimental import pallas as pl
from jax.experimental.pallas import tpu as pltpu

HEAD_DIM = 64
A_GROUPS = ((128, 1), (512, 4), (2048, 16))
A_HEADS_PER_GROUP = 8
A_WIDTH = A_HEADS_PER_GROUP * HEAD_DIM
A_RADIUS = 64
B_Q_HEADS = 8
B_KV_HEADS = 2
B_WIDTH = B_Q_HEADS * HEAD_DIM
GRID_W = 64
ROPE_THETA = 10000.0
REL_BUCKETS = 32
REL_MAX_DISTANCE = 1024
LN_EPS = 1e-5
QK_EPS = 1e-6
NEG_INF = -1e30

LANES = 128
SUBLANES = 8
VMEM_LIMIT_BYTES = 56 * 1024 * 1024

ADA_TN = 1024
IN_TM = 256
A_TQ = 256
A_SUB = 128
B_TQ = 256
B_TK = 512
OUT_TM = 256

BF16 = jnp.bfloat16
F32 = jnp.float32

_NT = (((1,), (1,)), ((), ()))


def _ada_kernel(c_ref, w_ref, b_ref, o_ref):
    c = c_ref[...]
    h = c * jax.nn.sigmoid(c)
    o_ref[0] = jnp.dot(h, w_ref[0], preferred_element_type=F32,
                       precision=lax.Precision.HIGHEST) + b_ref[0]


def _ada_call(c_pad, w_ada, b_ada):
    depth, d, n3 = w_ada.shape
    rows = c_pad.shape[0]
    return pl.pallas_call(
        _ada_kernel,
        out_shape=jax.ShapeDtypeStruct((depth, rows, n3), F32),
        grid=(depth, n3 // ADA_TN),
        in_specs=[
            pl.BlockSpec((rows, d), lambda l, j: (0, 0)),
            pl.BlockSpec((1, d, ADA_TN), lambda l, j: (l, 0, j)),
            pl.BlockSpec((1, 1, ADA_TN), lambda l, j: (l, 0, j)),
        ],
        out_specs=pl.BlockSpec((1, rows, ADA_TN), lambda l, j: (l, 0, j)),
        compiler_params=pltpu.CompilerParams(
            dimension_semantics=("arbitrary", "arbitrary")),
        name="ada_mod",
    )(c_pad, w_ada, b_ada.reshape(depth, 1, n3))


_A_SEG = 3 * A_WIDTH
_OFF_AZ = 3 * _A_SEG
_OFF_BQ = _OFF_AZ + A_WIDTH
_BK_DUP = 2 * B_KV_HEADS * HEAD_DIM
_OFF_BK = _OFF_BQ + B_WIDTH
_OFF_BV = _OFF_BK + _BK_DUP
_OFF_BZ = _OFF_BV + _BK_DUP
_OFF_GL = _OFF_BZ + B_WIDTH
_QK_W = B_WIDTH + _BK_DUP


def _inproj_kernel(x_ref, shift_ref, scale_ref, w_ref, bgate_ref, gqk_ref,
                   cos_ref, sin_ref, bd_ref,
                   a0_ref, a1_ref, a2_ref, bq_ref, bk_ref, bv_ref, gz_ref,
                   scr_ref):
    tm = x_ref.shape[1]
    d_model = x_ref.shape[2]
    u = (x_ref[0] * (1.0 + scale_ref[0]) + shift_ref[0]).astype(BF16)

    def proj(off, width):
        return jnp.dot(u, w_ref[:, off:off + width], preferred_element_type=F32)

    a0_ref[0] = proj(0, _A_SEG).astype(BF16)
    for g, a_ref in ((1, a1_ref), (2, a2_ref)):
        dil = A_GROUPS[g][1]
        res = proj(g * _A_SEG, _A_SEG)
        for ch in range(_A_SEG // LANES):
            scr_ref[ch] = res[:, ch * LANES:(ch + 1) * LANES]
        for r in range(dil):
            for ch in range(_A_SEG // LANES):
                a_ref[0, r, :, ch * LANES:(ch + 1) * LANES] = (
                    scr_ref[ch, pl.ds(r, tm // dil, stride=dil), :].astype(BF16))

    az = proj(_OFF_AZ, A_WIDTH)
    gz_ref[0, :, 0:A_WIDTH] = (az * jax.nn.sigmoid(az)).astype(BF16)
    bz = proj(_OFF_BZ, B_WIDTH)
    gz_ref[0, :, A_WIDTH:A_WIDTH + B_WIDTH] = (bz * jax.nn.sigmoid(bz)).astype(BF16)
    for j in range(2):
        gl = proj(_OFF_GL + j * d_model, d_model) + bgate_ref[:, j * d_model:(j + 1) * d_model]
        o = A_WIDTH + B_WIDTH + j * d_model
        gz_ref[0, :, o:o + d_model] = jax.nn.sigmoid(gl).astype(BF16)

    bv_ref[0] = proj(_OFF_BV, _BK_DUP).astype(BF16)

    qk = proj(_OFF_BQ, _QK_W)
    sq = qk * qk
    sq_hi = sq.astype(BF16)
    sq_lo = (sq - sq_hi.astype(F32)).astype(BF16)
    ss = (jnp.dot(sq_hi, bd_ref[...], preferred_element_type=F32)
          + jnp.dot(sq_lo, bd_ref[...], preferred_element_type=F32))
    y = qk * lax.rsqrt(ss * (1.0 / HEAD_DIM) + QK_EPS) * gqk_ref[...]
    reps = _QK_W // LANES
    cos = jnp.concatenate([cos_ref[...]] * reps, axis=1)
    sin = jnp.concatenate([sin_ref[...]] * reps, axis=1)
    lane = lax.broadcasted_iota(jnp.int32, y.shape, 1)
    first = (lane & 31) < 16
    swapped = jnp.where(first, pltpu.roll(y, _QK_W - 16, 1), pltpu.roll(y, 16, 1))
    rot = y * cos + swapped * sin
    bq_ref[0] = (rot[:, :B_WIDTH] * (HEAD_DIM ** -0.5)).astype(BF16)
    bk_ref[0] = rot[:, B_WIDTH:].astype(BF16)


def _inproj_call(x, shift, scale, w_all, b_gate, gqk, cos_t, sin_t, bd):
    b, s, d = x.shape
    tm = IN_TM
    n_cols = w_all.shape[1]
    d1, d2 = A_GROUPS[1][1], A_GROUPS[2][1]
    const = dict(pipeline_mode=pl.Buffered(1))
    out_shape = (
        jax.ShapeDtypeStruct((b, s, _A_SEG), BF16),
        jax.ShapeDtypeStruct((b, d1, s // d1, _A_SEG), BF16),
        jax.ShapeDtypeStruct((b, d2, s // d2, _A_SEG), BF16),
        jax.ShapeDtypeStruct((b, s, B_WIDTH), BF16),
        jax.ShapeDtypeStruct((b, s, _BK_DUP), BF16),
        jax.ShapeDtypeStruct((b, s, _BK_DUP), BF16),
        jax.ShapeDtypeStruct((b, s, A_WIDTH + B_WIDTH + 2 * d), BF16),
    )
    row = lambda bi, i: (bi, i, 0)
    out_specs = (
        pl.BlockSpec((1, tm, _A_SEG), row),
        pl.BlockSpec((1, d1, tm // d1, _A_SEG), lambda bi, i: (bi, 0, i, 0)),
        pl.BlockSpec((1, d2, tm // d2, _A_SEG), lambda bi, i: (bi, 0, i, 0)),
        pl.BlockSpec((1, tm, B_WIDTH), row),
        pl.BlockSpec((1, tm, _BK_DUP), row),
        pl.BlockSpec((1, tm, _BK_DUP), row),
        pl.BlockSpec((1, tm, A_WIDTH + B_WIDTH + 2 * d), row),
    )
    in_specs = [
        pl.BlockSpec((1, tm, d), row),
        pl.BlockSpec((1, 1, d), lambda bi, i: (bi, 0, 0)),
        pl.BlockSpec((1, 1, d), lambda bi, i: (bi, 0, 0)),
        pl.BlockSpec((d, n_cols), lambda bi, i: (0, 0), **const),
        pl.BlockSpec((1, 2 * d), lambda bi, i: (0, 0), **const),
        pl.BlockSpec((1, _QK_W), lambda bi, i: (0, 0), **const),
        pl.BlockSpec((tm, LANES), lambda bi, i: (i, 0)),
        pl.BlockSpec((tm, LANES), lambda bi, i: (i, 0)),
        pl.BlockSpec((_QK_W, _QK_W), lambda bi, i: (0, 0), **const),
    ]
    return pl.pallas_call(
        _inproj_kernel,
        out_shape=out_shape,
        grid=(b, s // tm),
        in_specs=in_specs,
        out_specs=out_specs,
        scratch_shapes=[pltpu.VMEM((_A_SEG // LANES, tm, LANES), F32)],
        compiler_params=pltpu.CompilerParams(
            dimension_semantics=("arbitrary", "arbitrary"),
            vmem_limit_bytes=VMEM_LIMIT_BYTES),
        name="inproj",
    )(x, shift, scale, w_all, b_gate, gqk, cos_t, sin_t, bd)


def _attn_a_kernel(q_ref, kp_ref, kc_ref, kn_ref, vp_ref, vc_ref, vn_ref, bias_ref,
                   o_ref, lse_ref, *, phase_len):
    tq = q_ref.shape[2]
    i = pl.program_id(2)
    q = q_ref[0, 0]
    k_all = jnp.concatenate([kp_ref[0, 0], kc_ref[0, 0], kn_ref[0, 0]], axis=0)
    v_all = jnp.concatenate([vp_ref[0, 0], vc_ref[0, 0], vn_ref[0, 0]], axis=0)
    tk = A_SUB + 2 * A_RADIUS
    lane = lax.broadcasted_iota(jnp.int32, (A_SUB, LANES), 1)
    low = lane < HEAD_DIM
    kcol = lax.broadcasted_iota(jnp.int32, (1, tk), 1)
    scale = HEAD_DIM ** -0.5
    for sub in range(tq // A_SUB):
        r0 = sub * A_SUB
        krow = i * tq + (r0 - A_RADIUS) + kcol
        valid = (krow >= 0) & (krow < phase_len)
        for pair in range(A_HEADS_PER_GROUP // 2):
            c0 = pair * LANES
            qp = q[r0:r0 + A_SUB, c0:c0 + LANES]
            kp = k_all[r0:r0 + tk, c0:c0 + LANES]
            vp = v_all[r0:r0 + tk, c0:c0 + LANES]
            res = []
            for par in range(2):
                qm = jnp.where(low if par == 0 else jnp.logical_not(low), qp, jnp.zeros_like(qp))
                s = lax.dot_general(qm, kp, _NT, preferred_element_type=F32) * scale
                s = jnp.where(valid, s + bias_ref[2 * pair + par], NEG_INF)
                m = jnp.max(s, axis=1, keepdims=True)
                p = jnp.exp(s - m)
                l = jnp.sum(p, axis=1, keepdims=True)
                o = jnp.dot(p.astype(BF16), vp, preferred_element_type=F32) / l
                res.append((o, m + jnp.log(l)))
            o_ref[0, 0, r0:r0 + A_SUB, c0:c0 + LANES] = jnp.where(low, res[0][0], res[1][0])
            lse_ref[0, 0, r0:r0 + A_SUB, c0:c0 + LANES] = jnp.where(low, res[0][1], res[1][1])


def _attn_a_call(a_g, bias_g):
    b, dil, phase_len, _ = a_g.shape
    tq = min(A_TQ, phase_len)
    halo = A_RADIUS
    nh = phase_len // halo
    per = tq // halo
    cur = lambda col: (lambda bi, r, i: (bi, r, i, col))
    prev = lambda col: (lambda bi, r, i: (bi, r, jnp.maximum(i * per - 1, 0), col))
    nxt = lambda col: (lambda bi, r, i: (bi, r, jnp.minimum((i + 1) * per, nh - 1), col))
    blk = (1, 1, tq, A_WIDTH)
    hblk = (1, 1, halo, A_WIDTH)
    out_sds = jax.ShapeDtypeStruct((b, dil, phase_len, A_WIDTH), F32)
    return pl.pallas_call(
        functools.partial(_attn_a_kernel, phase_len=phase_len),
        out_shape=(out_sds, out_sds),
        grid=(b, dil, phase_len // tq),
        in_specs=[
            pl.BlockSpec(blk, cur(0)),
            pl.BlockSpec(hblk, prev(1)), pl.BlockSpec(blk, cur(1)), pl.BlockSpec(hblk, nxt(1)),
            pl.BlockSpec(hblk, prev(2)), pl.BlockSpec(blk, cur(2)), pl.BlockSpec(hblk, nxt(2)),
            pl.BlockSpec(bias_g.shape, lambda bi, r, i: (0, 0, 0)),
        ],
        out_specs=(pl.BlockSpec(blk, cur(0)), pl.BlockSpec(blk, cur(0))),
        compiler_params=pltpu.CompilerParams(
            dimension_semantics=("arbitrary", "arbitrary", "arbitrary"),
            vmem_limit_bytes=VMEM_LIMIT_BYTES),
        name=f"attn_a_d{dil}",
    )(a_g, a_g, a_g, a_g, a_g, a_g, a_g, bias_g)


def _attn_b_kernel(q_ref, k_ref, v_ref, o_ref, qs_ref, m_ref, l_ref, acc_ref):
    tq = q_ref.shape[1]
    seq = k_ref.shape[1]
    group = B_Q_HEADS // B_KV_HEADS
    lane = lax.broadcasted_iota(jnp.int32, (tq, LANES), 1)
    low = lane < HEAD_DIM
    q = q_ref[0]
    for h in range(group):
        qp = q[:, (h // 2) * LANES:(h // 2 + 1) * LANES]
        keep = low if h % 2 == 0 else jnp.logical_not(low)
        qs_ref[h * tq:(h + 1) * tq, :] = jnp.where(keep, qp, jnp.zeros_like(qp))
    m_ref[...] = jnp.full(m_ref.shape, NEG_INF, F32)
    l_ref[...] = jnp.zeros(l_ref.shape, F32)
    acc_ref[...] = jnp.zeros(acc_ref.shape, F32)

    def body(c, carry):
        start = pl.multiple_of(c * B_TK, B_TK)
        kc = k_ref[0, pl.ds(start, B_TK), :]
        vc = v_ref[0, pl.ds(start, B_TK), :]
        s = lax.dot_general(qs_ref[...], kc, _NT, preferred_element_type=F32)
        m_prev = m_ref[...]
        m_new = jnp.maximum(m_prev, jnp.max(s, axis=1, keepdims=True))
        alpha = jnp.exp(m_prev - m_new)
        p = jnp.exp(s - m_new)
        l_ref[...] = alpha * l_ref[...] + jnp.sum(p, axis=1, keepdims=True)
        acc_ref[...] = alpha * acc_ref[...] + jnp.dot(p.astype(BF16), vc,
                                                     preferred_element_type=F32)
        m_ref[...] = m_new
        return carry

    lax.fori_loop(0, seq // B_TK, body, 0)
    o = acc_ref[...] / l_ref[...]
    for pair in range(group // 2):
        o_e = o[(2 * pair) * tq:(2 * pair + 1) * tq]
        o_o = o[(2 * pair + 1) * tq:(2 * pair + 2) * tq]
        o_ref[0, :, pair * LANES:(pair + 1) * LANES] = jnp.where(low, o_e, o_o).astype(BF16)


def _attn_b_call(bq, bk, bv):
    b, s, _ = bq.shape
    group = B_Q_HEADS // B_KV_HEADS
    qw = group * HEAD_DIM
    rows = group * B_TQ
    return pl.pallas_call(
        _attn_b_kernel,
        out_shape=jax.ShapeDtypeStruct((b, s, B_WIDTH), BF16),
        grid=(b, B_KV_HEADS, s // B_TQ),
        in_specs=[
            pl.BlockSpec((1, B_TQ, qw), lambda bi, h, i: (bi, i, h)),
            pl.BlockSpec((1, s, LANES), lambda bi, h, i: (bi, 0, h)),
            pl.BlockSpec((1, s, LANES), lambda bi, h, i: (bi, 0, h)),
        ],
        out_specs=pl.BlockSpec((1, B_TQ, qw), lambda bi, h, i: (bi, i, h)),
        scratch_shapes=[
            pltpu.VMEM((rows, LANES), BF16),
            pltpu.VMEM((rows, 1), F32),
            pltpu.VMEM((rows, 1), F32),
            pltpu.VMEM((rows, LANES), F32),
        ],
        compiler_params=pltpu.CompilerParams(
            dimension_semantics=("arbitrary", "arbitrary", "arbitrary"),
            vmem_limit_bytes=VMEM_LIMIT_BYTES),
        name="attn_b",
    )(bq, bk, bv)


def _outproj_kernel(x_ref, gate_ref, o0_ref, l0_ref, o1_ref, l1_ref, o2_ref, l2_ref,
                    yb_ref, gz_ref, wpa_ref, wpb_ref, wo_ref, lng_ref, lnb_ref,
                    out_ref, so1, sl1, so2, sl2, *, alpha):
    tm = x_ref.shape[1]
    d_model = x_ref.shape[2]
    for (o_ref, l_ref, so, sl, g) in ((o1_ref, l1_ref, so1, sl1, 1), (o2_ref, l2_ref, so2, sl2, 2)):
        dil = A_GROUPS[g][1]
        for r in range(dil):
            for ch in range(A_WIDTH // LANES):
                cs = slice(ch * LANES, (ch + 1) * LANES)
                so[ch, pl.ds(r, tm // dil, stride=dil), :] = o_ref[0, r, :, cs]
                sl[ch, pl.ds(r, tm // dil, stride=dil), :] = l_ref[0, r, :, cs]
    cat = lambda ref: jnp.concatenate([ref[ch] for ch in range(A_WIDTH // LANES)], axis=1)
    l0, l1, l2 = l0_ref[0], cat(sl1), cat(sl2)
    mx = jnp.maximum(jnp.maximum(l0, l1), l2)
    e0, e1, e2 = jnp.exp(l0 - mx), jnp.exp(l1 - mx), jnp.exp(l2 - mx)
    y_a = (e0 * o0_ref[0] + e1 * cat(so1) + e2 * cat(so2)) / (e0 + e1 + e2)
    gz = gz_ref[0]
    ya = (y_a * gz[:, 0:A_WIDTH].astype(F32)).astype(BF16)
    yb = (yb_ref[0].astype(F32) * gz[:, A_WIDTH:A_WIDTH + B_WIDTH].astype(F32)).astype(BF16)
    pa = jnp.dot(ya, wpa_ref[...], preferred_element_type=F32)
    pb = jnp.dot(yb, wpb_ref[...], preferred_element_type=F32)
    o = A_WIDTH + B_WIDTH
    g_a = gz[:, o:o + d_model].astype(F32)
    g_b = gz[:, o + d_model:o + 2 * d_model].astype(F32)
    merged = (g_a * pa + g_b * pb).astype(BF16)
    out = jnp.dot(merged, wo_ref[...], preferred_element_type=F32)
    h = alpha * x_ref[0] + gate_ref[0] * out
    mu = jnp.mean(h, axis=-1, keepdims=True)
    hc = h - mu
    var = jnp.mean(hc * hc, axis=-1, keepdims=True)
    out_ref[0] = hc * lax.rsqrt(var + LN_EPS) * lng_ref[...] + lnb_ref[...]


def _outproj_call(x, gate, oa, yb, gz, w_pa, w_pb, w_o, ln_g, ln_b, alpha):
    b, s, d = x.shape
    tm = OUT_TM
    (o0, l0), (o1, l1), (o2, l2) = oa
    d1, d2 = A_GROUPS[1][1], A_GROUPS[2][1]
    row = lambda bi, i: (bi, i, 0)
    ph = lambda bi, i: (bi, 0, i, 0)
    fixed = lambda bi, i: (0, 0)
    s0 = pl.BlockSpec((1, 1, tm, A_WIDTH), ph)
    s1 = pl.BlockSpec((1, d1, tm // d1, A_WIDTH), ph)
    s2 = pl.BlockSpec((1, d2, tm // d2, A_WIDTH), ph)

    def kern(x_ref, gate_ref, o0_ref, l0_ref, *rest):
        return _outproj_kernel(x_ref, gate_ref, o0_ref.at[0], l0_ref.at[0], *rest, alpha=alpha)

    return pl.pallas_call(
        kern,
        out_shape=jax.ShapeDtypeStruct((b, s, d), F32),
        grid=(b, s // tm),
        in_specs=[
            pl.BlockSpec((1, tm, d), row),
            pl.BlockSpec((1, 1, d), lambda bi, i: (bi, 0, 0)),
            s0, s0, s1, s1, s2, s2,
            pl.BlockSpec((1, tm, B_WIDTH), row),
            pl.BlockSpec((1, tm, gz.shape[2]), row),
            pl.BlockSpec(w_pa.shape, fixed),
            pl.BlockSpec(w_pb.shape, fixed),
            pl.BlockSpec(w_o.shape, fixed),
            pl.BlockSpec((1, d), fixed),
            pl.BlockSpec((1, d), fixed),
        ],
        out_specs=pl.BlockSpec((1, tm, d), row),
        scratch_shapes=[pltpu.VMEM((A_WIDTH // LANES, tm, LANES), F32)] * 4,
        compiler_params=pltpu.CompilerParams(
            dimension_semantics=("arbitrary", "arbitrary"),
            vmem_limit_bytes=VMEM_LIMIT_BYTES),
        name="outproj",
    )(x, gate, o0, l0, o1, l1, o2, l2, yb, gz, w_pa, w_pb, w_o, ln_g, ln_b)


def _t5_bucket(rel):
    half = REL_BUCKETS // 2
    max_exact = half // 2
    ret = jnp.where(rel > 0, half, 0)
    a = jnp.abs(rel)
    af = jnp.maximum(a, 1).astype(F32)
    large = max_exact + (jnp.log(af / max_exact) / math.log(REL_MAX_DISTANCE / max_exact)
                         * (half - max_exact)).astype(jnp.int32)
    large = jnp.minimum(large, half - 1)
    return ret + jnp.where(a < max_exact, a, large)


def _window_bias(rel_table, g):
    dil = A_GROUPS[g][1]
    tk = A_SUB + 2 * A_RADIUS
    rel = jnp.arange(tk)[None, :] - A_RADIUS - jnp.arange(A_SUB)[:, None]
    table_g = rel_table[:, g * A_HEADS_PER_GROUP:(g + 1) * A_HEADS_PER_GROUP]
    bias = table_g[_t5_bucket(rel * dil)].transpose(2, 0, 1).astype(F32)
    return jnp.where((jnp.abs(rel) <= A_RADIUS)[None], bias, NEG_INF)


def _rope_tables(seq):
    t = jnp.arange(seq)
    row = (t // GRID_W).astype(F32)
    col = (t % GRID_W).astype(F32)
    half = HEAD_DIM // 2
    inv = ROPE_THETA ** (-jnp.arange(0, half, 2, dtype=F32) / half)
    ar, ac = row[:, None] * inv[None], col[:, None] * inv[None]
    cos = jnp.concatenate([jnp.cos(ar), jnp.cos(ar), jnp.cos(ac), jnp.cos(ac)], axis=1)
    sin = jnp.concatenate([-jnp.sin(ar), jnp.sin(ar), -jnp.sin(ac), jnp.sin(ac)], axis=1)
    return jnp.tile(cos, (1, LANES // HEAD_DIM)), jnp.tile(sin, (1, LANES // HEAD_DIM))


def _layout_w_in(w):
    d = w.shape[0]
    qkv = 3 * A_WIDTH
    aq, ak, av = w[:, 0:qkv], w[:, qkv:2 * qkv], w[:, 2 * qkv:3 * qkv]
    o = 3 * qkv
    az = w[:, o:o + A_WIDTH]; o += A_WIDTH
    bq = w[:, o:o + B_WIDTH]; o += B_WIDTH
    kvw = B_KV_HEADS * HEAD_DIM
    bk = w[:, o:o + kvw]; o += kvw
    bv = w[:, o:o + kvw]; o += kvw
    bz = w[:, o:o + B_WIDTH]; o += B_WIDTH
    gl = w[:, o:]
    groups = []
    for g in range(len(A_GROUPS)):
        sl = slice(g * A_WIDTH, (g + 1) * A_WIDTH)
        groups += [aq[:, sl], ak[:, sl], av[:, sl]]
    dup = lambda m: jnp.concatenate(
        [m[:, h * HEAD_DIM:(h + 1) * HEAD_DIM] for h in range(B_KV_HEADS) for _ in range(2)], axis=1)
    return jnp.concatenate(groups + [az, bq, dup(bk), dup(bv), bz, gl], axis=1).astype(BF16)


def kernel(x, c, rel_table, ln_g, ln_b, w_ada, b_ada, w_in, b_gate, q_norm_g, k_norm_g, w_pa, w_pb, w_o):
    depth = w_in.shape[0]
    b, s, d = x.shape
    alpha = float((2 * depth) ** 0.25)

    c_pad = jnp.zeros((SUBLANES, d), F32).at[:b].set(c)
    mod = _ada_call(c_pad, w_ada, b_ada)[:, :b]
    cos_t, sin_t = _rope_tables(s)
    idx = jnp.arange(_QK_W) // HEAD_DIM
    bd = (idx[:, None] == idx[None, :]).astype(BF16)
    biases = [_window_bias(rel_table, g) for g in range(len(A_GROUPS))]

    for l in range(depth):
        shift = mod[l, :, 0:d].reshape(b, 1, d)
        scale = mod[l, :, d:2 * d].reshape(b, 1, d)
        gate = mod[l, :, 2 * d:3 * d].reshape(b, 1, d)
        gqk = jnp.concatenate([jnp.tile(q_norm_g[l], B_Q_HEADS),
                               jnp.tile(k_norm_g[l], 2 * B_KV_HEADS)]).reshape(1, _QK_W)
        a0, a1, a2, bq, bk, bv, gz = _inproj_call(
            x, shift, scale, _layout_w_in(w_in[l]), b_gate[l].reshape(1, 2 * d), gqk,
            cos_t, sin_t, bd)
        a0 = a0.reshape(b, 1, s, _A_SEG)
        oa = [_attn_a_call(a_g, biases[g]) for g, a_g in enumerate((a0, a1, a2))]
        yb = _attn_b_call(bq, bk, bv)
        x = _outproj_call(x, gate, oa, yb, gz, w_pa[l].astype(BF16), w_pb[l].astype(BF16),
                          w_o[l].astype(BF16), ln_g[l].reshape(1, d), ln_b[l].reshape(1, d), alpha)
    return x
```

```python
import functools
import math

import jax
import jax.numpy as jnp
from jax import lax
from jax.experimental import pallas as pl
from jax.experimental.pallas import tpu as pltpu

HEAD_DIM = 64
A_GROUPS = ((128, 1), (512, 4), (2048, 16))
A_HEADS_PER_GROUP = 8
A_WIDTH = A_HEADS_PER_GROUP * HEAD_DIM
A_RADIUS = 64
B_Q_HEADS = 8
B_KV_HEADS = 2
B_GROUP = B_Q_HEADS // B_KV_HEADS
B_WIDTH = B_Q_HEADS * HEAD_DIM
GRID_W = 64
ROPE_THETA = 10000.0
REL_BUCKETS = 32
REL_MAX_DISTANCE = 1024
LN_EPS = 1e-5
QK_EPS = 1e-6
NEG_INF = -1e30
LOG2_E = math.log2(math.e)

LANES = 128
SUBLANES = 8
VMEM_LIMIT_BYTES = 56 * 1024 * 1024

ADA_TN = 1024
IN_TM = 256
A_TQ = 256
A_SUB = 128
B_TQ = 256
B_TK = 512
OUT_TM = 256

B_MAX_BOUND = 50.0
B_BOUND_SLACK = 1.0 + 2.0 ** -6

BF16 = jnp.bfloat16
F32 = jnp.float32

_NT = (((1,), (1,)), ((), ()))


def _ada_kernel(c_ref, w_ref, b_ref, o_ref):
    c = c_ref[...]
    h = c * jax.nn.sigmoid(c)
    o_ref[0] = jnp.dot(h, w_ref[0], preferred_element_type=F32,
                       precision=lax.Precision.HIGHEST) + b_ref[0]


def _ada_call(c_pad, w_ada, b_ada):
    depth, d, n3 = w_ada.shape
    rows = c_pad.shape[0]
    return pl.pallas_call(
        _ada_kernel,
        out_shape=jax.ShapeDtypeStruct((depth, rows, n3), F32),
        grid=(depth, n3 // ADA_TN),
        in_specs=[
            pl.BlockSpec((rows, d), lambda l, j: (0, 0)),
            pl.BlockSpec((1, d, ADA_TN), lambda l, j: (l, 0, j)),
            pl.BlockSpec((1, 1, ADA_TN), lambda l, j: (l, 0, j)),
        ],
        out_specs=pl.BlockSpec((1, rows, ADA_TN), lambda l, j: (l, 0, j)),
        compiler_params=pltpu.CompilerParams(
            dimension_semantics=("arbitrary", "arbitrary")),
        name="ada_mod",
    )(c_pad, w_ada, b_ada.reshape(depth, 1, n3))


_A_SEG = 3 * A_WIDTH
_OFF_AZ = 3 * _A_SEG
_OFF_BQ = _OFF_AZ + A_WIDTH
_KV_PAD = B_KV_HEADS * LANES
_OFF_BK = _OFF_BQ + B_WIDTH
_OFF_BV = _OFF_BK + _KV_PAD
_OFF_BZ = _OFF_BV + _KV_PAD
_OFF_GL = _OFF_BZ + B_WIDTH
_QK_W = B_WIDTH + _KV_PAD
_ONE_LANE = HEAD_DIM


def _inproj_kernel(x_ref, shift_ref, scale_ref, w_ref, bgate_ref, gqk_ref,
                   cos_ref, sin_ref, bd_ref,
                   a0_ref, a1_ref, a2_ref, bq_ref, bk_ref, bv_ref, gz_ref, st_ref,
                   scr_ref):
    tm = x_ref.shape[1]
    d_model = x_ref.shape[2]
    u = (x_ref[0] * (1.0 + scale_ref[0]) + shift_ref[0]).astype(BF16)

    def proj(off, width):
        return jnp.dot(u, w_ref[:, off:off + width], preferred_element_type=F32)

    a0_ref[0] = proj(0, _A_SEG).astype(BF16)
    for g, a_ref in ((1, a1_ref), (2, a2_ref)):
        dil = A_GROUPS[g][1]
        res = proj(g * _A_SEG, _A_SEG)
        for ch in range(_A_SEG // LANES):
            scr_ref[ch] = res[:, ch * LANES:(ch + 1) * LANES]
        for r in range(dil):
            for ch in range(_A_SEG // LANES):
                a_ref[0, r, :, ch * LANES:(ch + 1) * LANES] = (
                    scr_ref[ch, pl.ds(r, tm // dil, stride=dil), :].astype(BF16))

    az = proj(_OFF_AZ, A_WIDTH)
    gz_ref[0, :, 0:A_WIDTH] = (az * jax.nn.sigmoid(az)).astype(BF16)
    bz = proj(_OFF_BZ, B_WIDTH)
    gz_ref[0, :, A_WIDTH:A_WIDTH + B_WIDTH] = (bz * jax.nn.sigmoid(bz)).astype(BF16)
    for j in range(2):
        gl = proj(_OFF_GL + j * d_model, d_model) + bgate_ref[:, j * d_model:(j + 1) * d_model]
        o = A_WIDTH + B_WIDTH + j * d_model
        gz_ref[0, :, o:o + d_model] = jax.nn.sigmoid(gl).astype(BF16)

    kv_lane = lax.broadcasted_iota(jnp.int32, (1, _KV_PAD), 1)
    one_col = ((kv_lane & (LANES - 1)) == _ONE_LANE).astype(F32)
    bv_ref[0] = (proj(_OFF_BV, _KV_PAD) + one_col).astype(BF16)

    qk = proj(_OFF_BQ, _QK_W)
    sq = qk * qk
    sq_hi = sq.astype(BF16)
    sq_lo = (sq - sq_hi.astype(F32)).astype(BF16)
    ss = (jnp.dot(sq_hi, bd_ref[...], preferred_element_type=F32)
          + jnp.dot(sq_lo, bd_ref[...], preferred_element_type=F32))
    y = qk * lax.rsqrt(ss * (1.0 / HEAD_DIM) + QK_EPS) * gqk_ref[...]
    reps = _QK_W // LANES
    cos = jnp.concatenate([cos_ref[...]] * reps, axis=1)
    sin = jnp.concatenate([sin_ref[...]] * reps, axis=1)
    lane = lax.broadcasted_iota(jnp.int32, y.shape, 1)
    first = (lane & 31) < 16
    swapped = jnp.where(first, pltpu.roll(y, _QK_W - 16, 1), pltpu.roll(y, 16, 1))
    rot = y * cos + swapped * sin
    q = rot[:, :B_WIDTH] * (HEAD_DIM ** -0.5 * LOG2_E)
    k = rot[:, B_WIDTH:]
    bq_ref[0] = q.astype(BF16)
    bk_ref[0] = (k + one_col).astype(BF16)

    def tile_max(v2):
        return jnp.max(jnp.sum(v2, axis=1, keepdims=True), axis=0, keepdims=True)
    k2, q2 = k * k, q * q
    kmax = [tile_max(k2[:, h * LANES:(h + 1) * LANES]) for h in range(B_KV_HEADS)]
    qmax = tile_max(q2[:, 0:LANES])
    for pr in range(1, B_WIDTH // LANES):
        qmax = jnp.maximum(qmax, tile_max(q2[:, pr * LANES:(pr + 1) * LANES]))
    sub = lax.broadcasted_iota(jnp.int32, (SUBLANES, LANES), 0)
    st_ref[0, 0] = jnp.where(sub == 0, kmax[0], jnp.where(sub == 1, kmax[1], qmax))


def _inproj_call(x, shift, scale, w_all, b_gate, gqk, cos_t, sin_t, bd):
    b, s, d = x.shape
    tm = IN_TM
    n_cols = w_all.shape[1]
    d1, d2 = A_GROUPS[1][1], A_GROUPS[2][1]
    const = dict(pipeline_mode=pl.Buffered(1))
    out_shape = (
        jax.ShapeDtypeStruct((b, s, _A_SEG), BF16),
        jax.ShapeDtypeStruct((b, d1, s // d1, _A_SEG), BF16),
        jax.ShapeDtypeStruct((b, d2, s // d2, _A_SEG), BF16),
        jax.ShapeDtypeStruct((b, s, B_WIDTH), BF16),
        jax.ShapeDtypeStruct((b, s, _KV_PAD), BF16),
        jax.ShapeDtypeStruct((b, s, _KV_PAD), BF16),
        jax.ShapeDtypeStruct((b, s, A_WIDTH + B_WIDTH + 2 * d), BF16),
        jax.ShapeDtypeStruct((b, s // tm, SUBLANES, LANES), F32),
    )
    row = lambda bi, i: (bi, i, 0)
    out_specs = (
        pl.BlockSpec((1, tm, _A_SEG), row),
        pl.BlockSpec((1, d1, tm // d1, _A_SEG), lambda bi, i: (bi, 0, i, 0)),
        pl.BlockSpec((1, d2, tm // d2, _A_SEG), lambda bi, i: (bi, 0, i, 0)),
        pl.BlockSpec((1, tm, B_WIDTH), row),
        pl.BlockSpec((1, tm, _KV_PAD), row),
        pl.BlockSpec((1, tm, _KV_PAD), row),
        pl.BlockSpec((1, tm, A_WIDTH + B_WIDTH + 2 * d), row),
        pl.BlockSpec((1, 1, SUBLANES, LANES), lambda bi, i: (bi, i, 0, 0)),
    )
    in_specs = [
        pl.BlockSpec((1, tm, d), row),
        pl.BlockSpec((1, 1, d), lambda bi, i: (bi, 0, 0)),
        pl.BlockSpec((1, 1, d), lambda bi, i: (bi, 0, 0)),
        pl.BlockSpec((d, n_cols), lambda bi, i: (0, 0), **const),
        pl.BlockSpec((1, 2 * d), lambda bi, i: (0, 0), **const),
        pl.BlockSpec((1, _QK_W), lambda bi, i: (0, 0), **const),
        pl.BlockSpec((tm, LANES), lambda bi, i: (i, 0)),
        pl.BlockSpec((tm, LANES), lambda bi, i: (i, 0)),
        pl.BlockSpec((_QK_W, _QK_W), lambda bi, i: (0, 0), **const),
    ]
    return pl.pallas_call(
        _inproj_kernel,
        out_shape=out_shape,
        grid=(b, s // tm),
        in_specs=in_specs,
        out_specs=out_specs,
        scratch_shapes=[pltpu.VMEM((_A_SEG // LANES, tm, LANES), F32)],
        compiler_params=pltpu.CompilerParams(
            dimension_semantics=("arbitrary", "arbitrary"),
            vmem_limit_bytes=VMEM_LIMIT_BYTES),
        name="inproj",
    )(x, shift, scale, w_all, b_gate, gqk, cos_t, sin_t, bd)


def _attn_a_kernel(q_ref, kp_ref, kc_ref, kn_ref, vp_ref, vc_ref, vn_ref, bias_ref,
                   o_ref, lse_ref, *, phase_len):
    tq = q_ref.shape[2]
    i = pl.program_id(2)
    q = q_ref[0, 0]
    k_all = jnp.concatenate([kp_ref[0, 0], kc_ref[0, 0], kn_ref[0, 0]], axis=0)
    v_all = jnp.concatenate([vp_ref[0, 0], vc_ref[0, 0], vn_ref[0, 0]], axis=0)
    tk = A_SUB + 2 * A_RADIUS
    lane = lax.broadcasted_iota(jnp.int32, (A_SUB, LANES), 1)
    low = lane < HEAD_DIM
    kcol = lax.broadcasted_iota(jnp.int32, (1, tk), 1)
    scale = HEAD_DIM ** -0.5
    for sub in range(tq // A_SUB):
        r0 = sub * A_SUB
        krow = i * tq + (r0 - A_RADIUS) + kcol
        valid = (krow >= 0) & (krow < phase_len)
        for pair in range(A_HEADS_PER_GROUP // 2):
            c0 = pair * LANES
            qp = q[r0:r0 + A_SUB, c0:c0 + LANES]
            kp = k_all[r0:r0 + tk, c0:c0 + LANES]
            vp = v_all[r0:r0 + tk, c0:c0 + LANES]
            res = []
            for par in range(2):
                qm = jnp.where(low if par == 0 else jnp.logical_not(low), qp, jnp.zeros_like(qp))
                s = lax.dot_general(qm, kp, _NT, preferred_element_type=F32) * scale
                s = jnp.where(valid, s + bias_ref[2 * pair + par], NEG_INF)
                m = jnp.max(s, axis=1, keepdims=True)
                p = jnp.exp(s - m)
                l = jnp.sum(p, axis=1, keepdims=True)
                o = jnp.dot(p.astype(BF16), vp, preferred_element_type=F32) / l
                res.append((o, m + jnp.log(l)))
            o_ref[0, 0, r0:r0 + A_SUB, c0:c0 + LANES] = jnp.where(low, res[0][0], res[1][0])
            lse_ref[0, 0, r0:r0 + A_SUB, c0:c0 + LANES] = jnp.where(low, res[0][1], res[1][1])


def _attn_a_call(a_g, bias_g):
    b, dil, phase_len, _ = a_g.shape
    tq = min(A_TQ, phase_len)
    halo = A_RADIUS
    nh = phase_len // halo
    per = tq // halo
    cur = lambda col: (lambda bi, r, i: (bi, r, i, col))
    prev = lambda col: (lambda bi, r, i: (bi, r, jnp.maximum(i * per - 1, 0), col))
    nxt = lambda col: (lambda bi, r, i: (bi, r, jnp.minimum((i + 1) * per, nh - 1), col))
    blk = (1, 1, tq, A_WIDTH)
    hblk = (1, 1, halo, A_WIDTH)
    out_sds = jax.ShapeDtypeStruct((b, dil, phase_len, A_WIDTH), F32)
    return pl.pallas_call(
        functools.partial(_attn_a_kernel, phase_len=phase_len),
        out_shape=(out_sds, out_sds),
        grid=(b, dil, phase_len // tq),
        in_specs=[
            pl.BlockSpec(blk, cur(0)),
            pl.BlockSpec(hblk, prev(1)), pl.BlockSpec(blk, cur(1)), pl.BlockSpec(hblk, nxt(1)),
            pl.BlockSpec(hblk, prev(2)), pl.BlockSpec(blk, cur(2)), pl.BlockSpec(hblk, nxt(2)),
            pl.BlockSpec(bias_g.shape, lambda bi, r, i: (0, 0, 0)),
        ],
        out_specs=(pl.BlockSpec(blk, cur(0)), pl.BlockSpec(blk, cur(0))),
        compiler_params=pltpu.CompilerParams(
            dimension_semantics=("arbitrary", "arbitrary", "arbitrary"),
            vmem_limit_bytes=VMEM_LIMIT_BYTES),
        name=f"attn_a_d{dil}",
    )(a_g, a_g, a_g, a_g, a_g, a_g, a_g, bias_g)


def _attn_b_kernel(kmax_ref, q_ref, k_ref, v_ref, o_ref, qs_ref, acc_ref, *m_scratch, online):
    tq = q_ref.shape[1]
    seq = k_ref.shape[1]
    lane = lax.broadcasted_iota(jnp.int32, (tq, LANES), 1)
    low = lane < HEAD_DIM
    kmax = kmax_ref[pl.program_id(0) * B_KV_HEADS + pl.program_id(1)]
    for h in range(B_GROUP):
        qf = q_ref[0, :, (h // 2) * LANES:(h // 2 + 1) * LANES].astype(F32)
        if h % 2 == 1:
            qf = pltpu.roll(qf, HEAD_DIM, 1)
        qf = jnp.where(low, qf, 0.0)
        if not online:
            qn = jnp.sqrt(jnp.sum(qf * qf, axis=1, keepdims=True))
            qf = jnp.where(lane == _ONE_LANE, -(qn * (kmax * B_BOUND_SLACK)), qf)
        qs_ref[h * tq:(h + 1) * tq, :] = qf.astype(BF16)
    acc_ref[...] = jnp.zeros(acc_ref.shape, F32)
    if online:
        m_ref, = m_scratch
        m_ref[...] = jnp.full(m_ref.shape, NEG_INF, F32)

    def body(c, carry):
        start = pl.multiple_of(c * B_TK, B_TK)
        kc = k_ref[0, pl.ds(start, B_TK), :]
        vc = v_ref[0, pl.ds(start, B_TK), :]
        s = lax.dot_general(qs_ref[...], kc, _NT, preferred_element_type=F32)
        if online:
            m_prev = m_ref[...]
            m_new = jnp.maximum(m_prev, jnp.max(s, axis=1, keepdims=True))
            p = jnp.exp2(s - m_new).astype(BF16)
            acc_ref[...] = (jnp.exp2(m_prev - m_new) * acc_ref[...]
                            + jnp.dot(p, vc, preferred_element_type=F32))
            m_ref[...] = m_new
        else:
            p = jnp.exp2(s).astype(BF16)
            acc_ref[...] += jnp.dot(p, vc, preferred_element_type=F32)
        return carry

    lax.fori_loop(0, seq // B_TK, body, 0)
    acc = acc_ref[...]
    o = acc / acc[:, _ONE_LANE:_ONE_LANE + 1]
    for pair in range(B_GROUP // 2):
        o_e = o[(2 * pair) * tq:(2 * pair + 1) * tq]
        o_o = pltpu.roll(o[(2 * pair + 1) * tq:(2 * pair + 2) * tq], HEAD_DIM, 1)
        o_ref[0, :, pair * LANES:(pair + 1) * LANES] = jnp.where(low, o_e, o_o).astype(BF16)


def _attn_b_call(kmax, bq, bk, bv, *, online):
    b, s, _ = bq.shape
    qw = B_GROUP * HEAD_DIM
    rows = B_GROUP * B_TQ
    scratch = [pltpu.VMEM((rows, LANES), BF16), pltpu.VMEM((rows, LANES), F32)]
    if online:
        scratch.append(pltpu.VMEM((rows, 1), F32))
    return pl.pallas_call(
        functools.partial(_attn_b_kernel, online=online),
        out_shape=jax.ShapeDtypeStruct((b, s, B_WIDTH), BF16),
        grid_spec=pltpu.PrefetchScalarGridSpec(
            num_scalar_prefetch=1,
            grid=(b, B_KV_HEADS, s // B_TQ),
            in_specs=[
                pl.BlockSpec((1, B_TQ, qw), lambda bi, h, i, km: (bi, i, h)),
                pl.BlockSpec((1, s, LANES), lambda bi, h, i, km: (bi, 0, h)),
                pl.BlockSpec((1, s, LANES), lambda bi, h, i, km: (bi, 0, h)),
            ],
            out_specs=pl.BlockSpec((1, B_TQ, qw), lambda bi, h, i, km: (bi, i, h)),
            scratch_shapes=scratch,
        ),
        compiler_params=pltpu.CompilerParams(
            dimension_semantics=("arbitrary", "arbitrary", "arbitrary"),
            vmem_limit_bytes=VMEM_LIMIT_BYTES),
        name="attn_b_online" if online else "attn_b",
    )(kmax, bq, bk, bv)


def _outproj_kernel(x_ref, gate_ref, o0_ref, l0_ref, o1_ref, l1_ref, o2_ref, l2_ref,
                    yb_ref, gz_ref, wpa_ref, wpb_ref, wo_ref, lng_ref, lnb_ref,
                    out_ref, so1, sl1, so2, sl2, *, alpha):
    tm = x_ref.shape[1]
    d_model = x_ref.shape[2]
    for (o_ref, l_ref, so, sl, g) in ((o1_ref, l1_ref, so1, sl1, 1), (o2_ref, l2_ref, so2, sl2, 2)):
        dil = A_GROUPS[g][1]
        for r in range(dil):
            for ch in range(A_WIDTH // LANES):
                cs = slice(ch * LANES, (ch + 1) * LANES)
                so[ch, pl.ds(r, tm // dil, stride=dil), :] = o_ref[0, r, :, cs]
                sl[ch, pl.ds(r, tm // dil, stride=dil), :] = l_ref[0, r, :, cs]
    cat = lambda ref: jnp.concatenate([ref[ch] for ch in range(A_WIDTH // LANES)], axis=1)
    l0, l1, l2 = l0_ref[0], cat(sl1), cat(sl2)
    mx = jnp.maximum(jnp.maximum(l0, l1), l2)
    e0, e1, e2 = jnp.exp(l0 - mx), jnp.exp(l1 - mx), jnp.exp(l2 - mx)
    y_a = (e0 * o0_ref[0] + e1 * cat(so1) + e2 * cat(so2)) / (e0 + e1 + e2)
    gz = gz_ref[0]
    ya = (y_a * gz[:, 0:A_WIDTH].astype(F32)).astype(BF16)
    yb = (yb_ref[0].astype(F32) * gz[:, A_WIDTH:A_WIDTH + B_WIDTH].astype(F32)).astype(BF16)
    pa = jnp.dot(ya, wpa_ref[...], preferred_element_type=F32)
    pb = jnp.dot(yb, wpb_ref[...], preferred_element_type=F32)
    o = A_WIDTH + B_WIDTH
    g_a = gz[:, o:o + d_model].astype(F32)
    g_b = gz[:, o + d_model:o + 2 * d_model].astype(F32)
    merged = (g_a * pa + g_b * pb).astype(BF16)
    out = jnp.dot(merged, wo_ref[...], preferred_element_type=F32)
    h = alpha * x_ref[0] + gate_ref[0] * out
    mu = jnp.mean(h, axis=-1, keepdims=True)
    hc = h - mu
    var = jnp.mean(hc * hc, axis=-1, keepdims=True)
    out_ref[0] = hc * lax.rsqrt(var + LN_EPS) * lng_ref[...] + lnb_ref[...]


def _outproj_call(x, gate, oa, yb, gz, w_pa, w_pb, w_o, ln_g, ln_b, alpha):
    b, s, d = x.shape
    tm = OUT_TM
    (o0, l0), (o1, l1), (o2, l2) = oa
    d1, d2 = A_GROUPS[1][1], A_GROUPS[2][1]
    row = lambda bi, i: (bi, i, 0)
    ph = lambda bi, i: (bi, 0, i, 0)
    fixed = lambda bi, i: (0, 0)
    s0 = pl.BlockSpec((1, 1, tm, A_WIDTH), ph)
    s1 = pl.BlockSpec((1, d1, tm // d1, A_WIDTH), ph)
    s2 = pl.BlockSpec((1, d2, tm // d2, A_WIDTH), ph)

    def kern(x_ref, gate_ref, o0_ref, l0_ref, *rest):
        return _outproj_kernel(x_ref, gate_ref, o0_ref.at[0], l0_ref.at[0], *rest, alpha=alpha)

    return pl.pallas_call(
        kern,
        out_shape=jax.ShapeDtypeStruct((b, s, d), F32),
        grid=(b, s // tm),
        in_specs=[
            pl.BlockSpec((1, tm, d), row),
            pl.BlockSpec((1, 1, d), lambda bi, i: (bi, 0, 0)),
            s0, s0, s1, s1, s2, s2,
            pl.BlockSpec((1, tm, B_WIDTH), row),
            pl.BlockSpec((1, tm, gz.shape[2]), row),
            pl.BlockSpec(w_pa.shape, fixed),
            pl.BlockSpec(w_pb.shape, fixed),
            pl.BlockSpec(w_o.shape, fixed),
            pl.BlockSpec((1, d), fixed),
            pl.BlockSpec((1, d), fixed),
        ],
        out_specs=pl.BlockSpec((1, tm, d), row),
        scratch_shapes=[pltpu.VMEM((A_WIDTH // LANES, tm, LANES), F32)] * 4,
        compiler_params=pltpu.CompilerParams(
            dimension_semantics=("arbitrary", "arbitrary"),
            vmem_limit_bytes=VMEM_LIMIT_BYTES),
        name="outproj",
    )(x, gate, o0, l0, o1, l1, o2, l2, yb, gz, w_pa, w_pb, w_o, ln_g, ln_b)


def _t5_bucket(rel):
    half = REL_BUCKETS // 2
    max_exact = half // 2
    ret = jnp.where(rel > 0, half, 0)
    a = jnp.abs(rel)
    af = jnp.maximum(a, 1).astype(F32)
    large = max_exact + (jnp.log(af / max_exact) / math.log(REL_MAX_DISTANCE / max_exact)
                         * (half - max_exact)).astype(jnp.int32)
    large = jnp.minimum(large, half - 1)
    return ret + jnp.where(a < max_exact, a, large)


def _window_bias(rel_table, g):
    dil = A_GROUPS[g][1]
    tk = A_SUB + 2 * A_RADIUS
    rel = jnp.arange(tk)[None, :] - A_RADIUS - jnp.arange(A_SUB)[:, None]
    table_g = rel_table[:, g * A_HEADS_PER_GROUP:(g + 1) * A_HEADS_PER_GROUP]
    onehot = (_t5_bucket(rel * dil)[..., None] == jnp.arange(REL_BUCKETS)).astype(F32)
    bias = jnp.einsum("qkb,bh->hqk", onehot, table_g.astype(F32), precision=lax.Precision.HIGHEST)
    return jnp.where((jnp.abs(rel) <= A_RADIUS)[None], bias, NEG_INF)


def _rope_tables(seq):
    t = jnp.arange(seq)
    row = (t // GRID_W).astype(F32)
    col = (t % GRID_W).astype(F32)
    half = HEAD_DIM // 2
    inv = ROPE_THETA ** (-jnp.arange(0, half, 2, dtype=F32) / half)
    ar, ac = row[:, None] * inv[None], col[:, None] * inv[None]
    cos = jnp.concatenate([jnp.cos(ar), jnp.cos(ar), jnp.cos(ac), jnp.cos(ac)], axis=1)
    sin = jnp.concatenate([-jnp.sin(ar), jnp.sin(ar), -jnp.sin(ac), jnp.sin(ac)], axis=1)
    return jnp.tile(cos, (1, LANES // HEAD_DIM)), jnp.tile(sin, (1, LANES // HEAD_DIM))


def _layout_w_in(w):
    d = w.shape[0]
    qkv = 3 * A_WIDTH
    aq, ak, av = w[:, 0:qkv], w[:, qkv:2 * qkv], w[:, 2 * qkv:3 * qkv]
    o = 3 * qkv
    az = w[:, o:o + A_WIDTH]; o += A_WIDTH
    bq = w[:, o:o + B_WIDTH]; o += B_WIDTH
    kvw = B_KV_HEADS * HEAD_DIM
    bk = w[:, o:o + kvw]; o += kvw
    bv = w[:, o:o + kvw]; o += kvw
    bz = w[:, o:o + B_WIDTH]; o += B_WIDTH
    gl = w[:, o:]
    groups = []
    for g in range(len(A_GROUPS)):
        sl = slice(g * A_WIDTH, (g + 1) * A_WIDTH)
        groups += [aq[:, sl], ak[:, sl], av[:, sl]]
    zeros = jnp.zeros((d, LANES - HEAD_DIM), w.dtype)
    pad = lambda m: jnp.concatenate(
        [blk for h in range(B_KV_HEADS) for blk in (m[:, h * HEAD_DIM:(h + 1) * HEAD_DIM], zeros)],
        axis=1)
    return jnp.concatenate(groups + [az, bq, pad(bk), pad(bv), bz, gl], axis=1).astype(BF16)


def _pad_heads(g):
    return jnp.concatenate([g, jnp.zeros((LANES - HEAD_DIM,), g.dtype)])


def kernel(x, c, rel_table, ln_g, ln_b, w_ada, b_ada, w_in, b_gate, q_norm_g, k_norm_g, w_pa, w_pb, w_o):
    depth = w_in.shape[0]
    b, s, d = x.shape
    alpha = float((2 * depth) ** 0.25)

    c_pad = jnp.zeros((SUBLANES, d), F32).at[:b].set(c)
    mod = _ada_call(c_pad, w_ada, b_ada)[:, :b]
    cos_t, sin_t = _rope_tables(s)
    idx = jnp.arange(_QK_W) // HEAD_DIM
    bd = (idx[:, None] == idx[None, :]).astype(BF16)
    biases = [_window_bias(rel_table, g) for g in range(len(A_GROUPS))]

    for l in range(depth):
        shift = mod[l, :, 0:d].reshape(b, 1, d)
        scale = mod[l, :, d:2 * d].reshape(b, 1, d)
        gate = mod[l, :, 2 * d:3 * d].reshape(b, 1, d)
        gqk = jnp.concatenate([jnp.tile(q_norm_g[l], B_Q_HEADS),
                               jnp.tile(_pad_heads(k_norm_g[l]), B_KV_HEADS)]).reshape(1, _QK_W)
        a0, a1, a2, bq, bk, bv, gz, st = _inproj_call(
            x, shift, scale, _layout_w_in(w_in[l]), b_gate[l].reshape(1, 2 * d), gqk,
            cos_t, sin_t, bd)
        a0 = a0.reshape(b, 1, s, _A_SEG)
        oa = [_attn_a_call(a_g, biases[g]) for g, a_g in enumerate((a0, a1, a2))]
        kmax = jnp.sqrt(jnp.max(st[:, :, 0:B_KV_HEADS, 0], axis=1))
        qmax = jnp.sqrt(jnp.max(st[:, :, B_KV_HEADS, 0]))
        bound_ok = qmax * jnp.max(kmax) * B_BOUND_SLACK <= B_MAX_BOUND
        yb = lax.cond(
            bound_ok,
            functools.partial(_attn_b_call, online=False),
            functools.partial(_attn_b_call, online=True),
            kmax.reshape(-1), bq, bk, bv)
        x = _outproj_call(x, gate, oa, yb, gz, w_pa[l].astype(BF16), w_pb[l].astype(BF16),
                          w_o[l].astype(BF16), ln_g[l].reshape(1, d), ln_b[l].reshape(1, d), alpha)
    return x
```

```python
import functools
import math

import jax
import jax.numpy as jnp
from jax import lax
from jax.experimental import pallas as pl
from jax.experimental.pallas import tpu as pltpu

HEAD_DIM = 64
A_GROUPS = ((128, 1), (512, 4), (2048, 16))
A_HEADS_PER_GROUP = 8
A_WIDTH = A_HEADS_PER_GROUP * HEAD_DIM
A_RADIUS = 64
B_Q_HEADS = 8
B_KV_HEADS = 2
B_GROUP = B_Q_HEADS // B_KV_HEADS
B_WIDTH = B_Q_HEADS * HEAD_DIM
GRID_W = 64
ROPE_THETA = 10000.0
REL_BUCKETS = 32
REL_MAX_DISTANCE = 1024
LN_EPS = 1e-5
QK_EPS = 1e-6
NEG_INF = -1e30
LOG2_E = math.log2(math.e)
_Q_SCALE = HEAD_DIM ** -0.5 * LOG2_E

LANES = 128
SUBLANES = 8
VMEM_LIMIT_BYTES = 56 * 1024 * 1024

ADA_TN = 1024
IN_TM = 256
A_TQ = 512
A_SUB = 128
B_TQ = 512
B_TK = 2048
OUT_TM = 512
OUT_SPLIT = 1

B_MAX_BOUND = 50.0
B_BOUND_SLACK = 1.0 + 2.0 ** -6

BF16 = jnp.bfloat16
F32 = jnp.float32

_NT = (((1,), (1,)), ((), ()))


def _ada_kernel(c_ref, w_ref, b_ref, o_ref):
    c = c_ref[...]
    h = c * jax.nn.sigmoid(c)
    o_ref[0] = jnp.dot(h, w_ref[0], preferred_element_type=F32,
                       precision=lax.Precision.HIGHEST) + b_ref[0]


def _ada_call(c_pad, w_ada, b_ada):
    depth, d, n3 = w_ada.shape
    rows = c_pad.shape[0]
    return pl.pallas_call(
        _ada_kernel,
        out_shape=jax.ShapeDtypeStruct((depth, rows, n3), F32),
        grid=(depth, n3 // ADA_TN),
        in_specs=[
            pl.BlockSpec((rows, d), lambda l, j: (0, 0)),
            pl.BlockSpec((1, d, ADA_TN), lambda l, j: (l, 0, j)),
            pl.BlockSpec((1, 1, ADA_TN), lambda l, j: (l, 0, j)),
        ],
        out_specs=pl.BlockSpec((1, rows, ADA_TN), lambda l, j: (l, 0, j)),
        compiler_params=pltpu.CompilerParams(
            dimension_semantics=("arbitrary", "arbitrary")),
        name="ada_mod",
    )(c_pad, w_ada, b_ada.reshape(depth, 1, n3))


_A_SEG = 3 * A_WIDTH
_OFF_AZ = 3 * _A_SEG
_OFF_BQ = _OFF_AZ + A_WIDTH
_KV_PAD = B_KV_HEADS * LANES
_OFF_BK = _OFF_BQ + B_WIDTH
_OFF_BV = _OFF_BK + _KV_PAD
_OFF_BZ = _OFF_BV + _KV_PAD
_OFF_GL = _OFF_BZ + B_WIDTH
_QK_W = B_WIDTH + _KV_PAD
_BD_W = 2 * LANES
_ONE_LANE = HEAD_DIM


def _inproj_kernel(x_ref, shift_ref, scale_ref, w_ref, bgate_ref, gqk_ref,
                   cos_ref, sin_ref, bd_ref,
                   a0_ref, a1_ref, a2_ref, bq_ref, bk_ref, bv_ref, gz_ref, st_ref,
                   scr_ref):
    tm = x_ref.shape[1]
    d_model = x_ref.shape[2]
    u = (x_ref[0] * (1.0 + scale_ref[0]) + shift_ref[0]).astype(BF16)

    def proj(off, width):
        return jnp.dot(u, w_ref[:, off:off + width], preferred_element_type=F32)

    kv_lane = lax.broadcasted_iota(jnp.int32, (1, _KV_PAD), 1)
    one_col = ((kv_lane & (LANES - 1)) == _ONE_LANE).astype(F32)

    qk = proj(_OFF_BQ, _QK_W)
    sq = qk * qk
    sq_hi = sq.astype(BF16)
    sq_lo = (sq - sq_hi.astype(F32)).astype(BF16)
    def head_sums(part):
        return jnp.concatenate(
            [jnp.dot(part[:, c:c + _BD_W], bd_ref[...], preferred_element_type=F32)
             for c in range(0, _QK_W, _BD_W)], axis=1)
    ss = head_sums(sq_hi) + head_sums(sq_lo)
    y = qk * lax.rsqrt(ss * (1.0 / HEAD_DIM) + QK_EPS) * gqk_ref[...]
    reps = _QK_W // LANES
    cos = jnp.concatenate([cos_ref[...]] * reps, axis=1)
    sin = jnp.concatenate([sin_ref[...]] * reps, axis=1)
    lane = lax.broadcasted_iota(jnp.int32, y.shape, 1)
    first = (lane & 31) < 16
    swapped = jnp.where(first, pltpu.roll(y, _QK_W - 16, 1), pltpu.roll(y, 16, 1))
    rot = y * cos + swapped * sin
    q = rot[:, :B_WIDTH] * _Q_SCALE
    k = rot[:, B_WIDTH:]
    bq_ref[0] = q.astype(BF16)
    bk_ref[0] = (k + one_col).astype(BF16)

    def tile_max(v2):
        return jnp.max(jnp.sum(v2, axis=1, keepdims=True), axis=0, keepdims=True)
    k2, q2 = k * k, q * q
    kmax = [tile_max(k2[:, h * LANES:(h + 1) * LANES]) for h in range(B_KV_HEADS)]
    qmax = tile_max(q2[:, 0:LANES])
    for pr in range(1, B_WIDTH // LANES):
        qmax = jnp.maximum(qmax, tile_max(q2[:, pr * LANES:(pr + 1) * LANES]))
    sub = lax.broadcasted_iota(jnp.int32, (SUBLANES, LANES), 0)
    st_ref[0, 0] = jnp.where(sub == 0, kmax[0], jnp.where(sub == 1, kmax[1], qmax))

    def proj_a(g):
        res = proj(g * _A_SEG, _A_SEG)
        return [res[:, ch * LANES:(ch + 1) * LANES] * _Q_SCALE if ch < A_WIDTH // LANES
                else res[:, ch * LANES:(ch + 1) * LANES] for ch in range(_A_SEG // LANES)]

    for g, a_ref in ((1, a1_ref), (2, a2_ref)):
        dil = A_GROUPS[g][1]
        for ch, chunk in enumerate(proj_a(g)):
            scr_ref[ch] = chunk
        for r in range(dil):
            for ch in range(_A_SEG // LANES):
                a_ref[0, r, :, ch * LANES:(ch + 1) * LANES] = (
                    scr_ref[ch, pl.ds(r, tm // dil, stride=dil), :].astype(BF16))

    az = proj(_OFF_AZ, A_WIDTH)
    gz_ref[0, :, 0:A_WIDTH] = (az * jax.nn.sigmoid(az)).astype(BF16)
    bz = proj(_OFF_BZ, B_WIDTH)
    gz_ref[0, :, A_WIDTH:A_WIDTH + B_WIDTH] = (bz * jax.nn.sigmoid(bz)).astype(BF16)
    for j in range(2):
        gl = proj(_OFF_GL + j * d_model, d_model) + bgate_ref[:, j * d_model:(j + 1) * d_model]
        o = A_WIDTH + B_WIDTH + j * d_model
        gz_ref[0, :, o:o + d_model] = jax.nn.sigmoid(gl).astype(BF16)

    a0_ref[0] = jnp.concatenate(proj_a(0), axis=1).astype(BF16)
    bv_ref[0] = (proj(_OFF_BV, _KV_PAD) + one_col).astype(BF16)


def _inproj_call(x, shift, scale, w_all, b_gate, gqk, cos_t, sin_t, bd):
    b, s, d = x.shape
    tm = IN_TM
    n_cols = w_all.shape[1]
    d1, d2 = A_GROUPS[1][1], A_GROUPS[2][1]
    const = dict(pipeline_mode=pl.Buffered(1))
    out_shape = (
        jax.ShapeDtypeStruct((b, s, _A_SEG), BF16),
        jax.ShapeDtypeStruct((b, d1, s // d1, _A_SEG), BF16),
        jax.ShapeDtypeStruct((b, d2, s // d2, _A_SEG), BF16),
        jax.ShapeDtypeStruct((b, s, B_WIDTH), BF16),
        jax.ShapeDtypeStruct((b, s, _KV_PAD), BF16),
        jax.ShapeDtypeStruct((b, s, _KV_PAD), BF16),
        jax.ShapeDtypeStruct((b, s, A_WIDTH + B_WIDTH + 2 * d), BF16),
        jax.ShapeDtypeStruct((b, s // tm, SUBLANES, LANES), F32),
    )
    row = lambda bi, i: (bi, i, 0)
    out_specs = (
        pl.BlockSpec((1, tm, _A_SEG), row),
        pl.BlockSpec((1, d1, tm // d1, _A_SEG), lambda bi, i: (bi, 0, i, 0)),
        pl.BlockSpec((1, d2, tm // d2, _A_SEG), lambda bi, i: (bi, 0, i, 0)),
        pl.BlockSpec((1, tm, B_WIDTH), row),
        pl.BlockSpec((1, tm, _KV_PAD), row),
        pl.BlockSpec((1, tm, _KV_PAD), row),
        pl.BlockSpec((1, tm, A_WIDTH + B_WIDTH + 2 * d), row),
        pl.BlockSpec((1, 1, SUBLANES, LANES), lambda bi, i: (bi, i, 0, 0)),
    )
    in_specs = [
        pl.BlockSpec((1, tm, d), row),
        pl.BlockSpec((1, 1, d), lambda bi, i: (bi, 0, 0)),
        pl.BlockSpec((1, 1, d), lambda bi, i: (bi, 0, 0)),
        pl.BlockSpec((d, n_cols), lambda bi, i: (0, 0), **const),
        pl.BlockSpec((1, 2 * d), lambda bi, i: (0, 0), **const),
        pl.BlockSpec((1, _QK_W), lambda bi, i: (0, 0), **const),
        pl.BlockSpec((tm, LANES), lambda bi, i: (i, 0)),
        pl.BlockSpec((tm, LANES), lambda bi, i: (i, 0)),
        pl.BlockSpec((_BD_W, _BD_W), lambda bi, i: (0, 0), **const),
    ]
    return pl.pallas_call(
        _inproj_kernel,
        out_shape=out_shape,
        grid=(b, s // tm),
        in_specs=in_specs,
        out_specs=out_specs,
        scratch_shapes=[pltpu.VMEM((_A_SEG // LANES, tm, LANES), F32)],
        compiler_params=pltpu.CompilerParams(
            dimension_semantics=("arbitrary", "arbitrary"),
            vmem_limit_bytes=VMEM_LIMIT_BYTES),
        name="inproj",
    )(x, shift, scale, w_all, b_gate, gqk, cos_t, sin_t, bd)


def _attn_a_kernel(q_ref, kp_ref, kc_ref, kn_ref, vp_ref, vc_ref, vn_ref, bias_ref,
                   o_ref, lse_ref, *, phase_len):
    tq = q_ref.shape[2]
    i = pl.program_id(2)
    q = q_ref[0, 0]
    k_all = jnp.concatenate([kp_ref[0, 0], kc_ref[0, 0], kn_ref[0, 0]], axis=0)
    v_all = jnp.concatenate([vp_ref[0, 0], vc_ref[0, 0], vn_ref[0, 0]], axis=0)
    tk = A_SUB + 2 * A_RADIUS
    n_sub = tq // A_SUB
    lane = lax.broadcasted_iota(jnp.int32, (A_SUB, LANES), 1)
    low = lane < HEAD_DIM
    kcol = lax.broadcasted_iota(jnp.int32, (1, tk), 1)
    for sub in range(n_sub):
        r0 = sub * A_SUB
        at_edge = sub == 0 or sub == n_sub - 1
        if at_edge:
            krow = i * tq + (r0 - A_RADIUS) + kcol
            valid = (krow >= 0) & (krow < phase_len)
        for pair in range(A_HEADS_PER_GROUP // 2):
            c0 = pair * LANES
            qp = q[r0:r0 + A_SUB, c0:c0 + LANES]
            kp = k_all[r0:r0 + tk, c0:c0 + LANES]
            vp = v_all[r0:r0 + tk, c0:c0 + LANES]
            zero = jnp.zeros_like(qp)
            qs = jnp.concatenate([jnp.where(low, qp, zero), jnp.where(low, zero, qp)], axis=0)
            s = lax.dot_general(qs, kp, _NT, preferred_element_type=F32) + bias_ref[pair]
            if at_edge:
                s = jnp.where(valid, s, NEG_INF)
            m = jnp.max(s, axis=1, keepdims=True)
            p = jnp.exp2(s - m)
            l = jnp.sum(p, axis=1, keepdims=True)
            o = jnp.dot(p.astype(BF16), vp, preferred_element_type=F32) / l
            lse = m + jnp.log2(l)
            o_ref[0, 0, r0:r0 + A_SUB, c0:c0 + LANES] = jnp.where(low, o[:A_SUB], o[A_SUB:])
            lse_ref[0, 0, r0:r0 + A_SUB, c0:c0 + LANES] = jnp.where(low, lse[:A_SUB], lse[A_SUB:])


def _attn_a_call(a_g, bias_g):
    b, dil, phase_len, _ = a_g.shape
    tq = min(A_TQ, phase_len)
    halo = A_RADIUS
    nh = phase_len // halo
    per = tq // halo
    cur = lambda col: (lambda bi, r, i: (bi, r, i, col))
    prev = lambda col: (lambda bi, r, i: (bi, r, jnp.maximum(i * per - 1, 0), col))
    nxt = lambda col: (lambda bi, r, i: (bi, r, jnp.minimum((i + 1) * per, nh - 1), col))
    blk = (1, 1, tq, A_WIDTH)
    hblk = (1, 1, halo, A_WIDTH)
    out_sds = jax.ShapeDtypeStruct((b, dil, phase_len, A_WIDTH), F32)
    return pl.pallas_call(
        functools.partial(_attn_a_kernel, phase_len=phase_len),
        out_shape=(out_sds, out_sds),
        grid=(b, dil, phase_len // tq),
        in_specs=[
            pl.BlockSpec(blk, cur(0)),
            pl.BlockSpec(hblk, prev(1)), pl.BlockSpec(blk, cur(1)), pl.BlockSpec(hblk, nxt(1)),
            pl.BlockSpec(hblk, prev(2)), pl.BlockSpec(blk, cur(2)), pl.BlockSpec(hblk, nxt(2)),
            pl.BlockSpec(bias_g.shape, lambda bi, r, i: (0, 0, 0)),
        ],
        out_specs=(pl.BlockSpec(blk, cur(0)), pl.BlockSpec(blk, cur(0))),
        compiler_params=pltpu.CompilerParams(
            dimension_semantics=("arbitrary", "arbitrary", "arbitrary"),
            vmem_limit_bytes=VMEM_LIMIT_BYTES),
        name=f"attn_a_d{dil}",
    )(a_g, a_g, a_g, a_g, a_g, a_g, a_g, bias_g)


def _attn_b_kernel(kmax_ref, q_ref, k_ref, v_ref, o_ref, qs_ref, acc_ref, *m_scratch, online):
    tq = q_ref.shape[1]
    seq = k_ref.shape[1]
    lane = lax.broadcasted_iota(jnp.int32, (tq, LANES), 1)
    low = lane < HEAD_DIM
    kmax = kmax_ref[pl.program_id(0) * B_KV_HEADS + pl.program_id(1)]
    for h in range(B_GROUP):
        qf = q_ref[0, :, (h // 2) * LANES:(h // 2 + 1) * LANES].astype(F32)
        if h % 2 == 1:
            qf = pltpu.roll(qf, HEAD_DIM, 1)
        qf = jnp.where(low, qf, 0.0)
        if not online:
            qn = jnp.sqrt(jnp.sum(qf * qf, axis=1, keepdims=True))
            qf = jnp.where(lane == _ONE_LANE, -(qn * (kmax * B_BOUND_SLACK)), qf)
        qs_ref[h * tq:(h + 1) * tq, :] = qf.astype(BF16)
    acc_ref[...] = jnp.zeros(acc_ref.shape, F32)
    if online:
        m_ref, = m_scratch
        m_ref[...] = jnp.full(m_ref.shape, NEG_INF, F32)

    def body(c, carry):
        start = pl.multiple_of(c * B_TK, B_TK)
        kc = k_ref[0, pl.ds(start, B_TK), :]
        vc = v_ref[0, pl.ds(start, B_TK), :]
        s = lax.dot_general(qs_ref[...], kc, _NT, preferred_element_type=F32)
        if online:
            m_prev = m_ref[...]
            m_new = jnp.maximum(m_prev, jnp.max(s, axis=1, keepdims=True))
            p = jnp.exp2(s - m_new).astype(BF16)
            acc_ref[...] = (jnp.exp2(m_prev - m_new) * acc_ref[...]
                            + jnp.dot(p, vc, preferred_element_type=F32))
            m_ref[...] = m_new
        else:
            p = jnp.exp2(s).astype(BF16)
            acc_ref[...] += jnp.dot(p, vc, preferred_element_type=F32)
        return carry

    lax.fori_loop(0, seq // B_TK, body, 0)
    acc = acc_ref[...]
    o = acc / acc[:, _ONE_LANE:_ONE_LANE + 1]
    for pair in range(B_GROUP // 2):
        o_e = o[(2 * pair) * tq:(2 * pair + 1) * tq]
        o_o = pltpu.roll(o[(2 * pair + 1) * tq:(2 * pair + 2) * tq], HEAD_DIM, 1)
        o_ref[0, :, pair * LANES:(pair + 1) * LANES] = jnp.where(low, o_e, o_o).astype(BF16)


def _attn_b_call(kmax, bq, bk, bv, *, online):
    b, s, _ = bq.shape
    qw = B_GROUP * HEAD_DIM
    rows = B_GROUP * B_TQ
    scratch = [pltpu.VMEM((rows, LANES), BF16), pltpu.VMEM((rows, LANES), F32)]
    if online:
        scratch.append(pltpu.VMEM((rows, 1), F32))
    return pl.pallas_call(
        functools.partial(_attn_b_kernel, online=online),
        out_shape=jax.ShapeDtypeStruct((b, s, B_WIDTH), BF16),
        grid_spec=pltpu.PrefetchScalarGridSpec(
            num_scalar_prefetch=1,
            grid=(b, B_KV_HEADS, s // B_TQ),
            in_specs=[
                pl.BlockSpec((1, B_TQ, qw), lambda bi, h, i, km: (bi, i, h)),
                pl.BlockSpec((1, s, LANES), lambda bi, h, i, km: (bi, 0, h)),
                pl.BlockSpec((1, s, LANES), lambda bi, h, i, km: (bi, 0, h)),
            ],
            out_specs=pl.BlockSpec((1, B_TQ, qw), lambda bi, h, i, km: (bi, i, h)),
            scratch_shapes=scratch,
        ),
        compiler_params=pltpu.CompilerParams(
            dimension_semantics=("arbitrary", "arbitrary", "arbitrary"),
            vmem_limit_bytes=VMEM_LIMIT_BYTES),
        name="attn_b_online" if online else "attn_b",
    )(kmax, bq, bk, bv)


def _outproj_kernel(x_ref, gate_ref, o0_ref, l0_ref, o1_ref, l1_ref, o2_ref, l2_ref,
                    yb_ref, gz_ref, wpa_ref, wpb_ref, wo_ref, lng_ref, lnb_ref,
                    out_ref, so1, sl1, so2, sl2, *, alpha):
    tm = x_ref.shape[1]
    d_model = x_ref.shape[2]
    for (o_ref, l_ref, so, sl, g) in ((o1_ref, l1_ref, so1, sl1, 1), (o2_ref, l2_ref, so2, sl2, 2)):
        dil = A_GROUPS[g][1]
        for r in range(dil):
            for ch in range(A_WIDTH // LANES):
                cs = slice(ch * LANES, (ch + 1) * LANES)
                so[ch, pl.ds(r, tm // dil, stride=dil), :] = o_ref[0, r, :, cs]
                sl[ch, pl.ds(r, tm // dil, stride=dil), :] = l_ref[0, r, :, cs]
    rows = tm // OUT_SPLIT
    for part in range(OUT_SPLIT):
        rs = slice(part * rows, (part + 1) * rows)
        cat = lambda ref: jnp.concatenate([ref[ch, rs, :] for ch in range(A_WIDTH // LANES)], axis=1)
        l0, l1, l2 = l0_ref[0, rs, :], cat(sl1), cat(sl2)
        mx = jnp.maximum(jnp.maximum(l0, l1), l2)
        e0, e1, e2 = jnp.exp2(l0 - mx), jnp.exp2(l1 - mx), jnp.exp2(l2 - mx)
        y_a = (e0 * o0_ref[0, rs, :] + e1 * cat(so1) + e2 * cat(so2)) / (e0 + e1 + e2)
        gz = gz_ref[0, rs, :]
        ya = (y_a * gz[:, 0:A_WIDTH].astype(F32)).astype(BF16)
        yb = (yb_ref[0, rs, :].astype(F32) * gz[:, A_WIDTH:A_WIDTH + B_WIDTH].astype(F32)).astype(BF16)
        pa = jnp.dot(ya, wpa_ref[...], preferred_element_type=F32)
        pb = jnp.dot(yb, wpb_ref[...], preferred_element_type=F32)
        o = A_WIDTH + B_WIDTH
        g_a = gz[:, o:o + d_model].astype(F32)
        g_b = gz[:, o + d_model:o + 2 * d_model].astype(F32)
        merged = (g_a * pa + g_b * pb).astype(BF16)
        out = jnp.dot(merged, wo_ref[...], preferred_element_type=F32)
        h = alpha * x_ref[0, rs, :] + gate_ref[0] * out
        mu = jnp.mean(h, axis=-1, keepdims=True)
        hc = h - mu
        var = jnp.mean(hc * hc, axis=-1, keepdims=True)
        out_ref[0, rs, :] = hc * lax.rsqrt(var + LN_EPS) * lng_ref[...] + lnb_ref[...]


def _outproj_call(x, gate, oa, yb, gz, w_pa, w_pb, w_o, ln_g, ln_b, alpha):
    b, s, d = x.shape
    tm = OUT_TM
    (o0, l0), (o1, l1), (o2, l2) = oa
    d1, d2 = A_GROUPS[1][1], A_GROUPS[2][1]
    row = lambda bi, i: (bi, i, 0)
    ph = lambda bi, i: (bi, 0, i, 0)
    fixed = lambda bi, i: (0, 0)
    s0 = pl.BlockSpec((1, 1, tm, A_WIDTH), ph)
    s1 = pl.BlockSpec((1, d1, tm // d1, A_WIDTH), ph)
    s2 = pl.BlockSpec((1, d2, tm // d2, A_WIDTH), ph)

    def kern(x_ref, gate_ref, o0_ref, l0_ref, *rest):
        return _outproj_kernel(x_ref, gate_ref, o0_ref.at[0], l0_ref.at[0], *rest, alpha=alpha)

    return pl.pallas_call(
        kern,
        out_shape=jax.ShapeDtypeStruct((b, s, d), F32),
        grid=(b, s // tm),
        in_specs=[
            pl.BlockSpec((1, tm, d), row),
            pl.BlockSpec((1, 1, d), lambda bi, i: (bi, 0, 0)),
            s0, s0, s1, s1, s2, s2,
            pl.BlockSpec((1, tm, B_WIDTH), row),
            pl.BlockSpec((1, tm, gz.shape[2]), row),
            pl.BlockSpec(w_pa.shape, fixed),
            pl.BlockSpec(w_pb.shape, fixed),
            pl.BlockSpec(w_o.shape, fixed),
            pl.BlockSpec((1, d), fixed),
            pl.BlockSpec((1, d), fixed),
        ],
        out_specs=pl.BlockSpec((1, tm, d), row),
        scratch_shapes=[pltpu.VMEM((A_WIDTH // LANES, tm, LANES), F32)] * 4,
        compiler_params=pltpu.CompilerParams(
            dimension_semantics=("arbitrary", "arbitrary"),
            vmem_limit_bytes=VMEM_LIMIT_BYTES),
        name="outproj",
    )(x, gate, o0, l0, o1, l1, o2, l2, yb, gz, w_pa, w_pb, w_o, ln_g, ln_b)


def _t5_bucket(rel):
    half = REL_BUCKETS // 2
    max_exact = half // 2
    ret = jnp.where(rel > 0, half, 0)
    a = jnp.abs(rel)
    af = jnp.maximum(a, 1).astype(F32)
    large = max_exact + (jnp.log(af / max_exact) / math.log(REL_MAX_DISTANCE / max_exact)
                         * (half - max_exact)).astype(jnp.int32)
    large = jnp.minimum(large, half - 1)
    return ret + jnp.where(a < max_exact, a, large)


def _window_bias(rel_table, g):
    dil = A_GROUPS[g][1]
    tk = A_SUB + 2 * A_RADIUS
    rel = jnp.arange(tk)[None, :] - A_RADIUS - jnp.arange(A_SUB)[:, None]
    table_g = rel_table[:, g * A_HEADS_PER_GROUP:(g + 1) * A_HEADS_PER_GROUP]
    onehot = (_t5_bucket(rel * dil)[..., None] == jnp.arange(REL_BUCKETS)).astype(F32)
    bias = jnp.einsum("qkb,bh->hqk", onehot, table_g.astype(F32), precision=lax.Precision.HIGHEST)
    bias = jnp.where((jnp.abs(rel) <= A_RADIUS)[None], bias * LOG2_E, NEG_INF)
    return bias.reshape(A_HEADS_PER_GROUP // 2, 2 * A_SUB, tk)


def _rope_tables(seq):
    t = jnp.arange(seq)
    row = (t // GRID_W).astype(F32)
    col = (t % GRID_W).astype(F32)
    half = HEAD_DIM // 2
    inv = ROPE_THETA ** (-jnp.arange(0, half, 2, dtype=F32) / half)
    ar, ac = row[:, None] * inv[None], col[:, None] * inv[None]
    cos = jnp.concatenate([jnp.cos(ar), jnp.cos(ar), jnp.cos(ac), jnp.cos(ac)], axis=1)
    sin = jnp.concatenate([-jnp.sin(ar), jnp.sin(ar), -jnp.sin(ac), jnp.sin(ac)], axis=1)
    return jnp.tile(cos, (1, LANES // HEAD_DIM)), jnp.tile(sin, (1, LANES // HEAD_DIM))


def _layout_w_in(w):
    d = w.shape[0]
    qkv = 3 * A_WIDTH
    aq, ak, av = w[:, 0:qkv], w[:, qkv:2 * qkv], w[:, 2 * qkv:3 * qkv]
    o = 3 * qkv
    az = w[:, o:o + A_WIDTH]; o += A_WIDTH
    bq = w[:, o:o + B_WIDTH]; o += B_WIDTH
    kvw = B_KV_HEADS * HEAD_DIM
    bk = w[:, o:o + kvw]; o += kvw
    bv = w[:, o:o + kvw]; o += kvw
    bz = w[:, o:o + B_WIDTH]; o += B_WIDTH
    gl = w[:, o:]
    groups = []
    for g in range(len(A_GROUPS)):
        sl = slice(g * A_WIDTH, (g + 1) * A_WIDTH)
        groups += [aq[:, sl], ak[:, sl], av[:, sl]]
    zeros = jnp.zeros((d, LANES - HEAD_DIM), w.dtype)
    pad = lambda m: jnp.concatenate(
        [blk for h in range(B_KV_HEADS) for blk in (m[:, h * HEAD_DIM:(h + 1) * HEAD_DIM], zeros)],
        axis=1)
    return jnp.concatenate(groups + [az, bq, pad(bk), pad(bv), bz, gl], axis=1).astype(BF16)


def _pad_heads(g):
    return jnp.concatenate([g, jnp.zeros((LANES - HEAD_DIM,), g.dtype)])


def kernel(x, c, rel_table, ln_g, ln_b, w_ada, b_ada, w_in, b_gate, q_norm_g, k_norm_g, w_pa, w_pb, w_o):
    depth = w_in.shape[0]
    b, s, d = x.shape
    alpha = float((2 * depth) ** 0.25)

    c_pad = jnp.zeros((SUBLANES, d), F32).at[:b].set(c)
    mod = _ada_call(c_pad, w_ada, b_ada)[:, :b]
    cos_t, sin_t = _rope_tables(s)
    idx = jnp.arange(_BD_W) // HEAD_DIM
    bd = (idx[:, None] == idx[None, :]).astype(BF16)
    biases = [_window_bias(rel_table, g) for g in range(len(A_GROUPS))]

    for l in range(depth):
        shift = mod[l, :, 0:d].reshape(b, 1, d)
        scale = mod[l, :, d:2 * d].reshape(b, 1, d)
        gate = mod[l, :, 2 * d:3 * d].reshape(b, 1, d)
        gqk = jnp.concatenate([jnp.tile(q_norm_g[l], B_Q_HEADS),
                               jnp.tile(_pad_heads(k_norm_g[l]), B_KV_HEADS)]).reshape(1, _QK_W)
        a0, a1, a2, bq, bk, bv, gz, st = _inproj_call(
            x, shift, scale, _layout_w_in(w_in[l]), b_gate[l].reshape(1, 2 * d), gqk,
            cos_t, sin_t, bd)
        a0 = a0.reshape(b, 1, s, _A_SEG)
        oa = [_attn_a_call(a_g, biases[g]) for g, a_g in enumerate((a0, a1, a2))]
        kmax = jnp.sqrt(jnp.max(st[:, :, 0:B_KV_HEADS, 0], axis=1))
        qmax = jnp.sqrt(jnp.max(st[:, :, B_KV_HEADS, 0]))
        bound_ok = qmax * jnp.max(kmax) * B_BOUND_SLACK <= B_MAX_BOUND
        yb = lax.cond(
            bound_ok,
            functools.partial(_attn_b_call, online=False),
            functools.partial(_attn_b_call, online=True),
            kmax.reshape(-1), bq, bk, bv)
        x = _outproj_call(x, gate, oa, yb, gz, w_pa[l].astype(BF16), w_pb[l].astype(BF16),
                          w_o[l].astype(BF16), ln_g[l].reshape(1, d), ln_b[l].reshape(1, d), alpha)
    return x
```

```python
import functools
import math

import jax
import jax.numpy as jnp
from jax import lax
from jax.experimental import pallas as pl
from jax.experimental.pallas import tpu as pltpu

HEAD_DIM = 64
A_GROUPS = ((128, 1), (512, 4), (2048, 16))
A_HEADS_PER_GROUP = 8
A_WIDTH = A_HEADS_PER_GROUP * HEAD_DIM
A_RADIUS = 64
B_Q_HEADS = 8
B_KV_HEADS = 2
B_GROUP = B_Q_HEADS // B_KV_HEADS
B_WIDTH = B_Q_HEADS * HEAD_DIM
GRID_W = 64
ROPE_THETA = 10000.0
REL_BUCKETS = 32
REL_MAX_DISTANCE = 1024
LN_EPS = 1e-5
QK_EPS = 1e-6
NEG_INF = -1e30
LOG2_E = math.log2(math.e)
_Q_SCALE = HEAD_DIM ** -0.5 * LOG2_E

LANES = 128
SUBLANES = 8
VMEM_LIMIT_BYTES = 56 * 1024 * 1024

ADA_TN = 1024
IN_TM = 256
A_TQ = 512
A_SUB = 128
B_TQ = 512
B_TK = 2048
OUT_TM = 512
OUT_SPLIT = 1

B_MAX_BOUND = 50.0
B_BOUND_SLACK = 1.0 + 2.0 ** -6

BF16 = jnp.bfloat16
F32 = jnp.float32

_NT = (((1,), (1,)), ((), ()))


def _ada_kernel(c_ref, w_ref, b_ref, o_ref):
    c = c_ref[...]
    h = c * jax.nn.sigmoid(c)
    o_ref[0] = jnp.dot(h, w_ref[0], preferred_element_type=F32,
                       precision=lax.Precision.HIGHEST) + b_ref[0]


def _ada_call(c_pad, w_ada, b_ada):
    depth, d, n3 = w_ada.shape
    rows = c_pad.shape[0]
    return pl.pallas_call(
        _ada_kernel,
        out_shape=jax.ShapeDtypeStruct((depth, rows, n3), F32),
        grid=(depth, n3 // ADA_TN),
        in_specs=[
            pl.BlockSpec((rows, d), lambda l, j: (0, 0)),
            pl.BlockSpec((1, d, ADA_TN), lambda l, j: (l, 0, j)),
            pl.BlockSpec((1, 1, ADA_TN), lambda l, j: (l, 0, j)),
        ],
        out_specs=pl.BlockSpec((1, rows, ADA_TN), lambda l, j: (l, 0, j)),
        compiler_params=pltpu.CompilerParams(
            dimension_semantics=("arbitrary", "arbitrary")),
        name="ada_mod",
    )(c_pad, w_ada, b_ada.reshape(depth, 1, n3))


_A_SEG = 3 * A_WIDTH
_OFF_AZ = 3 * _A_SEG
_OFF_BQ = _OFF_AZ + A_WIDTH
_KV_PAD = B_KV_HEADS * LANES
_OFF_BK = _OFF_BQ + B_WIDTH
_OFF_BV = _OFF_BK + _KV_PAD
_OFF_BZ = _OFF_BV + _KV_PAD
_OFF_GL = _OFF_BZ + B_WIDTH
_QK_W = B_WIDTH + _KV_PAD
_BD_W = 2 * LANES
_ONE_LANE = HEAD_DIM


def _inproj_kernel(x_ref, shift_ref, scale_ref, w_ref, bgate_ref, gqk_ref,
                   cos_ref, sin_ref, bd_ref,
                   a0_ref, a1_ref, a2_ref, bq_ref, bk_ref, bvt_ref, gz_ref, st_ref,
                   scr_ref):
    tm = x_ref.shape[1]
    d_model = x_ref.shape[2]
    u = (x_ref[0] * (1.0 + scale_ref[0]) + shift_ref[0]).astype(BF16)

    def proj(off, width):
        return jnp.dot(u, w_ref[:, off:off + width], preferred_element_type=F32)

    kv_lane = lax.broadcasted_iota(jnp.int32, (1, _KV_PAD), 1)
    one_col = ((kv_lane & (LANES - 1)) == _ONE_LANE).astype(F32)

    qk = proj(_OFF_BQ, _QK_W)
    sq = qk * qk
    sq_hi = sq.astype(BF16)
    sq_lo = (sq - sq_hi.astype(F32)).astype(BF16)
    def head_sums(part):
        return jnp.concatenate(
            [jnp.dot(part[:, c:c + _BD_W], bd_ref[...], preferred_element_type=F32)
             for c in range(0, _QK_W, _BD_W)], axis=1)
    ss = head_sums(sq_hi) + head_sums(sq_lo)
    y = qk * lax.rsqrt(ss * (1.0 / HEAD_DIM) + QK_EPS) * gqk_ref[...]
    reps = _QK_W // LANES
    cos = jnp.concatenate([cos_ref[...]] * reps, axis=1)
    sin = jnp.concatenate([sin_ref[...]] * reps, axis=1)
    lane = lax.broadcasted_iota(jnp.int32, y.shape, 1)
    first = (lane & 31) < 16
    swapped = jnp.where(first, pltpu.roll(y, _QK_W - 16, 1), pltpu.roll(y, 16, 1))
    rot = y * cos + swapped * sin
    q = rot[:, :B_WIDTH] * _Q_SCALE
    k = rot[:, B_WIDTH:]
    bq_ref[0] = q.astype(BF16)
    bk_ref[0] = (k + one_col).astype(BF16)

    def tile_max(v2):
        return jnp.max(jnp.sum(v2, axis=1, keepdims=True), axis=0, keepdims=True)
    k2, q2 = k * k, q * q
    kmax = [tile_max(k2[:, h * LANES:(h + 1) * LANES]) for h in range(B_KV_HEADS)]
    qmax = tile_max(q2[:, 0:LANES])
    for pr in range(1, B_WIDTH // LANES):
        qmax = jnp.maximum(qmax, tile_max(q2[:, pr * LANES:(pr + 1) * LANES]))
    sub = lax.broadcasted_iota(jnp.int32, (SUBLANES, LANES), 0)
    st_ref[0, 0] = jnp.where(sub == 0, kmax[0], jnp.where(sub == 1, kmax[1], qmax))

    def proj_a(g):
        res = proj(g * _A_SEG, _A_SEG)
        return [res[:, ch * LANES:(ch + 1) * LANES] * _Q_SCALE if ch < A_WIDTH // LANES
                else res[:, ch * LANES:(ch + 1) * LANES] for ch in range(_A_SEG // LANES)]

    for g, a_ref in ((1, a1_ref), (2, a2_ref)):
        dil = A_GROUPS[g][1]
        for ch, chunk in enumerate(proj_a(g)):
            scr_ref[ch] = chunk
        for r in range(dil):
            for ch in range(_A_SEG // LANES):
                a_ref[0, r, :, ch * LANES:(ch + 1) * LANES] = (
                    scr_ref[ch, pl.ds(r, tm // dil, stride=dil), :].astype(BF16))

    az = proj(_OFF_AZ, A_WIDTH)
    gz_ref[0, :, 0:A_WIDTH] = (az * jax.nn.sigmoid(az)).astype(BF16)
    bz = proj(_OFF_BZ, B_WIDTH)
    gz_ref[0, :, A_WIDTH:A_WIDTH + B_WIDTH] = (bz * jax.nn.sigmoid(bz)).astype(BF16)
    for j in range(2):
        gl = proj(_OFF_GL + j * d_model, d_model) + bgate_ref[:, j * d_model:(j + 1) * d_model]
        o = A_WIDTH + B_WIDTH + j * d_model
        gz_ref[0, :, o:o + d_model] = jax.nn.sigmoid(gl).astype(BF16)

    a0_ref[0] = jnp.concatenate(proj_a(0), axis=1).astype(BF16)
    bvt_ref[0] = (proj(_OFF_BV, _KV_PAD) + one_col).T.astype(BF16)


def _inproj_call(x, shift, scale, w_all, b_gate, gqk, cos_t, sin_t, bd):
    b, s, d = x.shape
    tm = IN_TM
    n_cols = w_all.shape[1]
    d1, d2 = A_GROUPS[1][1], A_GROUPS[2][1]
    const = dict(pipeline_mode=pl.Buffered(1))
    out_shape = (
        jax.ShapeDtypeStruct((b, s, _A_SEG), BF16),
        jax.ShapeDtypeStruct((b, d1, s // d1, _A_SEG), BF16),
        jax.ShapeDtypeStruct((b, d2, s // d2, _A_SEG), BF16),
        jax.ShapeDtypeStruct((b, s, B_WIDTH), BF16),
        jax.ShapeDtypeStruct((b, s, _KV_PAD), BF16),
        jax.ShapeDtypeStruct((b, _KV_PAD, s), BF16),
        jax.ShapeDtypeStruct((b, s, A_WIDTH + B_WIDTH + 2 * d), BF16),
        jax.ShapeDtypeStruct((b, s // tm, SUBLANES, LANES), F32),
    )
    row = lambda bi, i: (bi, i, 0)
    out_specs = (
        pl.BlockSpec((1, tm, _A_SEG), row),
        pl.BlockSpec((1, d1, tm // d1, _A_SEG), lambda bi, i: (bi, 0, i, 0)),
        pl.BlockSpec((1, d2, tm // d2, _A_SEG), lambda bi, i: (bi, 0, i, 0)),
        pl.BlockSpec((1, tm, B_WIDTH), row),
        pl.BlockSpec((1, tm, _KV_PAD), row),
        pl.BlockSpec((1, _KV_PAD, tm), lambda bi, i: (bi, 0, i)),
        pl.BlockSpec((1, tm, A_WIDTH + B_WIDTH + 2 * d), row),
        pl.BlockSpec((1, 1, SUBLANES, LANES), lambda bi, i: (bi, i, 0, 0)),
    )
    in_specs = [
        pl.BlockSpec((1, tm, d), row),
        pl.BlockSpec((1, 1, d), lambda bi, i: (bi, 0, 0)),
        pl.BlockSpec((1, 1, d), lambda bi, i: (bi, 0, 0)),
        pl.BlockSpec((d, n_cols), lambda bi, i: (0, 0), **const),
        pl.BlockSpec((1, 2 * d), lambda bi, i: (0, 0), **const),
        pl.BlockSpec((1, _QK_W), lambda bi, i: (0, 0), **const),
        pl.BlockSpec((tm, LANES), lambda bi, i: (i, 0)),
        pl.BlockSpec((tm, LANES), lambda bi, i: (i, 0)),
        pl.BlockSpec((_BD_W, _BD_W), lambda bi, i: (0, 0), **const),
    ]
    return pl.pallas_call(
        _inproj_kernel,
        out_shape=out_shape,
        grid=(b, s // tm),
        in_specs=in_specs,
        out_specs=out_specs,
        scratch_shapes=[pltpu.VMEM((_A_SEG // LANES, tm, LANES), F32)],
        compiler_params=pltpu.CompilerParams(
            dimension_semantics=("arbitrary", "arbitrary"),
            vmem_limit_bytes=VMEM_LIMIT_BYTES),
        name="inproj",
    )(x, shift, scale, w_all, b_gate, gqk, cos_t, sin_t, bd)


def _attn_a_kernel(q_ref, kp_ref, kc_ref, kn_ref, vp_ref, vc_ref, vn_ref, bias_ref,
                   o_ref, lse_ref, *, phase_len):
    tq = q_ref.shape[2]
    i = pl.program_id(2)
    q = q_ref[0, 0]
    k_all = jnp.concatenate([kp_ref[0, 0], kc_ref[0, 0], kn_ref[0, 0]], axis=0)
    v_all = jnp.concatenate([vp_ref[0, 0], vc_ref[0, 0], vn_ref[0, 0]], axis=0)
    tk = A_SUB + 2 * A_RADIUS
    n_sub = tq // A_SUB
    lane = lax.broadcasted_iota(jnp.int32, (A_SUB, LANES), 1)
    low = lane < HEAD_DIM
    kcol = lax.broadcasted_iota(jnp.int32, (1, tk), 1)
    for sub in range(n_sub):
        r0 = sub * A_SUB
        at_edge = sub == 0 or sub == n_sub - 1
        if at_edge:
            krow = i * tq + (r0 - A_RADIUS) + kcol
            valid = (krow >= 0) & (krow < phase_len)
        for pair in range(A_HEADS_PER_GROUP // 2):
            c0 = pair * LANES
            qp = q[r0:r0 + A_SUB, c0:c0 + LANES]
            kp = k_all[r0:r0 + tk, c0:c0 + LANES]
            vp = v_all[r0:r0 + tk, c0:c0 + LANES]
            zero = jnp.zeros_like(qp)
            qs = jnp.concatenate([jnp.where(low, qp, zero), jnp.where(low, zero, qp)], axis=0)
            s = lax.dot_general(qs, kp, _NT, preferred_element_type=F32) + bias_ref[pair]
            if at_edge:
                s = jnp.where(valid, s, NEG_INF)
            m = jnp.max(s, axis=1, keepdims=True)
            p = jnp.exp2(s - m)
            l = jnp.sum(p, axis=1, keepdims=True)
            o = jnp.dot(p.astype(BF16), vp, preferred_element_type=F32) / l
            lse = m + jnp.log2(l)
            o_ref[0, 0, r0:r0 + A_SUB, c0:c0 + LANES] = jnp.where(low, o[:A_SUB], o[A_SUB:])
            lse_ref[0, 0, r0:r0 + A_SUB, c0:c0 + LANES] = jnp.where(low, lse[:A_SUB], lse[A_SUB:])


def _attn_a_call(a_g, bias_g):
    b, dil, phase_len, _ = a_g.shape
    tq = min(A_TQ, phase_len)
    halo = A_RADIUS
    nh = phase_len // halo
    per = tq // halo
    cur = lambda col: (lambda bi, r, i: (bi, r, i, col))
    prev = lambda col: (lambda bi, r, i: (bi, r, jnp.maximum(i * per - 1, 0), col))
    nxt = lambda col: (lambda bi, r, i: (bi, r, jnp.minimum((i + 1) * per, nh - 1), col))
    blk = (1, 1, tq, A_WIDTH)
    hblk = (1, 1, halo, A_WIDTH)
    out_sds = jax.ShapeDtypeStruct((b, dil, phase_len, A_WIDTH), F32)
    return pl.pallas_call(
        functools.partial(_attn_a_kernel, phase_len=phase_len),
        out_shape=(out_sds, out_sds),
        grid=(b, dil, phase_len // tq),
        in_specs=[
            pl.BlockSpec(blk, cur(0)),
            pl.BlockSpec(hblk, prev(1)), pl.BlockSpec(blk, cur(1)), pl.BlockSpec(hblk, nxt(1)),
            pl.BlockSpec(hblk, prev(2)), pl.BlockSpec(blk, cur(2)), pl.BlockSpec(hblk, nxt(2)),
            pl.BlockSpec(bias_g.shape, lambda bi, r, i: (0, 0, 0)),
        ],
        out_specs=(pl.BlockSpec(blk, cur(0)), pl.BlockSpec(blk, cur(0))),
        compiler_params=pltpu.CompilerParams(
            dimension_semantics=("arbitrary", "arbitrary", "arbitrary"),
            vmem_limit_bytes=VMEM_LIMIT_BYTES),
        name=f"attn_a_d{dil}",
    )(a_g, a_g, a_g, a_g, a_g, a_g, a_g, bias_g)


_VT_ROWS = 80


def _attn_b_kernel(kmax_ref, q_ref, k_ref, vt_ref, o_ref, qst_ref, acc_ref, *m_scratch, online):
    tq = q_ref.shape[1]
    seq = k_ref.shape[1]
    lane = lax.broadcasted_iota(jnp.int32, (tq, LANES), 1)
    low = lane < HEAD_DIM
    kmax = kmax_ref[pl.program_id(0) * B_KV_HEADS + pl.program_id(1)]
    for h in range(B_GROUP):
        qf = q_ref[0, :, (h // 2) * LANES:(h // 2 + 1) * LANES].astype(F32)
        if h % 2 == 1:
            qf = pltpu.roll(qf, HEAD_DIM, 1)
        qf = jnp.where(low, qf, 0.0)
        if not online:
            qn = jnp.sqrt(jnp.sum(qf * qf, axis=1, keepdims=True))
            qf = jnp.where(lane == _ONE_LANE, -(qn * (kmax * B_BOUND_SLACK)), qf)
        qst_ref[:, h * tq:(h + 1) * tq] = qf.T.astype(BF16)
    acc_ref[...] = jnp.zeros(acc_ref.shape, F32)
    if online:
        m_ref, = m_scratch
        m_ref[...] = jnp.full(m_ref.shape, NEG_INF, F32)

    n_chunks = seq // B_TK

    def scores_t(c):
        start = pl.multiple_of(c * B_TK, B_TK)
        return jnp.dot(k_ref[0, pl.ds(start, B_TK), :], qst_ref[...],
                       preferred_element_type=F32)

    def values_t(c):
        start = pl.multiple_of(c * B_TK, B_TK)
        return vt_ref[0, 0:_VT_ROWS, pl.ds(start, B_TK)]

    def body(c, carry):
        s_t = scores_t(c)
        if online:
            m_prev = m_ref[...]
            m_new = jnp.maximum(m_prev, jnp.max(s_t, axis=0, keepdims=True))
            p_t = jnp.exp2(s_t - m_new).astype(BF16)
            acc_ref[...] = (jnp.exp2(m_prev - m_new) * acc_ref[...]
                            + jnp.dot(values_t(c), p_t, preferred_element_type=F32))
            m_ref[...] = m_new
        else:
            p_t = jnp.exp2(s_t).astype(BF16)
            acc_ref[...] += jnp.dot(values_t(c), p_t, preferred_element_type=F32)
        return carry

    lax.fori_loop(0, n_chunks, body, 0)
    acc = acc_ref[...]
    o_t = acc[0:HEAD_DIM, :] / acc[_ONE_LANE:_ONE_LANE + 1, :]
    for pair in range(B_GROUP // 2):
        both = jnp.concatenate([o_t[:, (2 * pair) * tq:(2 * pair + 1) * tq],
                                o_t[:, (2 * pair + 1) * tq:(2 * pair + 2) * tq]], axis=0)
        o_ref[0, :, pair * LANES:(pair + 1) * LANES] = both.T.astype(BF16)


def _attn_b_call(kmax, bq, bk, bvt, *, online):
    b, s, _ = bq.shape
    qw = B_GROUP * HEAD_DIM
    cols = B_GROUP * B_TQ
    scratch = [pltpu.VMEM((LANES, cols), BF16), pltpu.VMEM((_VT_ROWS, cols), F32)]
    if online:
        scratch.append(pltpu.VMEM((1, cols), F32))
    return pl.pallas_call(
        functools.partial(_attn_b_kernel, online=online),
        out_shape=jax.ShapeDtypeStruct((b, s, B_WIDTH), BF16),
        grid_spec=pltpu.PrefetchScalarGridSpec(
            num_scalar_prefetch=1,
            grid=(b, B_KV_HEADS, s // B_TQ),
            in_specs=[
                pl.BlockSpec((1, B_TQ, qw), lambda bi, h, i, km: (bi, i, h)),
                pl.BlockSpec((1, s, LANES), lambda bi, h, i, km: (bi, 0, h)),
                pl.BlockSpec((1, LANES, s), lambda bi, h, i, km: (bi, h, 0)),
            ],
            out_specs=pl.BlockSpec((1, B_TQ, qw), lambda bi, h, i, km: (bi, i, h)),
            scratch_shapes=scratch,
        ),
        compiler_params=pltpu.CompilerParams(
            dimension_semantics=("arbitrary", "arbitrary", "arbitrary"),
            vmem_limit_bytes=VMEM_LIMIT_BYTES),
        name="attn_b_online" if online else "attn_b",
    )(kmax, bq, bk, bvt)


def _outproj_kernel(x_ref, gate_ref, o0_ref, l0_ref, o1_ref, l1_ref, o2_ref, l2_ref,
                    yb_ref, gz_ref, wpa_ref, wpb_ref, wo_ref, lng_ref, lnb_ref,
                    out_ref, so1, sl1, so2, sl2, *, alpha):
    tm = x_ref.shape[1]
    d_model = x_ref.shape[2]
    for (o_ref, l_ref, so, sl, g) in ((o1_ref, l1_ref, so1, sl1, 1), (o2_ref, l2_ref, so2, sl2, 2)):
        dil = A_GROUPS[g][1]
        for r in range(dil):
            for ch in range(A_WIDTH // LANES):
                cs = slice(ch * LANES, (ch + 1) * LANES)
                so[ch, pl.ds(r, tm // dil, stride=dil), :] = o_ref[0, r, :, cs]
                sl[ch, pl.ds(r, tm // dil, stride=dil), :] = l_ref[0, r, :, cs]
    rows = tm // OUT_SPLIT
    for part in range(OUT_SPLIT):
        rs = slice(part * rows, (part + 1) * rows)
        cat = lambda ref: jnp.concatenate([ref[ch, rs, :] for ch in range(A_WIDTH // LANES)], axis=1)
        l0, l1, l2 = l0_ref[0, rs, :], cat(sl1), cat(sl2)
        mx = jnp.maximum(jnp.maximum(l0, l1), l2)
        e0, e1, e2 = jnp.exp2(l0 - mx), jnp.exp2(l1 - mx), jnp.exp2(l2 - mx)
        y_a = (e0 * o0_ref[0, rs, :] + e1 * cat(so1) + e2 * cat(so2)) / (e0 + e1 + e2)
        gz = gz_ref[0, rs, :]
        ya = (y_a * gz[:, 0:A_WIDTH].astype(F32)).astype(BF16)
        yb = (yb_ref[0, rs, :].astype(F32) * gz[:, A_WIDTH:A_WIDTH + B_WIDTH].astype(F32)).astype(BF16)
        pa = jnp.dot(ya, wpa_ref[...], preferred_element_type=F32)
        pb = jnp.dot(yb, wpb_ref[...], preferred_element_type=F32)
        o = A_WIDTH + B_WIDTH
        g_a = gz[:, o:o + d_model].astype(F32)
        g_b = gz[:, o + d_model:o + 2 * d_model].astype(F32)
        merged = (g_a * pa + g_b * pb).astype(BF16)
        out = jnp.dot(merged, wo_ref[...], preferred_element_type=F32)
        h = alpha * x_ref[0, rs, :] + gate_ref[0] * out
        mu = jnp.mean(h, axis=-1, keepdims=True)
        hc = h - mu
        var = jnp.mean(hc * hc, axis=-1, keepdims=True)
        out_ref[0, rs, :] = hc * lax.rsqrt(var + LN_EPS) * lng_ref[...] + lnb_ref[...]


def _outproj_call(x, gate, oa, yb, gz, w_pa, w_pb, w_o, ln_g, ln_b, alpha):
    b, s, d = x.shape
    tm = OUT_TM
    (o0, l0), (o1, l1), (o2, l2) = oa
    d1, d2 = A_GROUPS[1][1], A_GROUPS[2][1]
    row = lambda bi, i: (bi, i, 0)
    ph = lambda bi, i: (bi, 0, i, 0)
    fixed = lambda bi, i: (0, 0)
    s0 = pl.BlockSpec((1, 1, tm, A_WIDTH), ph)
    s1 = pl.BlockSpec((1, d1, tm // d1, A_WIDTH), ph)
    s2 = pl.BlockSpec((1, d2, tm // d2, A_WIDTH), ph)

    def kern(x_ref, gate_ref, o0_ref, l0_ref, *rest):
        return _outproj_kernel(x_ref, gate_ref, o0_ref.at[0], l0_ref.at[0], *rest, alpha=alpha)

    return pl.pallas_call(
        kern,
        out_shape=jax.ShapeDtypeStruct((b, s, d), F32),
        grid=(b, s // tm),
        in_specs=[
            pl.BlockSpec((1, tm, d), row),
            pl.BlockSpec((1, 1, d), lambda bi, i: (bi, 0, 0)),
            s0, s0, s1, s1, s2, s2,
            pl.BlockSpec((1, tm, B_WIDTH), row),
            pl.BlockSpec((1, tm, gz.shape[2]), row),
            pl.BlockSpec(w_pa.shape, fixed),
            pl.BlockSpec(w_pb.shape, fixed),
            pl.BlockSpec(w_o.shape, fixed),
            pl.BlockSpec((1, d), fixed),
            pl.BlockSpec((1, d), fixed),
        ],
        out_specs=pl.BlockSpec((1, tm, d), row),
        scratch_shapes=[pltpu.VMEM((A_WIDTH // LANES, tm, LANES), F32)] * 4,
        compiler_params=pltpu.CompilerParams(
            dimension_semantics=("arbitrary", "arbitrary"),
            vmem_limit_bytes=VMEM_LIMIT_BYTES),
        name="outproj",
    )(x, gate, o0, l0, o1, l1, o2, l2, yb, gz, w_pa, w_pb, w_o, ln_g, ln_b)


def _t5_bucket(rel):
    half = REL_BUCKETS // 2
    max_exact = half // 2
    ret = jnp.where(rel > 0, half, 0)
    a = jnp.abs(rel)
    af = jnp.maximum(a, 1).astype(F32)
    large = max_exact + (jnp.log(af / max_exact) / math.log(REL_MAX_DISTANCE / max_exact)
                         * (half - max_exact)).astype(jnp.int32)
    large = jnp.minimum(large, half - 1)
    return ret + jnp.where(a < max_exact, a, large)


def _window_bias(rel_table, g):
    dil = A_GROUPS[g][1]
    tk = A_SUB + 2 * A_RADIUS
    rel = jnp.arange(tk)[None, :] - A_RADIUS - jnp.arange(A_SUB)[:, None]
    table_g = rel_table[:, g * A_HEADS_PER_GROUP:(g + 1) * A_HEADS_PER_GROUP]
    onehot = (_t5_bucket(rel * dil)[..., None] == jnp.arange(REL_BUCKETS)).astype(F32)
    bias = jnp.einsum("qkb,bh->hqk", onehot, table_g.astype(F32), precision=lax.Precision.HIGHEST)
    bias = jnp.where((jnp.abs(rel) <= A_RADIUS)[None], bias * LOG2_E, NEG_INF)
    return bias.reshape(A_HEADS_PER_GROUP // 2, 2 * A_SUB, tk)


def _rope_tables(seq):
    t = jnp.arange(seq)
    row = (t // GRID_W).astype(F32)
    col = (t % GRID_W).astype(F32)
    half = HEAD_DIM // 2
    inv = ROPE_THETA ** (-jnp.arange(0, half, 2, dtype=F32) / half)
    ar, ac = row[:, None] * inv[None], col[:, None] * inv[None]
    cos = jnp.concatenate([jnp.cos(ar), jnp.cos(ar), jnp.cos(ac), jnp.cos(ac)], axis=1)
    sin = jnp.concatenate([-jnp.sin(ar), jnp.sin(ar), -jnp.sin(ac), jnp.sin(ac)], axis=1)
    return jnp.tile(cos, (1, LANES // HEAD_DIM)), jnp.tile(sin, (1, LANES // HEAD_DIM))


def _layout_w_in(w):
    d = w.shape[0]
    qkv = 3 * A_WIDTH
    aq, ak, av = w[:, 0:qkv], w[:, qkv:2 * qkv], w[:, 2 * qkv:3 * qkv]
    o = 3 * qkv
    az = w[:, o:o + A_WIDTH]; o += A_WIDTH
    bq = w[:, o:o + B_WIDTH]; o += B_WIDTH
    kvw = B_KV_HEADS * HEAD_DIM
    bk = w[:, o:o + kvw]; o += kvw
    bv = w[:, o:o + kvw]; o += kvw
    bz = w[:, o:o + B_WIDTH]; o += B_WIDTH
    gl = w[:, o:]
    groups = []
    for g in range(len(A_GROUPS)):
        sl = slice(g * A_WIDTH, (g + 1) * A_WIDTH)
        groups += [aq[:, sl], ak[:, sl], av[:, sl]]
    zeros = jnp.zeros((d, LANES - HEAD_DIM), w.dtype)
    pad = lambda m: jnp.concatenate(
        [blk for h in range(B_KV_HEADS) for blk in (m[:, h * HEAD_DIM:(h + 1) * HEAD_DIM], zeros)],
        axis=1)
    return jnp.concatenate(groups + [az, bq, pad(bk), pad(bv), bz, gl], axis=1).astype(BF16)


def _pad_heads(g):
    return jnp.concatenate([g, jnp.zeros((LANES - HEAD_DIM,), g.dtype)])


def kernel(x, c, rel_table, ln_g, ln_b, w_ada, b_ada, w_in, b_gate, q_norm_g, k_norm_g, w_pa, w_pb, w_o):
    depth = w_in.shape[0]
    b, s, d = x.shape
    alpha = float((2 * depth) ** 0.25)

    c_pad = jnp.zeros((SUBLANES, d), F32).at[:b].set(c)
    mod = _ada_call(c_pad, w_ada, b_ada)[:, :b]
    cos_t, sin_t = _rope_tables(s)
    idx = jnp.arange(_BD_W) // HEAD_DIM
    bd = (idx[:, None] == idx[None, :]).astype(BF16)
    biases = [_window_bias(rel_table, g) for g in range(len(A_GROUPS))]

    for l in range(depth):
        shift = mod[l, :, 0:d].reshape(b, 1, d)
        scale = mod[l, :, d:2 * d].reshape(b, 1, d)
        gate = mod[l, :, 2 * d:3 * d].reshape(b, 1, d)
        gqk = jnp.concatenate([jnp.tile(q_norm_g[l], B_Q_HEADS),
                               jnp.tile(_pad_heads(k_norm_g[l]), B_KV_HEADS)]).reshape(1, _QK_W)
        a0, a1, a2, bq, bk, bv, gz, st = _inproj_call(
            x, shift, scale, _layout_w_in(w_in[l]), b_gate[l].reshape(1, 2 * d), gqk,
            cos_t, sin_t, bd)
        a0 = a0.reshape(b, 1, s, _A_SEG)
        oa = [_attn_a_call(a_g, biases[g]) for g, a_g in enumerate((a0, a1, a2))]
        kmax = jnp.sqrt(jnp.max(st[:, :, 0:B_KV_HEADS, 0], axis=1))
        qmax = jnp.sqrt(jnp.max(st[:, :, B_KV_HEADS, 0]))
        bound_ok = qmax * jnp.max(kmax) * B_BOUND_SLACK <= B_MAX_BOUND
        yb = lax.cond(
            bound_ok,
            functools.partial(_attn_b_call, online=False),
            functools.partial(_attn_b_call, online=True),
            kmax.reshape(-1), bq, bk, bv)
        x = _outproj_call(x, gate, oa, yb, gz, w_pa[l].astype(BF16), w_pb[l].astype(BF16),
                          w_o[l].astype(BF16), ln_g[l].reshape(1, d), ln_b[l].reshape(1, d), alpha)
    return x
```

```python
import functools
import math

import jax
import jax.numpy as jnp
from jax import lax
from jax.experimental import pallas as pl
from jax.experimental.pallas import tpu as pltpu

HEAD_DIM = 64
A_GROUPS = ((128, 1), (512, 4), (2048, 16))
A_HEADS_PER_GROUP = 8
A_WIDTH = A_HEADS_PER_GROUP * HEAD_DIM
A_RADIUS = 64
B_Q_HEADS = 8
B_KV_HEADS = 2
B_GROUP = B_Q_HEADS // B_KV_HEADS
B_WIDTH = B_Q_HEADS * HEAD_DIM
GRID_W = 64
ROPE_THETA = 10000.0
REL_BUCKETS = 32
REL_MAX_DISTANCE = 1024
LN_EPS = 1e-5
QK_EPS = 1e-6
NEG_INF = -1e30
LOG2_E = math.log2(math.e)
_Q_SCALE = HEAD_DIM ** -0.5 * LOG2_E

LANES = 128
SUBLANES = 8
VMEM_LIMIT_BYTES = 56 * 1024 * 1024

ADA_TN = 1024
IN_TM = 256
A_TQ = 512
A_SUB = 128
B_TQ = 512
B_TK = 1024
OUT_TM = 512
OUT_SPLIT = 1

B_MAX_BOUND = 50.0
A_MAX_ABS_LOG2_DENOM = 90.0
B_BOUND_SLACK = 1.0 + 2.0 ** -6

BF16 = jnp.bfloat16
F32 = jnp.float32

_NT = (((1,), (1,)), ((), ()))


def _ada_kernel(c_ref, w_ref, b_ref, o_ref):
    c = c_ref[...]
    h = c * jax.nn.sigmoid(c)
    o_ref[0] = jnp.dot(h, w_ref[0], preferred_element_type=F32,
                       precision=lax.Precision.HIGHEST) + b_ref[0]


def _ada_call(c_pad, w_ada, b_ada):
    depth, d, n3 = w_ada.shape
    rows = c_pad.shape[0]
    return pl.pallas_call(
        _ada_kernel,
        out_shape=jax.ShapeDtypeStruct((depth, rows, n3), F32),
        grid=(depth, n3 // ADA_TN),
        in_specs=[
            pl.BlockSpec((rows, d), lambda l, j: (0, 0)),
            pl.BlockSpec((1, d, ADA_TN), lambda l, j: (l, 0, j)),
            pl.BlockSpec((1, 1, ADA_TN), lambda l, j: (l, 0, j)),
        ],
        out_specs=pl.BlockSpec((1, rows, ADA_TN), lambda l, j: (l, 0, j)),
        compiler_params=pltpu.CompilerParams(
            dimension_semantics=("arbitrary", "arbitrary")),
        name="ada_mod",
    )(c_pad, w_ada, b_ada.reshape(depth, 1, n3))


_A_SEG = 3 * A_WIDTH
_OFF_AZ = 3 * _A_SEG
_OFF_BQ = _OFF_AZ + A_WIDTH
_KV_PAD = B_KV_HEADS * LANES
_OFF_BK = _OFF_BQ + B_WIDTH
_OFF_BV = _OFF_BK + _KV_PAD
_OFF_BZ = _OFF_BV + _KV_PAD
_OFF_GL = _OFF_BZ + B_WIDTH
_QK_W = B_WIDTH + _KV_PAD
_BD_W = 2 * LANES
_ONE_LANE = HEAD_DIM


def _inproj_kernel(x_ref, shift_ref, scale_ref, w_ref, bgate_ref, gqk_ref,
                   cos_ref, sin_ref, bd_ref,
                   a0_ref, a1_ref, a2_ref, bq_ref, bk_ref, bvt_ref, gz_ref, st_ref,
                   scr_ref):
    tm = x_ref.shape[1]
    d_model = x_ref.shape[2]
    u = (x_ref[0] * (1.0 + scale_ref[0]) + shift_ref[0]).astype(BF16)

    def proj(off, width):
        return jnp.dot(u, w_ref[:, off:off + width], preferred_element_type=F32)

    kv_lane = lax.broadcasted_iota(jnp.int32, (1, _KV_PAD), 1)
    one_col = ((kv_lane & (LANES - 1)) == _ONE_LANE).astype(F32)

    qk = proj(_OFF_BQ, _QK_W)
    sq = qk * qk
    sq_hi = sq.astype(BF16)
    sq_lo = (sq - sq_hi.astype(F32)).astype(BF16)
    def head_sums(part):
        return jnp.concatenate(
            [jnp.dot(part[:, c:c + _BD_W], bd_ref[...], preferred_element_type=F32)
             for c in range(0, _QK_W, _BD_W)], axis=1)
    ss = head_sums(sq_hi) + head_sums(sq_lo)
    y = qk * lax.rsqrt(ss * (1.0 / HEAD_DIM) + QK_EPS) * gqk_ref[...]
    reps = _QK_W // LANES
    cos = jnp.concatenate([cos_ref[...]] * reps, axis=1)
    sin = jnp.concatenate([sin_ref[...]] * reps, axis=1)
    lane = lax.broadcasted_iota(jnp.int32, y.shape, 1)
    first = (lane & 31) < 16
    swapped = jnp.where(first, pltpu.roll(y, _QK_W - 16, 1), pltpu.roll(y, 16, 1))
    rot = y * cos + swapped * sin
    q = rot[:, :B_WIDTH] * _Q_SCALE
    k = rot[:, B_WIDTH:]
    bq_ref[0] = q.astype(BF16)
    bk_ref[0] = (k + one_col).astype(BF16)

    def tile_max(v2):
        return jnp.max(jnp.sum(v2, axis=1, keepdims=True), axis=0, keepdims=True)
    k2, q2 = k * k, q * q
    kmax = [tile_max(k2[:, h * LANES:(h + 1) * LANES]) for h in range(B_KV_HEADS)]
    qmax = tile_max(q2[:, 0:LANES])
    for pr in range(1, B_WIDTH // LANES):
        qmax = jnp.maximum(qmax, tile_max(q2[:, pr * LANES:(pr + 1) * LANES]))
    sub = lax.broadcasted_iota(jnp.int32, (SUBLANES, LANES), 0)
    st_ref[0, 0] = jnp.where(sub == 0, kmax[0], jnp.where(sub == 1, kmax[1], qmax))

    def proj_a(g):
        res = proj(g * _A_SEG, _A_SEG)
        return [res[:, ch * LANES:(ch + 1) * LANES] * _Q_SCALE if ch < A_WIDTH // LANES
                else res[:, ch * LANES:(ch + 1) * LANES] for ch in range(_A_SEG // LANES)]

    for g, a_ref in ((1, a1_ref), (2, a2_ref)):
        dil = A_GROUPS[g][1]
        for ch, chunk in enumerate(proj_a(g)):
            scr_ref[ch] = chunk
        for r in range(dil):
            for ch in range(_A_SEG // LANES):
                a_ref[0, r, :, ch * LANES:(ch + 1) * LANES] = (
                    scr_ref[ch, pl.ds(r, tm // dil, stride=dil), :].astype(BF16))

    az = proj(_OFF_AZ, A_WIDTH)
    gz_ref[0, :, 0:A_WIDTH] = (az * jax.nn.sigmoid(az)).astype(BF16)
    bz = proj(_OFF_BZ, B_WIDTH)
    gz_ref[0, :, A_WIDTH:A_WIDTH + B_WIDTH] = (bz * jax.nn.sigmoid(bz)).astype(BF16)
    for j in range(2):
        gl = proj(_OFF_GL + j * d_model, d_model) + bgate_ref[:, j * d_model:(j + 1) * d_model]
        o = A_WIDTH + B_WIDTH + j * d_model
        gz_ref[0, :, o:o + d_model] = jax.nn.sigmoid(gl).astype(BF16)

    a0_ref[0] = jnp.concatenate(proj_a(0), axis=1).astype(BF16)
    bvt_ref[0] = (proj(_OFF_BV, _KV_PAD) + one_col).T.astype(BF16)


def _inproj_call(x, shift, scale, w_all, b_gate, gqk, cos_t, sin_t, bd):
    b, s, d = x.shape
    tm = IN_TM
    n_cols = w_all.shape[1]
    d1, d2 = A_GROUPS[1][1], A_GROUPS[2][1]
    const = dict(pipeline_mode=pl.Buffered(1))
    out_shape = (
        jax.ShapeDtypeStruct((b, s, _A_SEG), BF16),
        jax.ShapeDtypeStruct((b, d1, s // d1, _A_SEG), BF16),
        jax.ShapeDtypeStruct((b, d2, s // d2, _A_SEG), BF16),
        jax.ShapeDtypeStruct((b, s, B_WIDTH), BF16),
        jax.ShapeDtypeStruct((b, s, _KV_PAD), BF16),
        jax.ShapeDtypeStruct((b, _KV_PAD, s), BF16),
        jax.ShapeDtypeStruct((b, s, A_WIDTH + B_WIDTH + 2 * d), BF16),
        jax.ShapeDtypeStruct((b, s // tm, SUBLANES, LANES), F32),
    )
    row = lambda bi, i: (bi, i, 0)
    out_specs = (
        pl.BlockSpec((1, tm, _A_SEG), row),
        pl.BlockSpec((1, d1, tm // d1, _A_SEG), lambda bi, i: (bi, 0, i, 0)),
        pl.BlockSpec((1, d2, tm // d2, _A_SEG), lambda bi, i: (bi, 0, i, 0)),
        pl.BlockSpec((1, tm, B_WIDTH), row),
        pl.BlockSpec((1, tm, _KV_PAD), row),
        pl.BlockSpec((1, _KV_PAD, tm), lambda bi, i: (bi, 0, i)),
        pl.BlockSpec((1, tm, A_WIDTH + B_WIDTH + 2 * d), row),
        pl.BlockSpec((1, 1, SUBLANES, LANES), lambda bi, i: (bi, i, 0, 0)),
    )
    in_specs = [
        pl.BlockSpec((1, tm, d), row),
        pl.BlockSpec((1, 1, d), lambda bi, i: (bi, 0, 0)),
        pl.BlockSpec((1, 1, d), lambda bi, i: (bi, 0, 0)),
        pl.BlockSpec((d, n_cols), lambda bi, i: (0, 0), **const),
        pl.BlockSpec((1, 2 * d), lambda bi, i: (0, 0), **const),
        pl.BlockSpec((1, _QK_W), lambda bi, i: (0, 0), **const),
        pl.BlockSpec((tm, LANES), lambda bi, i: (i, 0)),
        pl.BlockSpec((tm, LANES), lambda bi, i: (i, 0)),
        pl.BlockSpec((_BD_W, _BD_W), lambda bi, i: (0, 0), **const),
    ]
    return pl.pallas_call(
        _inproj_kernel,
        out_shape=out_shape,
        grid=(b, s // tm),
        in_specs=in_specs,
        out_specs=out_specs,
        scratch_shapes=[pltpu.VMEM((_A_SEG // LANES, tm, LANES), F32)],
        compiler_params=pltpu.CompilerParams(
            dimension_semantics=("arbitrary", "arbitrary"),
            vmem_limit_bytes=VMEM_LIMIT_BYTES),
        name="inproj",
    )(x, shift, scale, w_all, b_gate, gqk, cos_t, sin_t, bd)


def _attn_a_kernel(q_ref, kp_ref, kc_ref, kn_ref, vp_ref, vc_ref, vn_ref, bias_ref,
                   o_ref, lse_ref, range_ref, *, phase_len, stabilise):
    tq = q_ref.shape[2]
    i = pl.program_id(2)
    q = q_ref[0, 0]
    k_all = jnp.concatenate([kp_ref[0, 0], kc_ref[0, 0], kn_ref[0, 0]], axis=0)
    v_all = jnp.concatenate([vp_ref[0, 0], vc_ref[0, 0], vn_ref[0, 0]], axis=0)
    tk = A_SUB + 2 * A_RADIUS
    n_sub = tq // A_SUB
    lane = lax.broadcasted_iota(jnp.int32, (A_SUB, LANES), 1)
    low = lane < HEAD_DIM
    kcol = lax.broadcasted_iota(jnp.int32, (1, tk), 1)
    worst = jnp.zeros((A_SUB, LANES), F32)
    for sub in range(n_sub):
        r0 = sub * A_SUB
        at_edge = sub == 0 or sub == n_sub - 1
        if at_edge:
            krow = i * tq + (r0 - A_RADIUS) + kcol
            valid = (krow >= 0) & (krow < phase_len)
        for pair in range(A_HEADS_PER_GROUP // 2):
            c0 = pair * LANES
            qp = q[r0:r0 + A_SUB, c0:c0 + LANES]
            kp = k_all[r0:r0 + tk, c0:c0 + LANES]
            vp = v_all[r0:r0 + tk, c0:c0 + LANES]
            zero = jnp.zeros_like(qp)
            qs = jnp.concatenate([jnp.where(low, qp, zero), jnp.where(low, zero, qp)], axis=0)
            s = lax.dot_general(qs, kp, _NT, preferred_element_type=F32) + bias_ref[pair]
            if at_edge:
                s = jnp.where(valid, s, NEG_INF)
            if stabilise:
                m = jnp.max(s, axis=1, keepdims=True)
                s = s - m
            p = jnp.exp2(s)
            l = jnp.sum(p, axis=1, keepdims=True)
            o = jnp.dot(p.astype(BF16), vp, preferred_element_type=F32)
            pick = lambda a: jnp.where(low, a[:A_SUB], a[A_SUB:])
            l = pick(l)
            lse = jnp.log2(l)
            worst = jnp.maximum(worst, jnp.abs(lse))
            if stabilise:
                lse = lse + pick(m)
            o_ref[0, 0, r0:r0 + A_SUB, c0:c0 + LANES] = pick(o) * (1.0 / l)
            lse_ref[0, 0, r0:r0 + A_SUB, c0:c0 + LANES] = lse
    range_ref[0, 0, 0] = jnp.broadcast_to(
        jnp.max(jnp.max(worst, axis=1, keepdims=True), axis=0, keepdims=True), (SUBLANES, LANES))


def _attn_a_call(a_g, bias_g, *, stabilise):
    b, dil, phase_len, _ = a_g.shape
    tq = min(A_TQ, phase_len)
    n_steps = phase_len // tq
    halo = A_RADIUS
    nh = phase_len // halo
    per = tq // halo
    cur = lambda col: (lambda bi, r, i: (bi, r, i, col))
    prev = lambda col: (lambda bi, r, i: (bi, r, jnp.maximum(i * per - 1, 0), col))
    nxt = lambda col: (lambda bi, r, i: (bi, r, jnp.minimum((i + 1) * per, nh - 1), col))
    blk = (1, 1, tq, A_WIDTH)
    hblk = (1, 1, halo, A_WIDTH)
    out_sds = jax.ShapeDtypeStruct((b, dil, phase_len, A_WIDTH), F32)
    range_sds = jax.ShapeDtypeStruct((b, dil, n_steps, SUBLANES, LANES), F32)
    return pl.pallas_call(
        functools.partial(_attn_a_kernel, phase_len=phase_len, stabilise=stabilise),
        out_shape=(out_sds, out_sds, range_sds),
        grid=(b, dil, n_steps),
        in_specs=[
            pl.BlockSpec(blk, cur(0)),
            pl.BlockSpec(hblk, prev(1)), pl.BlockSpec(blk, cur(1)), pl.BlockSpec(hblk, nxt(1)),
            pl.BlockSpec(hblk, prev(2)), pl.BlockSpec(blk, cur(2)), pl.BlockSpec(hblk, nxt(2)),
            pl.BlockSpec(bias_g.shape, lambda bi, r, i: (0, 0, 0)),
        ],
        out_specs=(pl.BlockSpec(blk, cur(0)), pl.BlockSpec(blk, cur(0)),
                   pl.BlockSpec((1, 1, 1, SUBLANES, LANES), lambda bi, r, i: (bi, r, i, 0, 0))),
        compiler_params=pltpu.CompilerParams(
            dimension_semantics=("arbitrary", "arbitrary", "arbitrary"),
            vmem_limit_bytes=VMEM_LIMIT_BYTES),
        name=f"attn_a_d{dil}" + ("_stab" if stabilise else ""),
    )(a_g, a_g, a_g, a_g, a_g, a_g, a_g, bias_g)


def _attn_a(a_g, bias_g):
    o, lse, worst = _attn_a_call(a_g, bias_g, stabilise=False)
    in_range = jnp.max(worst) <= A_MAX_ABS_LOG2_DENOM
    return lax.cond(in_range,
                    lambda: (o, lse),
                    lambda: _attn_a_call(a_g, bias_g, stabilise=True)[:2])


_VT_ROWS = 80


def _attn_b_kernel(kmax_ref, q_ref, k_ref, vt_ref, o_ref, qst_ref, acc_ref, *m_scratch, online):
    tq = q_ref.shape[1]
    seq = k_ref.shape[1]
    lane = lax.broadcasted_iota(jnp.int32, (tq, LANES), 1)
    low = lane < HEAD_DIM
    kmax = kmax_ref[pl.program_id(0) * B_KV_HEADS + pl.program_id(1)]
    for h in range(B_GROUP):
        qf = q_ref[0, :, (h // 2) * LANES:(h // 2 + 1) * LANES].astype(F32)
        if h % 2 == 1:
            qf = pltpu.roll(qf, HEAD_DIM, 1)
        qf = jnp.where(low, qf, 0.0)
        if not online:
            qn = jnp.sqrt(jnp.sum(qf * qf, axis=1, keepdims=True))
            qf = jnp.where(lane == _ONE_LANE, -(qn * (kmax * B_BOUND_SLACK)), qf)
        qst_ref[:, h * tq:(h + 1) * tq] = qf.T.astype(BF16)
    acc_ref[...] = jnp.zeros(acc_ref.shape, F32)
    if online:
        m_ref, = m_scratch
        m_ref[...] = jnp.full(m_ref.shape, NEG_INF, F32)

    n_chunks = seq // B_TK

    def scores_t(c):
        start = pl.multiple_of(c * B_TK, B_TK)
        return jnp.dot(k_ref[0, pl.ds(start, B_TK), :], qst_ref[...],
                       preferred_element_type=F32)

    def values_t(c):
        start = pl.multiple_of(c * B_TK, B_TK)
        return vt_ref[0, 0:_VT_ROWS, pl.ds(start, B_TK)]

    if online:
        def body(c, carry):
            s_t = scores_t(c)
            m_prev = m_ref[...]
            m_new = jnp.maximum(m_prev, jnp.max(s_t, axis=0, keepdims=True))
            p_t = jnp.exp2(s_t - m_new).astype(BF16)
            acc_ref[...] = (jnp.exp2(m_prev - m_new) * acc_ref[...]
                            + jnp.dot(values_t(c), p_t, preferred_element_type=F32))
            m_ref[...] = m_new
            return carry

        lax.fori_loop(0, n_chunks, body, 0)
    else:
        p_ref, = m_scratch

        def fill(slot, c):
            p_ref[slot] = jnp.exp2(scores_t(c)).astype(BF16)

        def drain(slot, c):
            acc_ref[...] += jnp.dot(values_t(c), p_ref[slot], preferred_element_type=F32)

        fill(0, 0)

        def body(j, carry):
            c = 2 * j
            fill(1, c + 1)
            drain(0, c)
            fill(0, c + 2)
            drain(1, c + 1)
            return carry

        lax.fori_loop(0, n_chunks // 2 - 1, body, 0)
        fill(1, n_chunks - 1)
        drain(0, n_chunks - 2)
        drain(1, n_chunks - 1)
    acc = acc_ref[...]
    o_t = acc[0:HEAD_DIM, :] / acc[_ONE_LANE:_ONE_LANE + 1, :]
    for pair in range(B_GROUP // 2):
        both = jnp.concatenate([o_t[:, (2 * pair) * tq:(2 * pair + 1) * tq],
                                o_t[:, (2 * pair + 1) * tq:(2 * pair + 2) * tq]], axis=0)
        o_ref[0, :, pair * LANES:(pair + 1) * LANES] = both.T.astype(BF16)


def _attn_b_call(kmax, bq, bk, bvt, *, online):
    b, s, _ = bq.shape
    qw = B_GROUP * HEAD_DIM
    cols = B_GROUP * B_TQ
    scratch = [pltpu.VMEM((LANES, cols), BF16), pltpu.VMEM((_VT_ROWS, cols), F32)]
    scratch.append(pltpu.VMEM((1, cols), F32) if online else pltpu.VMEM((2, B_TK, cols), BF16))
    return pl.pallas_call(
        functools.partial(_attn_b_kernel, online=online),
        out_shape=jax.ShapeDtypeStruct((b, s, B_WIDTH), BF16),
        grid_spec=pltpu.PrefetchScalarGridSpec(
            num_scalar_prefetch=1,
            grid=(b, B_KV_HEADS, s // B_TQ),
            in_specs=[
                pl.BlockSpec((1, B_TQ, qw), lambda bi, h, i, km: (bi, i, h)),
                pl.BlockSpec((1, s, LANES), lambda bi, h, i, km: (bi, 0, h)),
                pl.BlockSpec((1, LANES, s), lambda bi, h, i, km: (bi, h, 0)),
            ],
            out_specs=pl.BlockSpec((1, B_TQ, qw), lambda bi, h, i, km: (bi, i, h)),
            scratch_shapes=scratch,
        ),
        compiler_params=pltpu.CompilerParams(
            dimension_semantics=("arbitrary", "arbitrary", "arbitrary"),
            vmem_limit_bytes=VMEM_LIMIT_BYTES),
        name="attn_b_online" if online else "attn_b",
    )(kmax, bq, bk, bvt)


def _outproj_kernel(x_ref, gate_ref, o0_ref, l0_ref, o1_ref, l1_ref, o2_ref, l2_ref,
                    yb_ref, gz_ref, wpa_ref, wpb_ref, wo_ref, lng_ref, lnb_ref,
                    out_ref, so1, sl1, so2, sl2, *, alpha):
    tm = x_ref.shape[1]
    d_model = x_ref.shape[2]
    for (o_ref, l_ref, so, sl, g) in ((o1_ref, l1_ref, so1, sl1, 1), (o2_ref, l2_ref, so2, sl2, 2)):
        dil = A_GROUPS[g][1]
        for r in range(dil):
            for ch in range(A_WIDTH // LANES):
                cs = slice(ch * LANES, (ch + 1) * LANES)
                so[ch, pl.ds(r, tm // dil, stride=dil), :] = o_ref[0, r, :, cs]
                sl[ch, pl.ds(r, tm // dil, stride=dil), :] = l_ref[0, r, :, cs]
    rows = tm // OUT_SPLIT
    for part in range(OUT_SPLIT):
        rs = slice(part * rows, (part + 1) * rows)
        cat = lambda ref: jnp.concatenate([ref[ch, rs, :] for ch in range(A_WIDTH // LANES)], axis=1)
        l0, l1, l2 = l0_ref[0, rs, :], cat(sl1), cat(sl2)
        mx = jnp.maximum(jnp.maximum(l0, l1), l2)
        e0, e1, e2 = jnp.exp2(l0 - mx), jnp.exp2(l1 - mx), jnp.exp2(l2 - mx)
        y_a = (e0 * o0_ref[0, rs, :] + e1 * cat(so1) + e2 * cat(so2)) / (e0 + e1 + e2)
        gz = gz_ref[0, rs, :]
        ya = (y_a * gz[:, 0:A_WIDTH].astype(F32)).astype(BF16)
        yb = (yb_ref[0, rs, :].astype(F32) * gz[:, A_WIDTH:A_WIDTH + B_WIDTH].astype(F32)).astype(BF16)
        pa = jnp.dot(ya, wpa_ref[...], preferred_element_type=F32)
        pb = jnp.dot(yb, wpb_ref[...], preferred_element_type=F32)
        o = A_WIDTH + B_WIDTH
        g_a = gz[:, o:o + d_model].astype(F32)
        g_b = gz[:, o + d_model:o + 2 * d_model].astype(F32)
        merged = (g_a * pa + g_b * pb).astype(BF16)
        out = jnp.dot(merged, wo_ref[...], preferred_element_type=F32)
        h = alpha * x_ref[0, rs, :] + gate_ref[0] * out
        mu = jnp.mean(h, axis=-1, keepdims=True)
        hc = h - mu
        var = jnp.mean(hc * hc, axis=-1, keepdims=True)
        out_ref[0, rs, :] = hc * lax.rsqrt(var + LN_EPS) * lng_ref[...] + lnb_ref[...]


def _outproj_call(x, gate, oa, yb, gz, w_pa, w_pb, w_o, ln_g, ln_b, alpha):
    b, s, d = x.shape
    tm = OUT_TM
    (o0, l0), (o1, l1), (o2, l2) = oa
    d1, d2 = A_GROUPS[1][1], A_GROUPS[2][1]
    row = lambda bi, i: (bi, i, 0)
    ph = lambda bi, i: (bi, 0, i, 0)
    fixed = lambda bi, i: (0, 0)
    s0 = pl.BlockSpec((1, 1, tm, A_WIDTH), ph)
    s1 = pl.BlockSpec((1, d1, tm // d1, A_WIDTH), ph)
    s2 = pl.BlockSpec((1, d2, tm // d2, A_WIDTH), ph)

    def kern(x_ref, gate_ref, o0_ref, l0_ref, *rest):
        return _outproj_kernel(x_ref, gate_ref, o0_ref.at[0], l0_ref.at[0], *rest, alpha=alpha)

    return pl.pallas_call(
        kern,
        out_shape=jax.ShapeDtypeStruct((b, s, d), F32),
        grid=(b, s // tm),
        in_specs=[
            pl.BlockSpec((1, tm, d), row),
            pl.BlockSpec((1, 1, d), lambda bi, i: (bi, 0, 0)),
            s0, s0, s1, s1, s2, s2,
            pl.BlockSpec((1, tm, B_WIDTH), row),
            pl.BlockSpec((1, tm, gz.shape[2]), row),
            pl.BlockSpec(w_pa.shape, fixed),
            pl.BlockSpec(w_pb.shape, fixed),
            pl.BlockSpec(w_o.shape, fixed),
            pl.BlockSpec((1, d), fixed),
            pl.BlockSpec((1, d), fixed),
        ],
        out_specs=pl.BlockSpec((1, tm, d), row),
        scratch_shapes=[pltpu.VMEM((A_WIDTH // LANES, tm, LANES), F32)] * 4,
        compiler_params=pltpu.CompilerParams(
            dimension_semantics=("arbitrary", "arbitrary"),
            vmem_limit_bytes=VMEM_LIMIT_BYTES),
        name="outproj",
    )(x, gate, o0, l0, o1, l1, o2, l2, yb, gz, w_pa, w_pb, w_o, ln_g, ln_b)


def _t5_bucket(rel):
    half = REL_BUCKETS // 2
    max_exact = half // 2
    ret = jnp.where(rel > 0, half, 0)
    a = jnp.abs(rel)
    af = jnp.maximum(a, 1).astype(F32)
    large = max_exact + (jnp.log(af / max_exact) / math.log(REL_MAX_DISTANCE / max_exact)
                         * (half - max_exact)).astype(jnp.int32)
    large = jnp.minimum(large, half - 1)
    return ret + jnp.where(a < max_exact, a, large)


def _window_bias(rel_table, g):
    dil = A_GROUPS[g][1]
    tk = A_SUB + 2 * A_RADIUS
    rel = jnp.arange(tk)[None, :] - A_RADIUS - jnp.arange(A_SUB)[:, None]
    table_g = rel_table[:, g * A_HEADS_PER_GROUP:(g + 1) * A_HEADS_PER_GROUP]
    onehot = (_t5_bucket(rel * dil)[..., None] == jnp.arange(REL_BUCKETS)).astype(F32)
    bias = jnp.einsum("qkb,bh->hqk", onehot, table_g.astype(F32), precision=lax.Precision.HIGHEST)
    bias = jnp.where((jnp.abs(rel) <= A_RADIUS)[None], bias * LOG2_E, NEG_INF)
    return bias.reshape(A_HEADS_PER_GROUP // 2, 2 * A_SUB, tk)


def _rope_tables(seq):
    t = jnp.arange(seq)
    row = (t // GRID_W).astype(F32)
    col = (t % GRID_W).astype(F32)
    half = HEAD_DIM // 2
    inv = ROPE_THETA ** (-jnp.arange(0, half, 2, dtype=F32) / half)
    ar, ac = row[:, None] * inv[None], col[:, None] * inv[None]
    cos = jnp.concatenate([jnp.cos(ar), jnp.cos(ar), jnp.cos(ac), jnp.cos(ac)], axis=1)
    sin = jnp.concatenate([-jnp.sin(ar), jnp.sin(ar), -jnp.sin(ac), jnp.sin(ac)], axis=1)
    return jnp.tile(cos, (1, LANES // HEAD_DIM)), jnp.tile(sin, (1, LANES // HEAD_DIM))


def _layout_w_in(w):
    d = w.shape[0]
    qkv = 3 * A_WIDTH
    aq, ak, av = w[:, 0:qkv], w[:, qkv:2 * qkv], w[:, 2 * qkv:3 * qkv]
    o = 3 * qkv
    az = w[:, o:o + A_WIDTH]; o += A_WIDTH
    bq = w[:, o:o + B_WIDTH]; o += B_WIDTH
    kvw = B_KV_HEADS * HEAD_DIM
    bk = w[:, o:o + kvw]; o += kvw
    bv = w[:, o:o + kvw]; o += kvw
    bz = w[:, o:o + B_WIDTH]; o += B_WIDTH
    gl = w[:, o:]
    groups = []
    for g in range(len(A_GROUPS)):
        sl = slice(g * A_WIDTH, (g + 1) * A_WIDTH)
        groups += [aq[:, sl], ak[:, sl], av[:, sl]]
    zeros = jnp.zeros((d, LANES - HEAD_DIM), w.dtype)
    pad = lambda m: jnp.concatenate(
        [blk for h in range(B_KV_HEADS) for blk in (m[:, h * HEAD_DIM:(h + 1) * HEAD_DIM], zeros)],
        axis=1)
    return jnp.concatenate(groups + [az, bq, pad(bk), pad(bv), bz, gl], axis=1).astype(BF16)


def _pad_heads(g):
    return jnp.concatenate([g, jnp.zeros((LANES - HEAD_DIM,), g.dtype)])


def kernel(x, c, rel_table, ln_g, ln_b, w_ada, b_ada, w_in, b_gate, q_norm_g, k_norm_g, w_pa, w_pb, w_o):
    depth = w_in.shape[0]
    b, s, d = x.shape
    alpha = float((2 * depth) ** 0.25)

    c_pad = jnp.zeros((SUBLANES, d), F32).at[:b].set(c)
    mod = _ada_call(c_pad, w_ada, b_ada)[:, :b]
    cos_t, sin_t = _rope_tables(s)
    idx = jnp.arange(_BD_W) // HEAD_DIM
    bd = (idx[:, None] == idx[None, :]).astype(BF16)
    biases = [_window_bias(rel_table, g) for g in range(len(A_GROUPS))]

    for l in range(depth):
        shift = mod[l, :, 0:d].reshape(b, 1, d)
        scale = mod[l, :, d:2 * d].reshape(b, 1, d)
        gate = mod[l, :, 2 * d:3 * d].reshape(b, 1, d)
        gqk = jnp.concatenate([jnp.tile(q_norm_g[l], B_Q_HEADS),
                               jnp.tile(_pad_heads(k_norm_g[l]), B_KV_HEADS)]).reshape(1, _QK_W)
        a0, a1, a2, bq, bk, bv, gz, st = _inproj_call(
            x, shift, scale, _layout_w_in(w_in[l]), b_gate[l].reshape(1, 2 * d), gqk,
            cos_t, sin_t, bd)
        a0 = a0.reshape(b, 1, s, _A_SEG)
        oa = [_attn_a(a_g, biases[g]) for g, a_g in enumerate((a0, a1, a2))]
        kmax = jnp.sqrt(jnp.max(st[:, :, 0:B_KV_HEADS, 0], axis=1))
        qmax = jnp.sqrt(jnp.max(st[:, :, B_KV_HEADS, 0]))
        bound_ok = qmax * jnp.max(kmax) * B_BOUND_SLACK <= B_MAX_BOUND
        yb = lax.cond(
            bound_ok,
            functools.partial(_attn_b_call, online=False),
            functools.partial(_attn_b_call, online=True),
            kmax.reshape(-1), bq, bk, bv)
        x = _outproj_call(x, gate, oa, yb, gz, w_pa[l].astype(BF16), w_pb[l].astype(BF16),
                          w_o[l].astype(BF16), ln_g[l].reshape(1, d), ln_b[l].reshape(1, d), alpha)
    return x
```

```python
import functools
import math

import jax
import jax.numpy as jnp
from jax import lax
from jax.experimental import pallas as pl
from jax.experimental.pallas import tpu as pltpu

HEAD_DIM = 64
A_GROUPS = ((128, 1), (512, 4), (2048, 16))
A_HEADS_PER_GROUP = 8
A_WIDTH = A_HEADS_PER_GROUP * HEAD_DIM
A_RADIUS = 64
B_Q_HEADS = 8
B_KV_HEADS = 2
B_GROUP = B_Q_HEADS // B_KV_HEADS
B_WIDTH = B_Q_HEADS * HEAD_DIM
GRID_W = 64
ROPE_THETA = 10000.0
REL_BUCKETS = 32
REL_MAX_DISTANCE = 1024
LN_EPS = 1e-5
QK_EPS = 1e-6
NEG_INF = -1e30
LOG2_E = math.log2(math.e)
_Q_SCALE = HEAD_DIM ** -0.5 * LOG2_E

LANES = 128
SUBLANES = 8
VMEM_LIMIT_BYTES = 56 * 1024 * 1024

ADA_TN = 1024
IN_TM = 256
A_TQ = 512
A_SUB = 128
B_TQ = 512
B_TK = 1024
OUT_TM = 512
OUT_SPLIT = 1

B_MAX_BOUND = 50.0
A_MAX_ABS_LOG2_DENOM = 90.0
B_BOUND_SLACK = 1.0 + 2.0 ** -6

BF16 = jnp.bfloat16
F32 = jnp.float32

_NT = (((1,), (1,)), ((), ()))


def _ada_kernel(c_ref, w_ref, b_ref, o_ref):
    c = c_ref[...]
    h = c * jax.nn.sigmoid(c)
    o_ref[0] = jnp.dot(h, w_ref[0], preferred_element_type=F32,
                       precision=lax.Precision.HIGHEST) + b_ref[0]


def _ada_call(c_pad, w_ada, b_ada):
    depth, d, n3 = w_ada.shape
    rows = c_pad.shape[0]
    return pl.pallas_call(
        _ada_kernel,
        out_shape=jax.ShapeDtypeStruct((depth, rows, n3), F32),
        grid=(depth, n3 // ADA_TN),
        in_specs=[
            pl.BlockSpec((rows, d), lambda l, j: (0, 0)),
            pl.BlockSpec((1, d, ADA_TN), lambda l, j: (l, 0, j)),
            pl.BlockSpec((1, 1, ADA_TN), lambda l, j: (l, 0, j)),
        ],
        out_specs=pl.BlockSpec((1, rows, ADA_TN), lambda l, j: (l, 0, j)),
        compiler_params=pltpu.CompilerParams(
            dimension_semantics=("arbitrary", "arbitrary")),
        name="ada_mod",
    )(c_pad, w_ada, b_ada.reshape(depth, 1, n3))


_A_SEG = 3 * A_WIDTH
_A_PART = len(A_GROUPS) * A_WIDTH
_OFF_AZ = 3 * _A_PART
_OFF_BQ = _OFF_AZ + A_WIDTH
_KV_PAD = B_KV_HEADS * LANES
_OFF_BZ = _OFF_BQ + B_WIDTH + 2 * B_KV_HEADS * HEAD_DIM
_OFF_GL = _OFF_BZ + B_WIDTH
_QK_W = B_WIDTH + _KV_PAD
_BD_W = 2 * LANES
_ONE_LANE = HEAD_DIM


def _inproj_kernel(x_ref, shift_ref, scale_ref, w_ref, wkv_ref, bgate_ref, gqk_ref,
                   cos_ref, sin_ref, bd_ref,
                   a0_ref, a1_ref, a2_ref, bqt_ref, qn_ref, bk_ref, bvt_ref, gz_ref, st_ref,
                   scr_ref):
    tm = x_ref.shape[1]
    d_model = x_ref.shape[2]
    u = (x_ref[0] * (1.0 + scale_ref[0]) + shift_ref[0]).astype(BF16)

    def proj(off, width):
        return jnp.dot(u, w_ref[:, off:off + width], preferred_element_type=F32)

    kv_lane = lax.broadcasted_iota(jnp.int32, (1, _KV_PAD), 1)
    one_col = ((kv_lane & (LANES - 1)) == _ONE_LANE).astype(F32)

    qk = jnp.concatenate(
        [proj(_OFF_BQ, B_WIDTH),
         jnp.dot(u, wkv_ref[:, 0:_KV_PAD], preferred_element_type=F32)], axis=1)
    sq = qk * qk
    sq_hi = sq.astype(BF16)
    sq_lo = (sq - sq_hi.astype(F32)).astype(BF16)
    def head_sums(part):
        return jnp.concatenate(
            [jnp.dot(part[:, c:c + _BD_W], bd_ref[...], preferred_element_type=F32)
             for c in range(0, _QK_W, _BD_W)], axis=1)
    ss = head_sums(sq_hi) + head_sums(sq_lo)
    y = qk * lax.rsqrt(ss * (1.0 / HEAD_DIM) + QK_EPS) * gqk_ref[...]
    reps = _QK_W // LANES
    cos = jnp.concatenate([cos_ref[...]] * reps, axis=1)
    sin = jnp.concatenate([sin_ref[...]] * reps, axis=1)
    lane = lax.broadcasted_iota(jnp.int32, y.shape, 1)
    first = (lane & 31) < 16
    swapped = jnp.where(first, pltpu.roll(y, _QK_W - 16, 1), pltpu.roll(y, 16, 1))
    rot = y * cos + swapped * sin
    q = rot[:, :B_WIDTH] * _Q_SCALE
    k = rot[:, B_WIDTH:]
    bk_ref[0] = (k + one_col).astype(BF16)
    q_t = q.T
    bqt_ref[0] = q_t.astype(BF16)
    qn2 = jnp.concatenate(
        [jnp.sum(jnp.square(q_t[h * HEAD_DIM:(h + 1) * HEAD_DIM]), axis=0, keepdims=True)
         for h in range(B_Q_HEADS)], axis=0)
    qn_ref[0] = qn2

    def tile_max(v2):
        return jnp.max(jnp.sum(v2, axis=1, keepdims=True), axis=0, keepdims=True)
    k2 = k * k
    kmax = [tile_max(k2[:, h * LANES:(h + 1) * LANES]) for h in range(B_KV_HEADS)]
    qmax = jnp.max(jnp.max(qn2, axis=1, keepdims=True), axis=0, keepdims=True)
    sub = lax.broadcasted_iota(jnp.int32, (SUBLANES, LANES), 0)
    st_ref[0, 0] = jnp.where(sub == 0, kmax[0], jnp.where(sub == 1, kmax[1], qmax))

    def proj_a(g):
        chunks = []
        for part in range(3):
            res = proj(part * _A_PART + g * A_WIDTH, A_WIDTH)
            if part == 0:
                res = res * _Q_SCALE
            chunks += [res[:, ch * LANES:(ch + 1) * LANES] for ch in range(A_WIDTH // LANES)]
        return chunks

    for g, a_ref in ((1, a1_ref), (2, a2_ref)):
        dil = A_GROUPS[g][1]
        for ch, chunk in enumerate(proj_a(g)):
            scr_ref[ch] = chunk
        for r in range(dil):
            for ch in range(_A_SEG // LANES):
                a_ref[0, r, :, ch * LANES:(ch + 1) * LANES] = (
                    scr_ref[ch, pl.ds(r, tm // dil, stride=dil), :].astype(BF16))

    az = proj(_OFF_AZ, A_WIDTH)
    gz_ref[0, :, 0:A_WIDTH] = (az * jax.nn.sigmoid(az)).astype(BF16)
    bz = proj(_OFF_BZ, B_WIDTH)
    gz_ref[0, :, A_WIDTH:A_WIDTH + B_WIDTH] = (bz * jax.nn.sigmoid(bz)).astype(BF16)
    for j in range(2):
        gl = proj(_OFF_GL + j * d_model, d_model) + bgate_ref[:, j * d_model:(j + 1) * d_model]
        o = A_WIDTH + B_WIDTH + j * d_model
        gz_ref[0, :, o:o + d_model] = jax.nn.sigmoid(gl).astype(BF16)

    a0_ref[0] = jnp.concatenate(proj_a(0), axis=1).astype(BF16)
    v_pad = jnp.dot(u, wkv_ref[:, _KV_PAD:2 * _KV_PAD], preferred_element_type=F32)
    bvt_ref[0] = (v_pad + one_col).T.astype(BF16)


def _inproj_call(x, shift, scale, w_all, layer, w_kv, b_gate, gqk, cos_t, sin_t, bd):
    b, s, d = x.shape
    tm = IN_TM
    n_cols = w_all.shape[2]
    d1, d2 = A_GROUPS[1][1], A_GROUPS[2][1]
    const = dict(pipeline_mode=pl.Buffered(1))
    out_shape = (
        jax.ShapeDtypeStruct((b, s, _A_SEG), BF16),
        jax.ShapeDtypeStruct((b, d1, s // d1, _A_SEG), BF16),
        jax.ShapeDtypeStruct((b, d2, s // d2, _A_SEG), BF16),
        jax.ShapeDtypeStruct((b, B_WIDTH, s), BF16),
        jax.ShapeDtypeStruct((b, B_Q_HEADS, s), F32),
        jax.ShapeDtypeStruct((b, s, _KV_PAD), BF16),
        jax.ShapeDtypeStruct((b, _KV_PAD, s), BF16),
        jax.ShapeDtypeStruct((b, s, A_WIDTH + B_WIDTH + 2 * d), BF16),
        jax.ShapeDtypeStruct((b, s // tm, SUBLANES, LANES), F32),
    )
    row = lambda bi, i: (bi, i, 0)
    out_specs = (
        pl.BlockSpec((1, tm, _A_SEG), row),
        pl.BlockSpec((1, d1, tm // d1, _A_SEG), lambda bi, i: (bi, 0, i, 0)),
        pl.BlockSpec((1, d2, tm // d2, _A_SEG), lambda bi, i: (bi, 0, i, 0)),
        pl.BlockSpec((1, B_WIDTH, tm), lambda bi, i: (bi, 0, i)),
        pl.BlockSpec((1, B_Q_HEADS, tm), lambda bi, i: (bi, 0, i)),
        pl.BlockSpec((1, tm, _KV_PAD), row),
        pl.BlockSpec((1, _KV_PAD, tm), lambda bi, i: (bi, 0, i)),
        pl.BlockSpec((1, tm, A_WIDTH + B_WIDTH + 2 * d), row),
        pl.BlockSpec((1, 1, SUBLANES, LANES), lambda bi, i: (bi, i, 0, 0)),
    )
    in_specs = [
        pl.BlockSpec((1, tm, d), row),
        pl.BlockSpec((1, 1, d), lambda bi, i: (bi, 0, 0)),
        pl.BlockSpec((1, 1, d), lambda bi, i: (bi, 0, 0)),
        pl.BlockSpec((None, d, n_cols), lambda bi, i: (layer, 0, 0), **const),
        pl.BlockSpec((d, 2 * _KV_PAD), lambda bi, i: (0, 0), **const),
        pl.BlockSpec((1, 2 * d), lambda bi, i: (0, 0), **const),
        pl.BlockSpec((1, _QK_W), lambda bi, i: (0, 0), **const),
        pl.BlockSpec((tm, LANES), lambda bi, i: (i, 0)),
        pl.BlockSpec((tm, LANES), lambda bi, i: (i, 0)),
        pl.BlockSpec((_BD_W, _BD_W), lambda bi, i: (0, 0), **const),
    ]
    return pl.pallas_call(
        _inproj_kernel,
        out_shape=out_shape,
        grid=(b, s // tm),
        in_specs=in_specs,
        out_specs=out_specs,
        scratch_shapes=[pltpu.VMEM((_A_SEG // LANES, tm, LANES), F32)],
        compiler_params=pltpu.CompilerParams(
            dimension_semantics=("arbitrary", "arbitrary"),
            vmem_limit_bytes=VMEM_LIMIT_BYTES),
        name="inproj",
    )(x, shift, scale, w_all, w_kv, b_gate, gqk, cos_t, sin_t, bd)


def _attn_a_kernel(q_ref, kp_ref, kc_ref, kn_ref, vp_ref, vc_ref, vn_ref, bias_ref,
                   o_ref, lse_ref, range_ref, *, phase_len, stabilise):
    tq = q_ref.shape[2]
    i = pl.program_id(2)
    q = q_ref[0, 0]
    k_all = jnp.concatenate([kp_ref[0, 0], kc_ref[0, 0], kn_ref[0, 0]], axis=0)
    v_all = jnp.concatenate([vp_ref[0, 0], vc_ref[0, 0], vn_ref[0, 0]], axis=0)
    tk = A_SUB + 2 * A_RADIUS
    n_sub = tq // A_SUB
    lane = lax.broadcasted_iota(jnp.int32, (A_SUB, LANES), 1)
    low = lane < HEAD_DIM
    kcol = lax.broadcasted_iota(jnp.int32, (1, tk), 1)
    worst = jnp.zeros((A_SUB, LANES), F32)
    for sub in range(n_sub):
        r0 = sub * A_SUB
        at_edge = sub == 0 or sub == n_sub - 1
        if at_edge:
            krow = i * tq + (r0 - A_RADIUS) + kcol
            valid = (krow >= 0) & (krow < phase_len)
        for pair in range(A_HEADS_PER_GROUP // 2):
            c0 = pair * LANES
            qp = q[r0:r0 + A_SUB, c0:c0 + LANES]
            kp = k_all[r0:r0 + tk, c0:c0 + LANES]
            vp = v_all[r0:r0 + tk, c0:c0 + LANES]
            zero = jnp.zeros_like(qp)
            qs = jnp.concatenate([jnp.where(low, qp, zero), jnp.where(low, zero, qp)], axis=0)
            s = lax.dot_general(qs, kp, _NT, preferred_element_type=F32) + bias_ref[pair]
            if at_edge:
                s = jnp.where(valid, s, NEG_INF)
            if stabilise:
                m = jnp.max(s, axis=1, keepdims=True)
                s = s - m
            p = jnp.exp2(s)
            l = jnp.sum(p, axis=1, keepdims=True)
            o = jnp.dot(p.astype(BF16), vp, preferred_element_type=F32)
            pick = lambda a: jnp.where(low, a[:A_SUB], a[A_SUB:])
            l = pick(l)
            lse = jnp.log2(l)
            worst = jnp.maximum(worst, jnp.abs(lse))
            if stabilise:
                lse = lse + pick(m)
            o_ref[0, 0, r0:r0 + A_SUB, c0:c0 + LANES] = pick(o) * (1.0 / l)
            lse_ref[0, 0, r0:r0 + A_SUB, c0:c0 + LANES] = lse
    range_ref[0, 0, 0] = jnp.broadcast_to(
        jnp.max(jnp.max(worst, axis=1, keepdims=True), axis=0, keepdims=True), (SUBLANES, LANES))


def _attn_a_call(a_g, bias_g, *, stabilise):
    b, dil, phase_len, _ = a_g.shape
    tq = min(A_TQ, phase_len)
    n_steps = phase_len // tq
    halo = A_RADIUS
    nh = phase_len // halo
    per = tq // halo
    cur = lambda col: (lambda bi, r, i: (bi, r, i, col))
    prev = lambda col: (lambda bi, r, i: (bi, r, jnp.maximum(i * per - 1, 0), col))
    nxt = lambda col: (lambda bi, r, i: (bi, r, jnp.minimum((i + 1) * per, nh - 1), col))
    blk = (1, 1, tq, A_WIDTH)
    hblk = (1, 1, halo, A_WIDTH)
    out_sds = jax.ShapeDtypeStruct((b, dil, phase_len, A_WIDTH), F32)
    range_sds = jax.ShapeDtypeStruct((b, dil, n_steps, SUBLANES, LANES), F32)
    return pl.pallas_call(
        functools.partial(_attn_a_kernel, phase_len=phase_len, stabilise=stabilise),
        out_shape=(out_sds, out_sds, range_sds),
        grid=(b, dil, n_steps),
        in_specs=[
            pl.BlockSpec(blk, cur(0)),
            pl.BlockSpec(hblk, prev(1)), pl.BlockSpec(blk, cur(1)), pl.BlockSpec(hblk, nxt(1)),
            pl.BlockSpec(hblk, prev(2)), pl.BlockSpec(blk, cur(2)), pl.BlockSpec(hblk, nxt(2)),
            pl.BlockSpec(bias_g.shape, lambda bi, r, i: (0, 0, 0)),
        ],
        out_specs=(pl.BlockSpec(blk, cur(0)), pl.BlockSpec(blk, cur(0)),
                   pl.BlockSpec((1, 1, 1, SUBLANES, LANES), lambda bi, r, i: (bi, r, i, 0, 0))),
        compiler_params=pltpu.CompilerParams(
            dimension_semantics=("arbitrary", "arbitrary", "arbitrary"),
            vmem_limit_bytes=VMEM_LIMIT_BYTES),
        name=f"attn_a_d{dil}" + ("_stab" if stabilise else ""),
    )(a_g, a_g, a_g, a_g, a_g, a_g, a_g, bias_g)


def _attn_a(a_g, bias_g):
    o, lse, worst = _attn_a_call(a_g, bias_g, stabilise=False)
    in_range = jnp.max(worst) <= A_MAX_ABS_LOG2_DENOM
    return lax.cond(in_range,
                    lambda: (o, lse),
                    lambda: _attn_a_call(a_g, bias_g, stabilise=True)[:2])


_VT_ROWS = 80


def _attn_b_kernel(kmax_ref, qt_ref, qn_ref, k_ref, vt_ref, o_ref, qst_ref, acc_ref, *m_scratch,
                   online):
    tq = qt_ref.shape[2]
    seq = k_ref.shape[1]
    kmax = kmax_ref[pl.program_id(0) * B_KV_HEADS + pl.program_id(1)]
    tail_row = lax.broadcasted_iota(jnp.int32, (LANES - HEAD_DIM, tq), 0)
    for h in range(B_GROUP):
        cols = slice(h * tq, (h + 1) * tq)
        qst_ref[0:HEAD_DIM, cols] = qt_ref[0, h * HEAD_DIM:(h + 1) * HEAD_DIM, :]
        if online:
            tail = jnp.zeros(tail_row.shape, F32)
        else:
            bound = jnp.sqrt(qn_ref[0, 0, h:h + 1, :]) * (kmax * B_BOUND_SLACK)
            tail = jnp.where(tail_row == _ONE_LANE - HEAD_DIM, -bound, 0.0)
        qst_ref[HEAD_DIM:LANES, cols] = tail.astype(BF16)
    acc_ref[...] = jnp.zeros(acc_ref.shape, F32)
    if online:
        m_ref, = m_scratch
        m_ref[...] = jnp.full(m_ref.shape, NEG_INF, F32)

    n_chunks = seq // B_TK

    def scores_t(c):
        start = pl.multiple_of(c * B_TK, B_TK)
        return jnp.dot(k_ref[0, pl.ds(start, B_TK), :], qst_ref[...],
                       preferred_element_type=F32)

    def values_t(c):
        start = pl.multiple_of(c * B_TK, B_TK)
        return vt_ref[0, 0:_VT_ROWS, pl.ds(start, B_TK)]

    if online:
        def body(c, carry):
            s_t = scores_t(c)
            m_prev = m_ref[...]
            m_new = jnp.maximum(m_prev, jnp.max(s_t, axis=0, keepdims=True))
            p_t = jnp.exp2(s_t - m_new).astype(BF16)
            acc_ref[...] = (jnp.exp2(m_prev - m_new) * acc_ref[...]
                            + jnp.dot(values_t(c), p_t, preferred_element_type=F32))
            m_ref[...] = m_new
            return carry

        lax.fori_loop(0, n_chunks, body, 0)
    else:
        p_ref, = m_scratch

        def fill(slot, c):
            p_ref[slot] = jnp.exp2(scores_t(c)).astype(BF16)

        def drain(slot, c):
            acc_ref[...] += jnp.dot(values_t(c), p_ref[slot], preferred_element_type=F32)

        fill(0, 0)

        def body(j, carry):
            c = 2 * j
            fill(1, c + 1)
            drain(0, c)
            fill(0, c + 2)
            drain(1, c + 1)
            return carry

        lax.fori_loop(0, n_chunks // 2 - 1, body, 0)
        fill(1, n_chunks - 1)
        drain(0, n_chunks - 2)
        drain(1, n_chunks - 1)
    acc = acc_ref[...]
    o_t = acc[0:HEAD_DIM, :] * (1.0 / acc[_ONE_LANE:_ONE_LANE + 1, :])
    for h in range(B_GROUP):
        o_ref[0, h * HEAD_DIM:(h + 1) * HEAD_DIM, :] = o_t[:, h * tq:(h + 1) * tq].astype(BF16)


def _attn_b_call(kmax, bqt, qn2, bk, bvt, *, online):
    b, _, s = bqt.shape
    qw = B_GROUP * HEAD_DIM
    cols = B_GROUP * B_TQ
    scratch = [pltpu.VMEM((LANES, cols), BF16), pltpu.VMEM((_VT_ROWS, cols), F32)]
    scratch.append(pltpu.VMEM((1, cols), F32) if online else pltpu.VMEM((2, B_TK, cols), BF16))
    return pl.pallas_call(
        functools.partial(_attn_b_kernel, online=online),
        out_shape=jax.ShapeDtypeStruct((b, B_WIDTH, s), BF16),
        grid_spec=pltpu.PrefetchScalarGridSpec(
            num_scalar_prefetch=1,
            grid=(b, B_KV_HEADS, s // B_TQ),
            in_specs=[
                pl.BlockSpec((1, qw, B_TQ), lambda bi, h, i, km: (bi, h, i)),
                pl.BlockSpec((1, 1, B_GROUP, B_TQ), lambda bi, h, i, km: (bi, h, 0, i)),
                pl.BlockSpec((1, s, LANES), lambda bi, h, i, km: (bi, 0, h)),
                pl.BlockSpec((1, LANES, s), lambda bi, h, i, km: (bi, h, 0)),
            ],
            out_specs=pl.BlockSpec((1, qw, B_TQ), lambda bi, h, i, km: (bi, h, i)),
            scratch_shapes=scratch,
        ),
        compiler_params=pltpu.CompilerParams(
            dimension_semantics=("arbitrary", "arbitrary", "arbitrary"),
            vmem_limit_bytes=VMEM_LIMIT_BYTES),
        name="attn_b_online" if online else "attn_b",
    )(kmax, bqt, qn2.reshape(b, B_KV_HEADS, B_GROUP, s), bk, bvt)


def _outproj_kernel(x_ref, gate_ref, o0_ref, l0_ref, o1_ref, l1_ref, o2_ref, l2_ref,
                    ybt_ref, gz_ref, wpa_ref, wpb_ref, wo_ref, lng_ref, lnb_ref,
                    out_ref, so1, sl1, so2, sl2, *, alpha):
    tm = x_ref.shape[1]
    d_model = x_ref.shape[2]
    for (o_ref, l_ref, so, sl, g) in ((o1_ref, l1_ref, so1, sl1, 1), (o2_ref, l2_ref, so2, sl2, 2)):
        dil = A_GROUPS[g][1]
        for r in range(dil):
            for ch in range(A_WIDTH // LANES):
                cs = slice(ch * LANES, (ch + 1) * LANES)
                so[ch, pl.ds(r, tm // dil, stride=dil), :] = o_ref[0, r, :, cs]
                sl[ch, pl.ds(r, tm // dil, stride=dil), :] = l_ref[0, r, :, cs]
    rows = tm // OUT_SPLIT
    for part in range(OUT_SPLIT):
        rs = slice(part * rows, (part + 1) * rows)
        cat = lambda ref: jnp.concatenate([ref[ch, rs, :] for ch in range(A_WIDTH // LANES)], axis=1)
        l0, l1, l2 = l0_ref[0, rs, :], cat(sl1), cat(sl2)
        mx = jnp.maximum(jnp.maximum(l0, l1), l2)
        e0, e1, e2 = jnp.exp2(l0 - mx), jnp.exp2(l1 - mx), jnp.exp2(l2 - mx)
        y_a = (e0 * o0_ref[0, rs, :] + e1 * cat(so1) + e2 * cat(so2)) / (e0 + e1 + e2)
        gz = gz_ref[0, rs, :]
        ya = (y_a * gz[:, 0:A_WIDTH].astype(F32)).astype(BF16)
        y_b = ybt_ref[0, :, rs].astype(F32).T
        yb = (y_b * gz[:, A_WIDTH:A_WIDTH + B_WIDTH].astype(F32)).astype(BF16)
        pa = jnp.dot(ya, wpa_ref[...], preferred_element_type=F32)
        pb = jnp.dot(yb, wpb_ref[...], preferred_element_type=F32)
        o = A_WIDTH + B_WIDTH
        g_a = gz[:, o:o + d_model].astype(F32)
        g_b = gz[:, o + d_model:o + 2 * d_model].astype(F32)
        merged = (g_a * pa + g_b * pb).astype(BF16)
        out = jnp.dot(merged, wo_ref[...], preferred_element_type=F32)
        h = alpha * x_ref[0, rs, :] + gate_ref[0] * out
        mu = jnp.mean(h, axis=-1, keepdims=True)
        hc = h - mu
        var = jnp.mean(hc * hc, axis=-1, keepdims=True)
        out_ref[0, rs, :] = hc * lax.rsqrt(var + LN_EPS) * lng_ref[...] + lnb_ref[...]


def _outproj_call(x, gate, oa, ybt, gz, w_pa, w_pb, w_o, ln_g, ln_b, alpha):
    b, s, d = x.shape
    tm = OUT_TM
    (o0, l0), (o1, l1), (o2, l2) = oa
    d1, d2 = A_GROUPS[1][1], A_GROUPS[2][1]
    row = lambda bi, i: (bi, i, 0)
    ph = lambda bi, i: (bi, 0, i, 0)
    fixed = lambda bi, i: (0, 0)
    s0 = pl.BlockSpec((1, 1, tm, A_WIDTH), ph)
    s1 = pl.BlockSpec((1, d1, tm // d1, A_WIDTH), ph)
    s2 = pl.BlockSpec((1, d2, tm // d2, A_WIDTH), ph)

    def kern(x_ref, gate_ref, o0_ref, l0_ref, *rest):
        return _outproj_kernel(x_ref, gate_ref, o0_ref.at[0], l0_ref.at[0], *rest, alpha=alpha)

    return pl.pallas_call(
        kern,
        out_shape=jax.ShapeDtypeStruct((b, s, d), F32),
        grid=(b, s // tm),
        in_specs=[
            pl.BlockSpec((1, tm, d), row),
            pl.BlockSpec((1, 1, d), lambda bi, i: (bi, 0, 0)),
            s0, s0, s1, s1, s2, s2,
            pl.BlockSpec((1, B_WIDTH, tm), lambda bi, i: (bi, 0, i)),
            pl.BlockSpec((1, tm, gz.shape[2]), row),
            pl.BlockSpec(w_pa.shape, fixed),
            pl.BlockSpec(w_pb.shape, fixed),
            pl.BlockSpec(w_o.shape, fixed),
            pl.BlockSpec((1, d), fixed),
            pl.BlockSpec((1, d), fixed),
        ],
        out_specs=pl.BlockSpec((1, tm, d), row),
        scratch_shapes=[pltpu.VMEM((A_WIDTH // LANES, tm, LANES), F32)] * 4,
        compiler_params=pltpu.CompilerParams(
            dimension_semantics=("arbitrary", "arbitrary"),
            vmem_limit_bytes=VMEM_LIMIT_BYTES),
        name="outproj",
    )(x, gate, o0, l0, o1, l1, o2, l2, ybt, gz, w_pa, w_pb, w_o, ln_g, ln_b)


def _t5_bucket(rel):
    half = REL_BUCKETS // 2
    max_exact = half // 2
    ret = jnp.where(rel > 0, half, 0)
    a = jnp.abs(rel)
    af = jnp.maximum(a, 1).astype(F32)
    large = max_exact + (jnp.log(af / max_exact) / math.log(REL_MAX_DISTANCE / max_exact)
                         * (half - max_exact)).astype(jnp.int32)
    large = jnp.minimum(large, half - 1)
    return ret + jnp.where(a < max_exact, a, large)


def _window_bias(rel_table, g):
    dil = A_GROUPS[g][1]
    tk = A_SUB + 2 * A_RADIUS
    rel = jnp.arange(tk)[None, :] - A_RADIUS - jnp.arange(A_SUB)[:, None]
    table_g = rel_table[:, g * A_HEADS_PER_GROUP:(g + 1) * A_HEADS_PER_GROUP]
    onehot = (_t5_bucket(rel * dil)[..., None] == jnp.arange(REL_BUCKETS)).astype(F32)
    bias = jnp.einsum("qkb,bh->hqk", onehot, table_g.astype(F32), precision=lax.Precision.HIGHEST)
    bias = jnp.where((jnp.abs(rel) <= A_RADIUS)[None], bias * LOG2_E, NEG_INF)
    return bias.reshape(A_HEADS_PER_GROUP // 2, 2 * A_SUB, tk)


def _rope_tables(seq):
    t = jnp.arange(seq)
    row = (t // GRID_W).astype(F32)
    col = (t % GRID_W).astype(F32)
    half = HEAD_DIM // 2
    inv = ROPE_THETA ** (-jnp.arange(0, half, 2, dtype=F32) / half)
    ar, ac = row[:, None] * inv[None], col[:, None] * inv[None]
    cos = jnp.concatenate([jnp.cos(ar), jnp.cos(ar), jnp.cos(ac), jnp.cos(ac)], axis=1)
    sin = jnp.concatenate([-jnp.sin(ar), jnp.sin(ar), -jnp.sin(ac), jnp.sin(ac)], axis=1)
    return jnp.tile(cos, (1, LANES // HEAD_DIM)), jnp.tile(sin, (1, LANES // HEAD_DIM))


def _padded_kv_weights(w):
    d = w.shape[0]
    off = _OFF_BQ + B_WIDTH
    zeros = jnp.zeros((d, LANES - HEAD_DIM), w.dtype)
    blocks = []
    for h in range(2 * B_KV_HEADS):
        blocks += [w[:, off + h * HEAD_DIM:off + (h + 1) * HEAD_DIM], zeros]
    return jnp.concatenate(blocks, axis=1).astype(BF16)


def _pad_heads(g):
    return jnp.concatenate([g, jnp.zeros((LANES - HEAD_DIM,), g.dtype)])


def kernel(x, c, rel_table, ln_g, ln_b, w_ada, b_ada, w_in, b_gate, q_norm_g, k_norm_g, w_pa, w_pb, w_o):
    depth = w_in.shape[0]
    b, s, d = x.shape
    alpha = float((2 * depth) ** 0.25)

    c_pad = jnp.zeros((SUBLANES, d), F32).at[:b].set(c)
    mod = _ada_call(c_pad, w_ada, b_ada)[:, :b]
    cos_t, sin_t = _rope_tables(s)
    idx = jnp.arange(_BD_W) // HEAD_DIM
    bd = (idx[:, None] == idx[None, :]).astype(BF16)
    biases = [_window_bias(rel_table, g) for g in range(len(A_GROUPS))]
    w_all = w_in.astype(BF16)

    for l in range(depth):
        shift = mod[l, :, 0:d].reshape(b, 1, d)
        scale = mod[l, :, d:2 * d].reshape(b, 1, d)
        gate = mod[l, :, 2 * d:3 * d].reshape(b, 1, d)
        gqk = jnp.concatenate([jnp.tile(q_norm_g[l], B_Q_HEADS),
                               jnp.tile(_pad_heads(k_norm_g[l]), B_KV_HEADS)]).reshape(1, _QK_W)
        a0, a1, a2, bqt, qn2, bk, bvt, gz, st = _inproj_call(
            x, shift, scale, w_all, l, _padded_kv_weights(w_in[l]), b_gate[l].reshape(1, 2 * d), gqk,
            cos_t, sin_t, bd)
        a0 = a0.reshape(b, 1, s, _A_SEG)
        oa = [_attn_a(a_g, biases[g]) for g, a_g in enumerate((a0, a1, a2))]
        kmax = jnp.sqrt(jnp.max(st[:, :, 0:B_KV_HEADS, 0], axis=1))
        qmax = jnp.sqrt(jnp.max(st[:, :, B_KV_HEADS, 0]))
        bound_ok = qmax * jnp.max(kmax) * B_BOUND_SLACK <= B_MAX_BOUND
        ybt = lax.cond(
            bound_ok,
            functools.partial(_attn_b_call, online=False),
            functools.partial(_attn_b_call, online=True),
            kmax.reshape(-1), bqt, qn2, bk, bvt)
        x = _outproj_call(x, gate, oa, ybt, gz, w_pa[l].astype(BF16), w_pb[l].astype(BF16),
                          w_o[l].astype(BF16), ln_g[l].reshape(1, d), ln_b[l].reshape(1, d), alpha)
    return x
```

```python
import functools
import math

import jax
import jax.numpy as jnp
from jax import lax
from jax.experimental import pallas as pl
from jax.experimental.pallas import tpu as pltpu

HEAD_DIM = 64
A_GROUPS = ((128, 1), (512, 4), (2048, 16))
A_HEADS_PER_GROUP = 8
A_WIDTH = A_HEADS_PER_GROUP * HEAD_DIM
A_RADIUS = 64
B_Q_HEADS = 8
B_KV_HEADS = 2
B_GROUP = B_Q_HEADS // B_KV_HEADS
B_WIDTH = B_Q_HEADS * HEAD_DIM
GRID_W = 64
ROPE_THETA = 10000.0
REL_BUCKETS = 32
REL_MAX_DISTANCE = 1024
LN_EPS = 1e-5
QK_EPS = 1e-6
NEG_INF = -1e30
LOG2_E = math.log2(math.e)
_Q_SCALE = HEAD_DIM ** -0.5 * LOG2_E

LANES = 128
SUBLANES = 8
VMEM_LIMIT_BYTES = 56 * 1024 * 1024

ADA_TN = 1024
IN_TM = 256
A_TQ = 2048
A_SUB = 128
B_TQ = 512
B_TK = 2048
OUT_TM = 512
OUT_SPLIT = 1

B_MAX_BOUND = 50.0
A_MAX_ABS_LOG2_DENOM = 90.0
B_BOUND_SLACK = 1.0 + 2.0 ** -6

BF16 = jnp.bfloat16
F32 = jnp.float32

_NT = (((1,), (1,)), ((), ()))


def _ada_kernel(c_ref, w_ref, b_ref, o_ref):
    c = c_ref[...]
    h = c * jax.nn.sigmoid(c)
    o_ref[0] = jnp.dot(h, w_ref[0], preferred_element_type=F32,
                       precision=lax.Precision.HIGHEST) + b_ref[0]


def _ada_call(c_pad, w_ada, b_ada):
    depth, d, n3 = w_ada.shape
    rows = c_pad.shape[0]
    return pl.pallas_call(
        _ada_kernel,
        out_shape=jax.ShapeDtypeStruct((depth, rows, n3), F32),
        grid=(depth, n3 // ADA_TN),
        in_specs=[
            pl.BlockSpec((rows, d), lambda l, j: (0, 0)),
            pl.BlockSpec((1, d, ADA_TN), lambda l, j: (l, 0, j)),
            pl.BlockSpec((1, 1, ADA_TN), lambda l, j: (l, 0, j)),
        ],
        out_specs=pl.BlockSpec((1, rows, ADA_TN), lambda l, j: (l, 0, j)),
        compiler_params=pltpu.CompilerParams(
            dimension_semantics=("arbitrary", "arbitrary")),
        name="ada_mod",
    )(c_pad, w_ada, b_ada.reshape(depth, 1, n3))


_A_SEG = 3 * A_WIDTH
_A_PART = len(A_GROUPS) * A_WIDTH
_OFF_AZ = 3 * _A_PART
_OFF_BQ = _OFF_AZ + A_WIDTH
_KV_PAD = B_KV_HEADS * LANES
_OFF_BZ = _OFF_BQ + B_WIDTH + 2 * B_KV_HEADS * HEAD_DIM
_OFF_GL = _OFF_BZ + B_WIDTH
_QK_W = B_WIDTH + _KV_PAD
_BD_W = 2 * LANES
_ONE_LANE = HEAD_DIM


def _inproj_kernel(x_ref, shift_ref, scale_ref, w_ref, wkv_ref, bgate_ref, gqk_ref,
                   cos_ref, sin_ref, bd_ref,
                   a0_ref, a1_ref, a2_ref, bqt_ref, qn_ref, bk_ref, bvt_ref, gz_ref, st_ref,
                   scr_ref):
    tm = x_ref.shape[1]
    d_model = x_ref.shape[2]
    u = (x_ref[0] * (1.0 + scale_ref[0]) + shift_ref[0]).astype(BF16)

    def proj(off, width):
        return jnp.dot(u, w_ref[:, off:off + width], preferred_element_type=F32)

    kv_lane = lax.broadcasted_iota(jnp.int32, (1, _KV_PAD), 1)
    one_col = ((kv_lane & (LANES - 1)) == _ONE_LANE).astype(F32)

    qk = jnp.concatenate(
        [proj(_OFF_BQ, B_WIDTH),
         jnp.dot(u, wkv_ref[:, 0:_KV_PAD], preferred_element_type=F32)], axis=1)
    sq = qk * qk
    sq_hi = sq.astype(BF16)
    sq_lo = (sq - sq_hi.astype(F32)).astype(BF16)
    def head_sums(part):
        return jnp.concatenate(
            [jnp.dot(part[:, c:c + _BD_W], bd_ref[...], preferred_element_type=F32)
             for c in range(0, _QK_W, _BD_W)], axis=1)
    ss = head_sums(sq_hi) + head_sums(sq_lo)
    y = qk * lax.rsqrt(ss * (1.0 / HEAD_DIM) + QK_EPS) * gqk_ref[...]
    reps = _QK_W // LANES
    cos = jnp.concatenate([cos_ref[...]] * reps, axis=1)
    sin = jnp.concatenate([sin_ref[...]] * reps, axis=1)
    lane = lax.broadcasted_iota(jnp.int32, y.shape, 1)
    first = (lane & 31) < 16
    swapped = jnp.where(first, pltpu.roll(y, _QK_W - 16, 1), pltpu.roll(y, 16, 1))
    rot = y * cos + swapped * sin
    q = rot[:, :B_WIDTH] * _Q_SCALE
    k = rot[:, B_WIDTH:]
    bk_ref[0] = (k + one_col).astype(BF16)
    q_t = q.T
    bqt_ref[0] = q_t.astype(BF16)
    qn2 = jnp.concatenate(
        [jnp.sum(jnp.square(q_t[h * HEAD_DIM:(h + 1) * HEAD_DIM]), axis=0, keepdims=True)
         for h in range(B_Q_HEADS)], axis=0)
    qn_ref[0] = qn2

    def tile_max(v2):
        return jnp.max(jnp.sum(v2, axis=1, keepdims=True), axis=0, keepdims=True)
    k2 = k * k
    kmax = [tile_max(k2[:, h * LANES:(h + 1) * LANES]) for h in range(B_KV_HEADS)]
    qmax = jnp.max(jnp.max(qn2, axis=1, keepdims=True), axis=0, keepdims=True)
    sub = lax.broadcasted_iota(jnp.int32, (SUBLANES, LANES), 0)
    st_ref[0, 0] = jnp.where(sub == 0, kmax[0], jnp.where(sub == 1, kmax[1], qmax))

    def proj_a(g):
        chunks = []
        for part in range(3):
            res = proj(part * _A_PART + g * A_WIDTH, A_WIDTH)
            if part == 0:
                res = res * _Q_SCALE
            chunks += [res[:, ch * LANES:(ch + 1) * LANES] for ch in range(A_WIDTH // LANES)]
        return chunks

    for g, a_ref in ((1, a1_ref), (2, a2_ref)):
        dil = A_GROUPS[g][1]
        for ch, chunk in enumerate(proj_a(g)):
            scr_ref[ch] = chunk
        for r in range(dil):
            for ch in range(_A_SEG // LANES):
                a_ref[0, r, :, ch * LANES:(ch + 1) * LANES] = (
                    scr_ref[ch, pl.ds(r, tm // dil, stride=dil), :].astype(BF16))

    az = proj(_OFF_AZ, A_WIDTH)
    gz_ref[0, :, 0:A_WIDTH] = (az * jax.nn.sigmoid(az)).astype(BF16)
    bz = proj(_OFF_BZ, B_WIDTH)
    gz_ref[0, :, A_WIDTH:A_WIDTH + B_WIDTH] = (bz * jax.nn.sigmoid(bz)).astype(BF16)
    for j in range(2):
        gl = proj(_OFF_GL + j * d_model, d_model) + bgate_ref[:, j * d_model:(j + 1) * d_model]
        o = A_WIDTH + B_WIDTH + j * d_model
        gz_ref[0, :, o:o + d_model] = jax.nn.sigmoid(gl).astype(BF16)

    a0_ref[0] = jnp.concatenate(proj_a(0), axis=1).astype(BF16)
    v_pad = jnp.dot(u, wkv_ref[:, _KV_PAD:2 * _KV_PAD], preferred_element_type=F32)
    bvt_ref[0] = (v_pad + one_col).T.astype(BF16)


def _inproj_call(x, shift, scale, w_all, layer, w_kv, b_gate, gqk, cos_t, sin_t, bd):
    b, s, d = x.shape
    tm = IN_TM
    n_cols = w_all.shape[2]
    d1, d2 = A_GROUPS[1][1], A_GROUPS[2][1]
    const = dict(pipeline_mode=pl.Buffered(1))
    out_shape = (
        jax.ShapeDtypeStruct((b, s, _A_SEG), BF16),
        jax.ShapeDtypeStruct((b, d1, s // d1, _A_SEG), BF16),
        jax.ShapeDtypeStruct((b, d2, s // d2, _A_SEG), BF16),
        jax.ShapeDtypeStruct((b, B_WIDTH, s), BF16),
        jax.ShapeDtypeStruct((b, B_Q_HEADS, s), F32),
        jax.ShapeDtypeStruct((b, s, _KV_PAD), BF16),
        jax.ShapeDtypeStruct((b, _KV_PAD, s), BF16),
        jax.ShapeDtypeStruct((b, s, A_WIDTH + B_WIDTH + 2 * d), BF16),
        jax.ShapeDtypeStruct((b, s // tm, SUBLANES, LANES), F32),
    )
    row = lambda bi, i: (bi, i, 0)
    out_specs = (
        pl.BlockSpec((1, tm, _A_SEG), row),
        pl.BlockSpec((1, d1, tm // d1, _A_SEG), lambda bi, i: (bi, 0, i, 0)),
        pl.BlockSpec((1, d2, tm // d2, _A_SEG), lambda bi, i: (bi, 0, i, 0)),
        pl.BlockSpec((1, B_WIDTH, tm), lambda bi, i: (bi, 0, i)),
        pl.BlockSpec((1, B_Q_HEADS, tm), lambda bi, i: (bi, 0, i)),
        pl.BlockSpec((1, tm, _KV_PAD), row),
        pl.BlockSpec((1, _KV_PAD, tm), lambda bi, i: (bi, 0, i)),
        pl.BlockSpec((1, tm, A_WIDTH + B_WIDTH + 2 * d), row),
        pl.BlockSpec((1, 1, SUBLANES, LANES), lambda bi, i: (bi, i, 0, 0)),
    )
    in_specs = [
        pl.BlockSpec((1, tm, d), row),
        pl.BlockSpec((1, 1, d), lambda bi, i: (bi, 0, 0)),
        pl.BlockSpec((1, 1, d), lambda bi, i: (bi, 0, 0)),
        pl.BlockSpec((None, d, n_cols), lambda bi, i: (layer, 0, 0), **const),
        pl.BlockSpec((d, 2 * _KV_PAD), lambda bi, i: (0, 0), **const),
        pl.BlockSpec((1, 2 * d), lambda bi, i: (0, 0), **const),
        pl.BlockSpec((1, _QK_W), lambda bi, i: (0, 0), **const),
        pl.BlockSpec((tm, LANES), lambda bi, i: (i, 0)),
        pl.BlockSpec((tm, LANES), lambda bi, i: (i, 0)),
        pl.BlockSpec((_BD_W, _BD_W), lambda bi, i: (0, 0), **const),
    ]
    return pl.pallas_call(
        _inproj_kernel,
        out_shape=out_shape,
        grid=(b, s // tm),
        in_specs=in_specs,
        out_specs=out_specs,
        scratch_shapes=[pltpu.VMEM((_A_SEG // LANES, tm, LANES), F32)],
        compiler_params=pltpu.CompilerParams(
            dimension_semantics=("arbitrary", "arbitrary"),
            vmem_limit_bytes=VMEM_LIMIT_BYTES),
        name="inproj",
    )(x, shift, scale, w_all, w_kv, b_gate, gqk, cos_t, sin_t, bd)


def _attn_a_kernel(q_ref, kp_ref, kc_ref, kn_ref, vp_ref, vc_ref, vn_ref, bias_ref,
                   o_ref, lse_ref, range_ref, *, phase_len, stabilise):
    n_ph, tq = q_ref.shape[1], q_ref.shape[2]
    i = pl.program_id(2)
    tk = A_SUB + 2 * A_RADIUS
    n_sub = tq // A_SUB
    lane = lax.broadcasted_iota(jnp.int32, (A_SUB, LANES), 1)
    low = lane < HEAD_DIM
    kcol = lax.broadcasted_iota(jnp.int32, (1, tk), 1)
    worst = jnp.zeros((A_SUB, LANES), F32)
    for ph, sub in [(ph, sub) for ph in range(n_ph) for sub in range(n_sub)]:
        if sub == 0:
            q = q_ref[0, ph]
            k_all = jnp.concatenate([kp_ref[0, ph], kc_ref[0, ph], kn_ref[0, ph]], axis=0)
            v_all = jnp.concatenate([vp_ref[0, ph], vc_ref[0, ph], vn_ref[0, ph]], axis=0)
        r0 = sub * A_SUB
        at_edge = sub == 0 or sub == n_sub - 1
        if at_edge:
            krow = i * tq + (r0 - A_RADIUS) + kcol
            valid = (krow >= 0) & (krow < phase_len)
        for pair in range(A_HEADS_PER_GROUP // 2):
            c0 = pair * LANES
            qp = q[r0:r0 + A_SUB, c0:c0 + LANES]
            kp = k_all[r0:r0 + tk, c0:c0 + LANES]
            vp = v_all[r0:r0 + tk, c0:c0 + LANES]
            zero = jnp.zeros_like(qp)
            qs = jnp.concatenate([jnp.where(low, qp, zero), jnp.where(low, zero, qp)], axis=0)
            s = lax.dot_general(qs, kp, _NT, preferred_element_type=F32) + bias_ref[pair]
            if at_edge:
                s = jnp.where(valid, s, NEG_INF)
            if stabilise:
                m = jnp.max(s, axis=1, keepdims=True)
                s = s - m
            p = jnp.exp2(s)
            l = jnp.sum(p, axis=1, keepdims=True)
            o = jnp.dot(p.astype(BF16), vp, preferred_element_type=F32)
            pick = lambda a: jnp.where(low, a[:A_SUB], a[A_SUB:])
            l = pick(l)
            lse = jnp.log2(l)
            worst = jnp.maximum(worst, jnp.abs(lse))
            if stabilise:
                lse = lse + pick(m)
            o_ref[0, ph, r0:r0 + A_SUB, c0:c0 + LANES] = (pick(o) * (1.0 / l)).astype(BF16)
            lse_ref[0, ph, r0:r0 + A_SUB, c0:c0 + LANES] = lse
    range_ref[0, 0, 0] = jnp.broadcast_to(
        jnp.max(jnp.max(worst, axis=1, keepdims=True), axis=0, keepdims=True), (SUBLANES, LANES))


def _attn_a_call(a_g, bias_g, *, stabilise):
    b, dil, phase_len, _ = a_g.shape
    tq = min(A_TQ, phase_len)
    n_steps = phase_len // tq
    n_ph = min(dil, A_TQ // tq)
    halo = A_RADIUS
    nh = phase_len // halo
    per = tq // halo
    cur = lambda col: (lambda bi, r, i: (bi, r, i, col))
    prev = lambda col: (lambda bi, r, i: (bi, r, jnp.maximum(i * per - 1, 0), col))
    nxt = lambda col: (lambda bi, r, i: (bi, r, jnp.minimum((i + 1) * per, nh - 1), col))
    blk = (1, n_ph, tq, A_WIDTH)
    hblk = (1, n_ph, halo, A_WIDTH)
    o_sds = jax.ShapeDtypeStruct((b, dil, phase_len, A_WIDTH), BF16)
    lse_sds = jax.ShapeDtypeStruct((b, dil, phase_len, A_WIDTH), F32)
    range_sds = jax.ShapeDtypeStruct((b, dil // n_ph, n_steps, SUBLANES, LANES), F32)
    return pl.pallas_call(
        functools.partial(_attn_a_kernel, phase_len=phase_len, stabilise=stabilise),
        out_shape=(o_sds, lse_sds, range_sds),
        grid=(b, dil // n_ph, n_steps),
        in_specs=[
            pl.BlockSpec(blk, cur(0)),
            pl.BlockSpec(hblk, prev(1)), pl.BlockSpec(blk, cur(1)), pl.BlockSpec(hblk, nxt(1)),
            pl.BlockSpec(hblk, prev(2)), pl.BlockSpec(blk, cur(2)), pl.BlockSpec(hblk, nxt(2)),
            pl.BlockSpec(bias_g.shape, lambda bi, r, i: (0, 0, 0)),
        ],
        out_specs=(pl.BlockSpec(blk, cur(0)), pl.BlockSpec(blk, cur(0)),
                   pl.BlockSpec((1, 1, 1, SUBLANES, LANES), lambda bi, r, i: (bi, r, i, 0, 0))),
        compiler_params=pltpu.CompilerParams(
            dimension_semantics=("arbitrary", "arbitrary", "arbitrary"),
            vmem_limit_bytes=VMEM_LIMIT_BYTES),
        name=f"attn_a_d{dil}" + ("_stab" if stabilise else ""),
    )(a_g, a_g, a_g, a_g, a_g, a_g, a_g, bias_g)


def _attn_a(a_g, bias_g):
    o, lse, worst = _attn_a_call(a_g, bias_g, stabilise=False)
    in_range = jnp.max(worst) <= A_MAX_ABS_LOG2_DENOM
    return lax.cond(in_range,
                    lambda: (o, lse),
                    lambda: _attn_a_call(a_g, bias_g, stabilise=True)[:2])


_VT_ROWS = 80


def _attn_b_kernel(kmax_ref, qt_ref, qn_ref, k_ref, vt_ref, o_ref, qst_ref, acc_ref, *m_scratch,
                   online):
    tq = qt_ref.shape[2]
    seq = k_ref.shape[1]
    kmax = kmax_ref[pl.program_id(0) * B_KV_HEADS + pl.program_id(1)]
    tail_row = lax.broadcasted_iota(jnp.int32, (LANES - HEAD_DIM, tq), 0)
    for h in range(B_GROUP):
        cols = slice(h * tq, (h + 1) * tq)
        qst_ref[0:HEAD_DIM, cols] = qt_ref[0, h * HEAD_DIM:(h + 1) * HEAD_DIM, :]
        if online:
            tail = jnp.zeros(tail_row.shape, F32)
        else:
            bound = jnp.sqrt(qn_ref[0, 0, h:h + 1, :]) * (kmax * B_BOUND_SLACK)
            tail = jnp.where(tail_row == _ONE_LANE - HEAD_DIM, -bound, 0.0)
        qst_ref[HEAD_DIM:LANES, cols] = tail.astype(BF16)
    acc_ref[...] = jnp.zeros(acc_ref.shape, F32)
    if online:
        m_ref, = m_scratch
        m_ref[...] = jnp.full(m_ref.shape, NEG_INF, F32)

    n_chunks = seq // B_TK

    def scores_t(c):
        start = pl.multiple_of(c * B_TK, B_TK)
        return jnp.dot(k_ref[0, pl.ds(start, B_TK), :], qst_ref[...],
                       preferred_element_type=F32)

    def values_t(c):
        start = pl.multiple_of(c * B_TK, B_TK)
        return vt_ref[0, 0:_VT_ROWS, pl.ds(start, B_TK)]

    if online:
        def body(c, carry):
            s_t = scores_t(c)
            m_prev = m_ref[...]
            m_new = jnp.maximum(m_prev, jnp.max(s_t, axis=0, keepdims=True))
            p_t = jnp.exp2(s_t - m_new).astype(BF16)
            acc_ref[...] = (jnp.exp2(m_prev - m_new) * acc_ref[...]
                            + jnp.dot(values_t(c), p_t, preferred_element_type=F32))
            m_ref[...] = m_new
            return carry

        lax.fori_loop(0, n_chunks, body, 0)
    else:
        def body(c, carry):
            start = pl.multiple_of(c * B_TK, B_TK)
            kc = k_ref[0, pl.ds(start, B_TK), :]
            vtc = values_t(c)
            for h in range(B_GROUP):
                cols = slice(h * tq, (h + 1) * tq)
                p_t = jnp.exp2(jnp.dot(kc, qst_ref[:, cols],
                                       preferred_element_type=F32)).astype(BF16)
                acc_ref[:, cols] += jnp.dot(vtc, p_t, preferred_element_type=F32)
            return carry

        lax.fori_loop(0, n_chunks, body, 0)
    acc = acc_ref[...]
    o_t = acc[0:HEAD_DIM, :] * (1.0 / acc[_ONE_LANE:_ONE_LANE + 1, :])
    for h in range(B_GROUP):
        o_ref[0, h * HEAD_DIM:(h + 1) * HEAD_DIM, :] = o_t[:, h * tq:(h + 1) * tq].astype(BF16)


def _attn_b_call(kmax, bqt, qn2, bk, bvt, *, online):
    b, _, s = bqt.shape
    qw = B_GROUP * HEAD_DIM
    cols = B_GROUP * B_TQ
    scratch = [pltpu.VMEM((LANES, cols), BF16), pltpu.VMEM((_VT_ROWS, cols), F32)]
    if online:
        scratch.append(pltpu.VMEM((1, cols), F32))
    return pl.pallas_call(
        functools.partial(_attn_b_kernel, online=online),
        out_shape=jax.ShapeDtypeStruct((b, B_WIDTH, s), BF16),
        grid_spec=pltpu.PrefetchScalarGridSpec(
            num_scalar_prefetch=1,
            grid=(b, B_KV_HEADS, s // B_TQ),
            in_specs=[
                pl.BlockSpec((1, qw, B_TQ), lambda bi, h, i, km: (bi, h, i)),
                pl.BlockSpec((1, 1, B_GROUP, B_TQ), lambda bi, h, i, km: (bi, h, 0, i)),
                pl.BlockSpec((1, s, LANES), lambda bi, h, i, km: (bi, 0, h)),
                pl.BlockSpec((1, LANES, s), lambda bi, h, i, km: (bi, h, 0)),
            ],
            out_specs=pl.BlockSpec((1, qw, B_TQ), lambda bi, h, i, km: (bi, h, i)),
            scratch_shapes=scratch,
        ),
        compiler_params=pltpu.CompilerParams(
            dimension_semantics=("arbitrary", "arbitrary", "arbitrary"),
            vmem_limit_bytes=VMEM_LIMIT_BYTES),
        name="attn_b_online" if online else "attn_b",
    )(kmax, bqt, qn2.reshape(b, B_KV_HEADS, B_GROUP, s), bk, bvt)


def _outproj_kernel(x_ref, gate_ref, o0_ref, l0_ref, o1_ref, l1_ref, o2_ref, l2_ref,
                    ybt_ref, gz_ref, wpa_ref, wpb_ref, wo_ref, lng_ref, lnb_ref,
                    out_ref, so1, sl1, so2, sl2, *, alpha):
    tm = x_ref.shape[1]
    d_model = x_ref.shape[2]
    for (o_ref, l_ref, so, sl, g) in ((o1_ref, l1_ref, so1, sl1, 1), (o2_ref, l2_ref, so2, sl2, 2)):
        dil = A_GROUPS[g][1]
        for r in range(dil):
            for ch in range(A_WIDTH // LANES):
                cs = slice(ch * LANES, (ch + 1) * LANES)
                so[ch, pl.ds(r, tm // dil, stride=dil), :] = o_ref[0, r, :, cs].astype(F32)
                sl[ch, pl.ds(r, tm // dil, stride=dil), :] = l_ref[0, r, :, cs]
    rows = tm // OUT_SPLIT
    for part in range(OUT_SPLIT):
        rs = slice(part * rows, (part + 1) * rows)
        cat = lambda ref: jnp.concatenate([ref[ch, rs, :] for ch in range(A_WIDTH // LANES)], axis=1)
        l0, l1, l2 = l0_ref[0, rs, :], cat(sl1), cat(sl2)
        mx = jnp.maximum(jnp.maximum(l0, l1), l2)
        e0, e1, e2 = jnp.exp2(l0 - mx), jnp.exp2(l1 - mx), jnp.exp2(l2 - mx)
        y_a = ((e0 * o0_ref[0, rs, :].astype(F32) + e1 * cat(so1) + e2 * cat(so2))
               * (1.0 / (e0 + e1 + e2)))
        gz = gz_ref[0, rs, :]
        ya = (y_a * gz[:, 0:A_WIDTH].astype(F32)).astype(BF16)
        y_b = ybt_ref[0, :, rs].astype(F32).T
        yb = (y_b * gz[:, A_WIDTH:A_WIDTH + B_WIDTH].astype(F32)).astype(BF16)
        pa = jnp.dot(ya, wpa_ref[...], preferred_element_type=F32)
        pb = jnp.dot(yb, wpb_ref[...], preferred_element_type=F32)
        o = A_WIDTH + B_WIDTH
        g_a = gz[:, o:o + d_model].astype(F32)
        g_b = gz[:, o + d_model:o + 2 * d_model].astype(F32)
        merged = (g_a * pa + g_b * pb).astype(BF16)
        out = jnp.dot(merged, wo_ref[...], preferred_element_type=F32)
        h = alpha * x_ref[0, rs, :] + gate_ref[0] * out
        mu = jnp.mean(h, axis=-1, keepdims=True)
        hc = h - mu
        var = jnp.mean(hc * hc, axis=-1, keepdims=True)
        out_ref[0, rs, :] = hc * lax.rsqrt(var + LN_EPS) * lng_ref[...] + lnb_ref[...]


def _outproj_call(x, gate, oa, ybt, gz, w_pa, w_pb, w_o, ln_g, ln_b, alpha):
    b, s, d = x.shape
    tm = OUT_TM
    (o0, l0), (o1, l1), (o2, l2) = oa
    d1, d2 = A_GROUPS[1][1], A_GROUPS[2][1]
    row = lambda bi, i: (bi, i, 0)
    ph = lambda bi, i: (bi, 0, i, 0)
    fixed = lambda bi, i: (0, 0)
    s0 = pl.BlockSpec((1, 1, tm, A_WIDTH), ph)
    s1 = pl.BlockSpec((1, d1, tm // d1, A_WIDTH), ph)
    s2 = pl.BlockSpec((1, d2, tm // d2, A_WIDTH), ph)

    def kern(x_ref, gate_ref, o0_ref, l0_ref, *rest):
        return _outproj_kernel(x_ref, gate_ref, o0_ref.at[0], l0_ref.at[0], *rest, alpha=alpha)

    return pl.pallas_call(
        kern,
        out_shape=jax.ShapeDtypeStruct((b, s, d), F32),
        grid=(b, s // tm),
        in_specs=[
            pl.BlockSpec((1, tm, d), row),
            pl.BlockSpec((1, 1, d), lambda bi, i: (bi, 0, 0)),
            s0, s0, s1, s1, s2, s2,
            pl.BlockSpec((1, B_WIDTH, tm), lambda bi, i: (bi, 0, i)),
            pl.BlockSpec((1, tm, gz.shape[2]), row),
            pl.BlockSpec(w_pa.shape, fixed),
            pl.BlockSpec(w_pb.shape, fixed),
            pl.BlockSpec(w_o.shape, fixed),
            pl.BlockSpec((1, d), fixed),
            pl.BlockSpec((1, d), fixed),
        ],
        out_specs=pl.BlockSpec((1, tm, d), row),
        scratch_shapes=[pltpu.VMEM((A_WIDTH // LANES, tm, LANES), F32)] * 4,
        compiler_params=pltpu.CompilerParams(
            dimension_semantics=("arbitrary", "arbitrary"),
            vmem_limit_bytes=VMEM_LIMIT_BYTES),
        name="outproj",
    )(x, gate, o0, l0, o1, l1, o2, l2, ybt, gz, w_pa, w_pb, w_o, ln_g, ln_b)


def _t5_bucket(rel):
    half = REL_BUCKETS // 2
    max_exact = half // 2
    ret = jnp.where(rel > 0, half, 0)
    a = jnp.abs(rel)
    af = jnp.maximum(a, 1).astype(F32)
    large = max_exact + (jnp.log(af / max_exact) / math.log(REL_MAX_DISTANCE / max_exact)
                         * (half - max_exact)).astype(jnp.int32)
    large = jnp.minimum(large, half - 1)
    return ret + jnp.where(a < max_exact, a, large)


def _window_bias(rel_table, g):
    dil = A_GROUPS[g][1]
    tk = A_SUB + 2 * A_RADIUS
    rel = jnp.arange(tk)[None, :] - A_RADIUS - jnp.arange(A_SUB)[:, None]
    table_g = rel_table[:, g * A_HEADS_PER_GROUP:(g + 1) * A_HEADS_PER_GROUP]
    onehot = (_t5_bucket(rel * dil)[..., None] == jnp.arange(REL_BUCKETS)).astype(F32)
    bias = jnp.einsum("qkb,bh->hqk", onehot, table_g.astype(F32), precision=lax.Precision.HIGHEST)
    bias = jnp.where((jnp.abs(rel) <= A_RADIUS)[None], bias * LOG2_E, NEG_INF)
    return bias.reshape(A_HEADS_PER_GROUP // 2, 2 * A_SUB, tk)


def _rope_tables(seq):
    t = jnp.arange(seq)
    row = (t // GRID_W).astype(F32)
    col = (t % GRID_W).astype(F32)
    half = HEAD_DIM // 2
    inv = ROPE_THETA ** (-jnp.arange(0, half, 2, dtype=F32) / half)
    ar, ac = row[:, None] * inv[None], col[:, None] * inv[None]
    cos = jnp.concatenate([jnp.cos(ar), jnp.cos(ar), jnp.cos(ac), jnp.cos(ac)], axis=1)
    sin = jnp.concatenate([-jnp.sin(ar), jnp.sin(ar), -jnp.sin(ac), jnp.sin(ac)], axis=1)
    return jnp.tile(cos, (1, LANES // HEAD_DIM)), jnp.tile(sin, (1, LANES // HEAD_DIM))


def _padded_kv_weights(w):
    d = w.shape[0]
    off = _OFF_BQ + B_WIDTH
    zeros = jnp.zeros((d, LANES - HEAD_DIM), w.dtype)
    blocks = []
    for h in range(2 * B_KV_HEADS):
        blocks += [w[:, off + h * HEAD_DIM:off + (h + 1) * HEAD_DIM], zeros]
    return jnp.concatenate(blocks, axis=1).astype(BF16)


def _pad_heads(g):
    return jnp.concatenate([g, jnp.zeros((LANES - HEAD_DIM,), g.dtype)])


def kernel(x, c, rel_table, ln_g, ln_b, w_ada, b_ada, w_in, b_gate, q_norm_g, k_norm_g, w_pa, w_pb, w_o):
    depth = w_in.shape[0]
    b, s, d = x.shape
    alpha = float((2 * depth) ** 0.25)

    c_pad = jnp.zeros((SUBLANES, d), F32).at[:b].set(c)
    mod = _ada_call(c_pad, w_ada, b_ada)[:, :b]
    cos_t, sin_t = _rope_tables(s)
    idx = jnp.arange(_BD_W) // HEAD_DIM
    bd = (idx[:, None] == idx[None, :]).astype(BF16)
    biases = [_window_bias(rel_table, g) for g in range(len(A_GROUPS))]
    w_all = w_in.astype(BF16)

    for l in range(depth):
        shift = mod[l, :, 0:d].reshape(b, 1, d)
        scale = mod[l, :, d:2 * d].reshape(b, 1, d)
        gate = mod[l, :, 2 * d:3 * d].reshape(b, 1, d)
        gqk = jnp.concatenate([jnp.tile(q_norm_g[l], B_Q_HEADS),
                               jnp.tile(_pad_heads(k_norm_g[l]), B_KV_HEADS)]).reshape(1, _QK_W)
        a0, a1, a2, bqt, qn2, bk, bvt, gz, st = _inproj_call(
            x, shift, scale, w_all, l, _padded_kv_weights(w_in[l]), b_gate[l].reshape(1, 2 * d), gqk,
            cos_t, sin_t, bd)
        a0 = a0.reshape(b, 1, s, _A_SEG)
        oa = [_attn_a(a_g, biases[g]) for g, a_g in enumerate((a0, a1, a2))]
        kmax = jnp.sqrt(jnp.max(st[:, :, 0:B_KV_HEADS, 0], axis=1))
        qmax = jnp.sqrt(jnp.max(st[:, :, B_KV_HEADS, 0]))
        bound_ok = qmax * jnp.max(kmax) * B_BOUND_SLACK <= B_MAX_BOUND
        ybt = lax.cond(
            bound_ok,
            functools.partial(_attn_b_call, online=False),
            functools.partial(_attn_b_call, online=True),
            kmax.reshape(-1), bqt, qn2, bk, bvt)
        x = _outproj_call(x, gate, oa, ybt, gz, w_pa[l].astype(BF16), w_pb[l].astype(BF16),
                          w_o[l].astype(BF16), ln_g[l].reshape(1, d), ln_b[l].reshape(1, d), alpha)
    return x
```

```python
import functools
import math

import jax
import jax.numpy as jnp
from jax import lax
from jax.experimental import pallas as pl
from jax.experimental.pallas import tpu as pltpu

HEAD_DIM = 64
A_GROUPS = ((128, 1), (512, 4), (2048, 16))
A_HEADS_PER_GROUP = 8
A_WIDTH = A_HEADS_PER_GROUP * HEAD_DIM
A_RADIUS = 64
B_Q_HEADS = 8
B_KV_HEADS = 2
B_GROUP = B_Q_HEADS // B_KV_HEADS
B_WIDTH = B_Q_HEADS * HEAD_DIM
GRID_W = 64
ROPE_THETA = 10000.0
REL_BUCKETS = 32
REL_MAX_DISTANCE = 1024
LN_EPS = 1e-5
QK_EPS = 1e-6
NEG_INF = -1e30
LOG2_E = math.log2(math.e)
_Q_SCALE = HEAD_DIM ** -0.5 * LOG2_E

LANES = 128
SUBLANES = 8
VMEM_LIMIT_BYTES = 56 * 1024 * 1024

ADA_TN = 1024
IN_TM = 256
A_TQ = 2048
A_SUB = 128
B_TQ = 1024
B_TK = 2048
OUT_TM = 512
OUT_SPLIT = 1

B_MAX_BOUND = 50.0
A_MAX_ABS_LOG2_DENOM = 90.0
B_BOUND_SLACK = 1.0 + 2.0 ** -6

BF16 = jnp.bfloat16
F32 = jnp.float32

_NT = (((1,), (1,)), ((), ()))


def _ada_kernel(c_ref, w_ref, b_ref, o_ref):
    c = c_ref[...]
    h = c * jax.nn.sigmoid(c)
    o_ref[0] = jnp.dot(h, w_ref[0], preferred_element_type=F32,
                       precision=lax.Precision.HIGHEST) + b_ref[0]


def _ada_call(c_pad, w_ada, b_ada):
    depth, d, n3 = w_ada.shape
    rows = c_pad.shape[0]
    return pl.pallas_call(
        _ada_kernel,
        out_shape=jax.ShapeDtypeStruct((depth, rows, n3), F32),
        grid=(depth, n3 // ADA_TN),
        in_specs=[
            pl.BlockSpec((rows, d), lambda l, j: (0, 0)),
            pl.BlockSpec((1, d, ADA_TN), lambda l, j: (l, 0, j)),
            pl.BlockSpec((1, 1, ADA_TN), lambda l, j: (l, 0, j)),
        ],
        out_specs=pl.BlockSpec((1, rows, ADA_TN), lambda l, j: (l, 0, j)),
        compiler_params=pltpu.CompilerParams(
            dimension_semantics=("arbitrary", "arbitrary")),
        name="ada_mod",
    )(c_pad, w_ada, b_ada.reshape(depth, 1, n3))


_A_SEG = 3 * A_WIDTH
_A_PART = len(A_GROUPS) * A_WIDTH
_OFF_AZ = 3 * _A_PART
_OFF_BQ = _OFF_AZ + A_WIDTH
_KV_PAD = B_KV_HEADS * LANES
_OFF_BZ = _OFF_BQ + B_WIDTH + 2 * B_KV_HEADS * HEAD_DIM
_OFF_GL = _OFF_BZ + B_WIDTH
_QK_W = B_WIDTH + _KV_PAD
_BD_W = 2 * LANES
_ONE_LANE = HEAD_DIM


def _inproj_kernel(x_ref, shift_ref, scale_ref, w_ref, wkv_ref, bgate_ref, gqk_ref,
                   cos_ref, sin_ref, bd_ref,
                   a0_ref, a1_ref, a2_ref, bqt_ref, qn_ref, bk_ref, bvt_ref, gz_ref, st_ref,
                   scr_ref):
    tm = x_ref.shape[1]
    d_model = x_ref.shape[2]
    u = (x_ref[0] * (1.0 + scale_ref[0]) + shift_ref[0]).astype(BF16)

    def proj(off, width):
        return jnp.dot(u, w_ref[:, off:off + width], preferred_element_type=F32)

    kv_lane = lax.broadcasted_iota(jnp.int32, (1, _KV_PAD), 1)
    one_col = ((kv_lane & (LANES - 1)) == _ONE_LANE).astype(F32)

    qk = jnp.concatenate(
        [proj(_OFF_BQ, B_WIDTH),
         jnp.dot(u, wkv_ref[:, 0:_KV_PAD], preferred_element_type=F32)], axis=1)
    sq = qk * qk
    sq_hi = sq.astype(BF16)
    sq_lo = (sq - sq_hi.astype(F32)).astype(BF16)
    def head_sums(part):
        return jnp.concatenate(
            [jnp.dot(part[:, c:c + _BD_W], bd_ref[...], preferred_element_type=F32)
             for c in range(0, _QK_W, _BD_W)], axis=1)
    ss = head_sums(sq_hi) + head_sums(sq_lo)
    y = qk * lax.rsqrt(ss * (1.0 / HEAD_DIM) + QK_EPS) * gqk_ref[...]
    reps = _QK_W // LANES
    cos = jnp.concatenate([cos_ref[...]] * reps, axis=1)
    sin = jnp.concatenate([sin_ref[...]] * reps, axis=1)
    lane = lax.broadcasted_iota(jnp.int32, y.shape, 1)
    first = (lane & 31) < 16
    swapped = jnp.where(first, pltpu.roll(y, _QK_W - 16, 1), pltpu.roll(y, 16, 1))
    rot = y * cos + swapped * sin
    q = rot[:, :B_WIDTH] * _Q_SCALE
    k = rot[:, B_WIDTH:]
    bk_ref[0] = (k + one_col).astype(BF16)
    q_t = q.T
    bqt_ref[0] = q_t.astype(BF16)
    qn2 = jnp.concatenate(
        [jnp.sum(jnp.square(q_t[h * HEAD_DIM:(h + 1) * HEAD_DIM]), axis=0, keepdims=True)
         for h in range(B_Q_HEADS)], axis=0)
    qn_ref[0] = qn2

    def tile_max(v2):
        return jnp.max(jnp.sum(v2, axis=1, keepdims=True), axis=0, keepdims=True)
    k2 = k * k
    kmax = [tile_max(k2[:, h * LANES:(h + 1) * LANES]) for h in range(B_KV_HEADS)]
    qmax = jnp.max(jnp.max(qn2, axis=1, keepdims=True), axis=0, keepdims=True)
    sub = lax.broadcasted_iota(jnp.int32, (SUBLANES, LANES), 0)
    st_ref[0, 0] = jnp.where(sub == 0, kmax[0], jnp.where(sub == 1, kmax[1], qmax))

    def proj_a(g):
        chunks = []
        for part in range(3):
            res = proj(part * _A_PART + g * A_WIDTH, A_WIDTH)
            if part == 0:
                res = res * _Q_SCALE
            chunks += [res[:, ch * LANES:(ch + 1) * LANES] for ch in range(A_WIDTH // LANES)]
        return chunks

    for g, a_ref in ((1, a1_ref), (2, a2_ref)):
        dil = A_GROUPS[g][1]
        for ch, chunk in enumerate(proj_a(g)):
            scr_ref[ch] = chunk
        for r in range(dil):
            for ch in range(_A_SEG // LANES):
                a_ref[0, r, :, ch * LANES:(ch + 1) * LANES] = (
                    scr_ref[ch, pl.ds(r, tm // dil, stride=dil), :].astype(BF16))

    az = proj(_OFF_AZ, A_WIDTH)
    gz_ref[0, :, 0:A_WIDTH] = (az * jax.nn.sigmoid(az)).astype(BF16)
    bz = proj(_OFF_BZ, B_WIDTH)
    gz_ref[0, :, A_WIDTH:A_WIDTH + B_WIDTH] = (bz * jax.nn.sigmoid(bz)).astype(BF16)
    for j in range(2):
        gl = proj(_OFF_GL + j * d_model, d_model) + bgate_ref[:, j * d_model:(j + 1) * d_model]
        o = A_WIDTH + B_WIDTH + j * d_model
        gz_ref[0, :, o:o + d_model] = jax.nn.sigmoid(gl).astype(BF16)

    a0_ref[0] = jnp.concatenate(proj_a(0), axis=1).astype(BF16)
    v_pad = jnp.dot(u, wkv_ref[:, _KV_PAD:2 * _KV_PAD], preferred_element_type=F32)
    bvt_ref[0] = (v_pad + one_col).T.astype(BF16)


def _inproj_call(x, shift, scale, w_all, layer, w_kv, b_gate, gqk, cos_t, sin_t, bd):
    b, s, d = x.shape
    tm = IN_TM
    n_cols = w_all.shape[2]
    d1, d2 = A_GROUPS[1][1], A_GROUPS[2][1]
    const = dict(pipeline_mode=pl.Buffered(1))
    out_shape = (
        jax.ShapeDtypeStruct((b, s, _A_SEG), BF16),
        jax.ShapeDtypeStruct((b, d1, s // d1, _A_SEG), BF16),
        jax.ShapeDtypeStruct((b, d2, s // d2, _A_SEG), BF16),
        jax.ShapeDtypeStruct((b, B_WIDTH, s), BF16),
        jax.ShapeDtypeStruct((b, B_Q_HEADS, s), F32),
        jax.ShapeDtypeStruct((b, s, _KV_PAD), BF16),
        jax.ShapeDtypeStruct((b, _KV_PAD, s), BF16),
        jax.ShapeDtypeStruct((b, s, A_WIDTH + B_WIDTH + 2 * d), BF16),
        jax.ShapeDtypeStruct((b, s // tm, SUBLANES, LANES), F32),
    )
    row = lambda bi, i: (bi, i, 0)
    out_specs = (
        pl.BlockSpec((1, tm, _A_SEG), row),
        pl.BlockSpec((1, d1, tm // d1, _A_SEG), lambda bi, i: (bi, 0, i, 0)),
        pl.BlockSpec((1, d2, tm // d2, _A_SEG), lambda bi, i: (bi, 0, i, 0)),
        pl.BlockSpec((1, B_WIDTH, tm), lambda bi, i: (bi, 0, i)),
        pl.BlockSpec((1, B_Q_HEADS, tm), lambda bi, i: (bi, 0, i)),
        pl.BlockSpec((1, tm, _KV_PAD), row),
        pl.BlockSpec((1, _KV_PAD, tm), lambda bi, i: (bi, 0, i)),
        pl.BlockSpec((1, tm, A_WIDTH + B_WIDTH + 2 * d), row),
        pl.BlockSpec((1, 1, SUBLANES, LANES), lambda bi, i: (bi, i, 0, 0)),
    )
    in_specs = [
        pl.BlockSpec((1, tm, d), row),
        pl.BlockSpec((1, 1, d), lambda bi, i: (bi, 0, 0)),
        pl.BlockSpec((1, 1, d), lambda bi, i: (bi, 0, 0)),
        pl.BlockSpec((None, d, n_cols), lambda bi, i: (layer, 0, 0), **const),
        pl.BlockSpec((d, 2 * _KV_PAD), lambda bi, i: (0, 0), **const),
        pl.BlockSpec((1, 2 * d), lambda bi, i: (0, 0), **const),
        pl.BlockSpec((1, _QK_W), lambda bi, i: (0, 0), **const),
        pl.BlockSpec((tm, LANES), lambda bi, i: (i, 0)),
        pl.BlockSpec((tm, LANES), lambda bi, i: (i, 0)),
        pl.BlockSpec((_BD_W, _BD_W), lambda bi, i: (0, 0), **const),
    ]
    return pl.pallas_call(
        _inproj_kernel,
        out_shape=out_shape,
        grid=(b, s // tm),
        in_specs=in_specs,
        out_specs=out_specs,
        scratch_shapes=[pltpu.VMEM((_A_SEG // LANES, tm, LANES), F32)],
        compiler_params=pltpu.CompilerParams(
            dimension_semantics=("arbitrary", "arbitrary"),
            vmem_limit_bytes=VMEM_LIMIT_BYTES),
        name="inproj",
    )(x, shift, scale, w_all, w_kv, b_gate, gqk, cos_t, sin_t, bd)


def _attn_a_kernel(q_ref, kp_ref, kc_ref, kn_ref, vp_ref, vc_ref, vn_ref, bias_ref,
                   o_ref, lse_ref, range_ref, *, phase_len, stabilise):
    n_ph, tq = q_ref.shape[1], q_ref.shape[2]
    i = pl.program_id(2)
    tk = A_SUB + 2 * A_RADIUS
    n_sub = tq // A_SUB
    lane = lax.broadcasted_iota(jnp.int32, (A_SUB, LANES), 1)
    low = lane < HEAD_DIM
    kcol = lax.broadcasted_iota(jnp.int32, (1, tk), 1)
    worst = jnp.zeros((A_SUB, LANES), F32)
    for ph, sub in [(ph, sub) for ph in range(n_ph) for sub in range(n_sub)]:
        if sub == 0:
            q = q_ref[0, ph]
            k_all = jnp.concatenate([kp_ref[0, ph], kc_ref[0, ph], kn_ref[0, ph]], axis=0)
            v_all = jnp.concatenate([vp_ref[0, ph], vc_ref[0, ph], vn_ref[0, ph]], axis=0)
        r0 = sub * A_SUB
        at_edge = sub == 0 or sub == n_sub - 1
        if at_edge:
            krow = i * tq + (r0 - A_RADIUS) + kcol
            valid = (krow >= 0) & (krow < phase_len)
        for pair in range(A_HEADS_PER_GROUP // 2):
            c0 = pair * LANES
            qp = q[r0:r0 + A_SUB, c0:c0 + LANES]
            kp = k_all[r0:r0 + tk, c0:c0 + LANES]
            vp = v_all[r0:r0 + tk, c0:c0 + LANES]
            zero = jnp.zeros_like(qp)
            qs = jnp.concatenate([jnp.where(low, qp, zero), jnp.where(low, zero, qp)], axis=0)
            s = lax.dot_general(qs, kp, _NT, preferred_element_type=F32) + bias_ref[pair]
            if at_edge:
                s = jnp.where(valid, s, NEG_INF)
            if stabilise:
                m = jnp.max(s, axis=1, keepdims=True)
                s = s - m
            p = jnp.exp2(s)
            l = jnp.sum(p, axis=1, keepdims=True)
            o = jnp.dot(p.astype(BF16), vp, preferred_element_type=F32)
            pick = lambda a: jnp.where(low, a[:A_SUB], a[A_SUB:])
            l = pick(l)
            lse = jnp.log2(l)
            worst = jnp.maximum(worst, jnp.abs(lse))
            if stabilise:
                lse = lse + pick(m)
            o_ref[0, ph, r0:r0 + A_SUB, c0:c0 + LANES] = (pick(o) * (1.0 / l)).astype(BF16)
            lse_ref[0, ph, r0:r0 + A_SUB, c0:c0 + LANES] = lse
    range_ref[0, 0, 0] = jnp.broadcast_to(
        jnp.max(jnp.max(worst, axis=1, keepdims=True), axis=0, keepdims=True), (SUBLANES, LANES))


def _attn_a_call(a_g, bias_g, *, stabilise):
    b, dil, phase_len, _ = a_g.shape
    tq = min(A_TQ, phase_len)
    n_steps = phase_len // tq
    n_ph = min(dil, A_TQ // tq)
    halo = A_RADIUS
    nh = phase_len // halo
    per = tq // halo
    cur = lambda col: (lambda bi, r, i: (bi, r, i, col))
    prev = lambda col: (lambda bi, r, i: (bi, r, jnp.maximum(i * per - 1, 0), col))
    nxt = lambda col: (lambda bi, r, i: (bi, r, jnp.minimum((i + 1) * per, nh - 1), col))
    blk = (1, n_ph, tq, A_WIDTH)
    hblk = (1, n_ph, halo, A_WIDTH)
    o_sds = jax.ShapeDtypeStruct((b, dil, phase_len, A_WIDTH), BF16)
    lse_sds = jax.ShapeDtypeStruct((b, dil, phase_len, A_WIDTH), F32)
    range_sds = jax.ShapeDtypeStruct((b, dil // n_ph, n_steps, SUBLANES, LANES), F32)
    return pl.pallas_call(
        functools.partial(_attn_a_kernel, phase_len=phase_len, stabilise=stabilise),
        out_shape=(o_sds, lse_sds, range_sds),
        grid=(b, dil // n_ph, n_steps),
        in_specs=[
            pl.BlockSpec(blk, cur(0)),
            pl.BlockSpec(hblk, prev(1)), pl.BlockSpec(blk, cur(1)), pl.BlockSpec(hblk, nxt(1)),
            pl.BlockSpec(hblk, prev(2)), pl.BlockSpec(blk, cur(2)), pl.BlockSpec(hblk, nxt(2)),
            pl.BlockSpec(bias_g.shape, lambda bi, r, i: (0, 0, 0)),
        ],
        out_specs=(pl.BlockSpec(blk, cur(0)), pl.BlockSpec(blk, cur(0)),
                   pl.BlockSpec((1, 1, 1, SUBLANES, LANES), lambda bi, r, i: (bi, r, i, 0, 0))),
        compiler_params=pltpu.CompilerParams(
            dimension_semantics=("arbitrary", "arbitrary", "arbitrary"),
            vmem_limit_bytes=VMEM_LIMIT_BYTES),
        name=f"attn_a_d{dil}" + ("_stab" if stabilise else ""),
    )(a_g, a_g, a_g, a_g, a_g, a_g, a_g, bias_g)


def _attn_a(a_g, bias_g):
    o, lse, worst = _attn_a_call(a_g, bias_g, stabilise=False)
    in_range = jnp.max(worst) <= A_MAX_ABS_LOG2_DENOM
    return lax.cond(in_range,
                    lambda: (o, lse),
                    lambda: _attn_a_call(a_g, bias_g, stabilise=True)[:2])


_VT_ROWS = 80


def _attn_b_kernel(kmax_ref, qt_ref, qn_ref, k_ref, vt_ref, o_ref, qst_ref, acc_ref, *m_scratch,
                   online):
    tq = qt_ref.shape[2]
    seq = k_ref.shape[1]
    kmax = kmax_ref[pl.program_id(0) * B_KV_HEADS + pl.program_id(1)]
    tail_row = lax.broadcasted_iota(jnp.int32, (LANES - HEAD_DIM, tq), 0)
    for h in range(B_GROUP):
        cols = slice(h * tq, (h + 1) * tq)
        qst_ref[0:HEAD_DIM, cols] = qt_ref[0, h * HEAD_DIM:(h + 1) * HEAD_DIM, :]
        if online:
            tail = jnp.zeros(tail_row.shape, F32)
        else:
            bound = jnp.sqrt(qn_ref[0, 0, h:h + 1, :]) * (kmax * B_BOUND_SLACK)
            tail = jnp.where(tail_row == _ONE_LANE - HEAD_DIM, -bound, 0.0)
        qst_ref[HEAD_DIM:LANES, cols] = tail.astype(BF16)
    acc_ref[...] = jnp.zeros(acc_ref.shape, F32)
    if online:
        m_ref, = m_scratch
        m_ref[...] = jnp.full(m_ref.shape, NEG_INF, F32)

    n_chunks = seq // B_TK

    def scores_t(c):
        start = pl.multiple_of(c * B_TK, B_TK)
        return jnp.dot(k_ref[0, pl.ds(start, B_TK), :], qst_ref[...],
                       preferred_element_type=F32)

    def values_t(c):
        start = pl.multiple_of(c * B_TK, B_TK)
        return vt_ref[0, 0:_VT_ROWS, pl.ds(start, B_TK)]

    if online:
        def body(c, carry):
            s_t = scores_t(c)
            m_prev = m_ref[...]
            m_new = jnp.maximum(m_prev, jnp.max(s_t, axis=0, keepdims=True))
            p_t = jnp.exp2(s_t - m_new).astype(BF16)
            acc_ref[...] = (jnp.exp2(m_prev - m_new) * acc_ref[...]
                            + jnp.dot(values_t(c), p_t, preferred_element_type=F32))
            m_ref[...] = m_new
            return carry

        lax.fori_loop(0, n_chunks, body, 0)
    else:
        def body(c, carry):
            start = pl.multiple_of(c * B_TK, B_TK)
            kc = k_ref[0, pl.ds(start, B_TK), :]
            vtc = values_t(c)
            for h in range(B_GROUP):
                cols = slice(h * tq, (h + 1) * tq)
                p_t = jnp.exp2(jnp.dot(kc, qst_ref[:, cols],
                                       preferred_element_type=F32)).astype(BF16)
                acc_ref[:, cols] += jnp.dot(vtc, p_t, preferred_element_type=F32)
            return carry

        lax.fori_loop(0, n_chunks, body, 0)
    acc = acc_ref[...]
    o_t = acc[0:HEAD_DIM, :] * (1.0 / acc[_ONE_LANE:_ONE_LANE + 1, :])
    for h in range(B_GROUP):
        o_ref[0, h * HEAD_DIM:(h + 1) * HEAD_DIM, :] = o_t[:, h * tq:(h + 1) * tq].astype(BF16)


def _attn_b_call(kmax, bqt, qn2, bk, bvt, *, online):
    b, _, s = bqt.shape
    qw = B_GROUP * HEAD_DIM
    cols = B_GROUP * B_TQ
    scratch = [pltpu.VMEM((LANES, cols), BF16), pltpu.VMEM((_VT_ROWS, cols), F32)]
    if online:
        scratch.append(pltpu.VMEM((1, cols), F32))
    return pl.pallas_call(
        functools.partial(_attn_b_kernel, online=online),
        out_shape=jax.ShapeDtypeStruct((b, B_WIDTH, s), BF16),
        grid_spec=pltpu.PrefetchScalarGridSpec(
            num_scalar_prefetch=1,
            grid=(b, B_KV_HEADS, s // B_TQ),
            in_specs=[
                pl.BlockSpec((1, qw, B_TQ), lambda bi, h, i, km: (bi, h, i)),
                pl.BlockSpec((1, 1, B_GROUP, B_TQ), lambda bi, h, i, km: (bi, h, 0, i)),
                pl.BlockSpec((1, s, LANES), lambda bi, h, i, km: (bi, 0, h)),
                pl.BlockSpec((1, LANES, s), lambda bi, h, i, km: (bi, h, 0)),
            ],
            out_specs=pl.BlockSpec((1, qw, B_TQ), lambda bi, h, i, km: (bi, h, i)),
            scratch_shapes=scratch,
        ),
        compiler_params=pltpu.CompilerParams(
            dimension_semantics=("arbitrary", "arbitrary", "arbitrary"),
            vmem_limit_bytes=VMEM_LIMIT_BYTES),
        name="attn_b_online" if online else "attn_b",
    )(kmax, bqt, qn2.reshape(b, B_KV_HEADS, B_GROUP, s), bk, bvt)


def _outproj_kernel(x_ref, gate_ref, o0_ref, l0_ref, o1_ref, l1_ref, o2_ref, l2_ref,
                    ybt_ref, gz_ref, wpa_ref, wpb_ref, wo_ref, lng_ref, lnb_ref,
                    out_ref, so1, sl1, so2, sl2, *, alpha):
    tm = x_ref.shape[1]
    d_model = x_ref.shape[2]
    for (o_ref, l_ref, so, sl, g) in ((o1_ref, l1_ref, so1, sl1, 1), (o2_ref, l2_ref, so2, sl2, 2)):
        dil = A_GROUPS[g][1]
        for r in range(dil):
            for ch in range(A_WIDTH // LANES):
                cs = slice(ch * LANES, (ch + 1) * LANES)
                so[ch, pl.ds(r, tm // dil, stride=dil), :] = o_ref[0, r, :, cs].astype(F32)
                sl[ch, pl.ds(r, tm // dil, stride=dil), :] = l_ref[0, r, :, cs]
    rows = tm // OUT_SPLIT
    for part in range(OUT_SPLIT):
        rs = slice(part * rows, (part + 1) * rows)
        cat = lambda ref: jnp.concatenate([ref[ch, rs, :] for ch in range(A_WIDTH // LANES)], axis=1)
        l0, l1, l2 = l0_ref[0, rs, :], cat(sl1), cat(sl2)
        mx = jnp.maximum(jnp.maximum(l0, l1), l2)
        e0, e1, e2 = jnp.exp2(l0 - mx), jnp.exp2(l1 - mx), jnp.exp2(l2 - mx)
        y_a = ((e0 * o0_ref[0, rs, :].astype(F32) + e1 * cat(so1) + e2 * cat(so2))
               * (1.0 / (e0 + e1 + e2)))
        gz = gz_ref[0, rs, :]
        ya = (y_a * gz[:, 0:A_WIDTH].astype(F32)).astype(BF16)
        y_b = ybt_ref[0, :, rs].astype(F32).T
        yb = (y_b * gz[:, A_WIDTH:A_WIDTH + B_WIDTH].astype(F32)).astype(BF16)
        pa = jnp.dot(ya, wpa_ref[...], preferred_element_type=F32)
        pb = jnp.dot(yb, wpb_ref[...], preferred_element_type=F32)
        o = A_WIDTH + B_WIDTH
        g_a = gz[:, o:o + d_model].astype(F32)
        g_b = gz[:, o + d_model:o + 2 * d_model].astype(F32)
        merged = (g_a * pa + g_b * pb).astype(BF16)
        out = jnp.dot(merged, wo_ref[...], preferred_element_type=F32)
        h = alpha * x_ref[0, rs, :] + gate_ref[0] * out
        mu = jnp.mean(h, axis=-1, keepdims=True)
        hc = h - mu
        var = jnp.mean(hc * hc, axis=-1, keepdims=True)
        out_ref[0, rs, :] = hc * lax.rsqrt(var + LN_EPS) * lng_ref[...] + lnb_ref[...]


def _outproj_call(x, gate, oa, ybt, gz, w_pa, w_pb, w_o, ln_g, ln_b, alpha):
    b, s, d = x.shape
    tm = OUT_TM
    (o0, l0), (o1, l1), (o2, l2) = oa
    d1, d2 = A_GROUPS[1][1], A_GROUPS[2][1]
    row = lambda bi, i: (bi, i, 0)
    ph = lambda bi, i: (bi, 0, i, 0)
    fixed = lambda bi, i: (0, 0)
    s0 = pl.BlockSpec((1, 1, tm, A_WIDTH), ph)
    s1 = pl.BlockSpec((1, d1, tm // d1, A_WIDTH), ph)
    s2 = pl.BlockSpec((1, d2, tm // d2, A_WIDTH), ph)

    def kern(x_ref, gate_ref, o0_ref, l0_ref, *rest):
        return _outproj_kernel(x_ref, gate_ref, o0_ref.at[0], l0_ref.at[0], *rest, alpha=alpha)

    return pl.pallas_call(
        kern,
        out_shape=jax.ShapeDtypeStruct((b, s, d), F32),
        grid=(b, s // tm),
        in_specs=[
            pl.BlockSpec((1, tm, d), row),
            pl.BlockSpec((1, 1, d), lambda bi, i: (bi, 0, 0)),
            s0, s0, s1, s1, s2, s2,
            pl.BlockSpec((1, B_WIDTH, tm), lambda bi, i: (bi, 0, i)),
            pl.BlockSpec((1, tm, gz.shape[2]), row),
            pl.BlockSpec(w_pa.shape, fixed),
            pl.BlockSpec(w_pb.shape, fixed),
            pl.BlockSpec(w_o.shape, fixed),
            pl.BlockSpec((1, d), fixed),
            pl.BlockSpec((1, d), fixed),
        ],
        out_specs=pl.BlockSpec((1, tm, d), row),
        scratch_shapes=[pltpu.VMEM((A_WIDTH // LANES, tm, LANES), F32)] * 4,
        compiler_params=pltpu.CompilerParams(
            dimension_semantics=("arbitrary", "arbitrary"),
            vmem_limit_bytes=VMEM_LIMIT_BYTES),
        name="outproj",
    )(x, gate, o0, l0, o1, l1, o2, l2, ybt, gz, w_pa, w_pb, w_o, ln_g, ln_b)


def _t5_bucket(rel):
    half = REL_BUCKETS // 2
    max_exact = half // 2
    ret = jnp.where(rel > 0, half, 0)
    a = jnp.abs(rel)
    af = jnp.maximum(a, 1).astype(F32)
    large = max_exact + (jnp.log(af / max_exact) / math.log(REL_MAX_DISTANCE / max_exact)
                         * (half - max_exact)).astype(jnp.int32)
    large = jnp.minimum(large, half - 1)
    return ret + jnp.where(a < max_exact, a, large)


def _window_bias(rel_table, g):
    dil = A_GROUPS[g][1]
    tk = A_SUB + 2 * A_RADIUS
    rel = jnp.arange(tk)[None, :] - A_RADIUS - jnp.arange(A_SUB)[:, None]
    table_g = rel_table[:, g * A_HEADS_PER_GROUP:(g + 1) * A_HEADS_PER_GROUP]
    onehot = (_t5_bucket(rel * dil)[..., None] == jnp.arange(REL_BUCKETS)).astype(F32)
    bias = jnp.einsum("qkb,bh->hqk", onehot, table_g.astype(F32), precision=lax.Precision.HIGHEST)
    bias = jnp.where((jnp.abs(rel) <= A_RADIUS)[None], bias * LOG2_E, NEG_INF)
    return bias.reshape(A_HEADS_PER_GROUP // 2, 2 * A_SUB, tk)


def _rope_tables(seq):
    t = jnp.arange(seq)
    row = (t // GRID_W).astype(F32)
    col = (t % GRID_W).astype(F32)
    half = HEAD_DIM // 2
    inv = ROPE_THETA ** (-jnp.arange(0, half, 2, dtype=F32) / half)
    ar, ac = row[:, None] * inv[None], col[:, None] * inv[None]
    cos = jnp.concatenate([jnp.cos(ar), jnp.cos(ar), jnp.cos(ac), jnp.cos(ac)], axis=1)
    sin = jnp.concatenate([-jnp.sin(ar), jnp.sin(ar), -jnp.sin(ac), jnp.sin(ac)], axis=1)
    return jnp.tile(cos, (1, LANES // HEAD_DIM)), jnp.tile(sin, (1, LANES // HEAD_DIM))


def _padded_kv_weights(w):
    d = w.shape[0]
    off = _OFF_BQ + B_WIDTH
    zeros = jnp.zeros((d, LANES - HEAD_DIM), w.dtype)
    blocks = []
    for h in range(2 * B_KV_HEADS):
        blocks += [w[:, off + h * HEAD_DIM:off + (h + 1) * HEAD_DIM], zeros]
    return jnp.concatenate(blocks, axis=1).astype(BF16)


def _pad_heads(g):
    return jnp.concatenate([g, jnp.zeros((LANES - HEAD_DIM,), g.dtype)])


def kernel(x, c, rel_table, ln_g, ln_b, w_ada, b_ada, w_in, b_gate, q_norm_g, k_norm_g, w_pa, w_pb, w_o):
    depth = w_in.shape[0]
    b, s, d = x.shape
    alpha = float((2 * depth) ** 0.25)

    c_pad = jnp.zeros((SUBLANES, d), F32).at[:b].set(c)
    mod = _ada_call(c_pad, w_ada, b_ada)[:, :b]
    cos_t, sin_t = _rope_tables(s)
    idx = jnp.arange(_BD_W) // HEAD_DIM
    bd = (idx[:, None] == idx[None, :]).astype(BF16)
    biases = [_window_bias(rel_table, g) for g in range(len(A_GROUPS))]
    w_all = w_in.astype(BF16)

    for l in range(depth):
        shift = mod[l, :, 0:d].reshape(b, 1, d)
        scale = mod[l, :, d:2 * d].reshape(b, 1, d)
        gate = mod[l, :, 2 * d:3 * d].reshape(b, 1, d)
        gqk = jnp.concatenate([jnp.tile(q_norm_g[l], B_Q_HEADS),
                               jnp.tile(_pad_heads(k_norm_g[l]), B_KV_HEADS)]).reshape(1, _QK_W)
        a0, a1, a2, bqt, qn2, bk, bvt, gz, st = _inproj_call(
            x, shift, scale, w_all, l, _padded_kv_weights(w_in[l]), b_gate[l].reshape(1, 2 * d), gqk,
            cos_t, sin_t, bd)
        a0 = a0.reshape(b, 1, s, _A_SEG)
        oa = [_attn_a(a_g, biases[g]) for g, a_g in enumerate((a0, a1, a2))]
        kmax = jnp.sqrt(jnp.max(st[:, :, 0:B_KV_HEADS, 0], axis=1))
        qmax = jnp.sqrt(jnp.max(st[:, :, B_KV_HEADS, 0]))
        bound_ok = qmax * jnp.max(kmax) * B_BOUND_SLACK <= B_MAX_BOUND
        ybt = lax.cond(
            bound_ok,
            functools.partial(_attn_b_call, online=False),
            functools.partial(_attn_b_call, online=True),
            kmax.reshape(-1), bqt, qn2, bk, bvt)
        x = _outproj_call(x, gate, oa, ybt, gz, w_pa[l].astype(BF16), w_pb[l].astype(BF16),
                          w_o[l].astype(BF16), ln_g[l].reshape(1, d), ln_b[l].reshape(1, d), alpha)
    return x
```

```python
import functools
import math

import jax
import jax.numpy as jnp
from jax import lax
from jax.experimental import pallas as pl
from jax.experimental.pallas import tpu as pltpu

HEAD_DIM = 64
A_GROUPS = ((128, 1), (512, 4), (2048, 16))
A_HEADS_PER_GROUP = 8
A_WIDTH = A_HEADS_PER_GROUP * HEAD_DIM
A_RADIUS = 64
B_Q_HEADS = 8
B_KV_HEADS = 2
B_GROUP = B_Q_HEADS // B_KV_HEADS
B_WIDTH = B_Q_HEADS * HEAD_DIM
GRID_W = 64
ROPE_THETA = 10000.0
REL_BUCKETS = 32
REL_MAX_DISTANCE = 1024
LN_EPS = 1e-5
QK_EPS = 1e-6
NEG_INF = -1e30
LOG2_E = math.log2(math.e)
_Q_SCALE = HEAD_DIM ** -0.5 * LOG2_E

LANES = 128
SUBLANES = 8
VMEM_LIMIT_BYTES = 56 * 1024 * 1024

ADA_TN = 1024
IN_TM = 256
A_TQ = 2048
A_SUB = 128
B_TQ = 1024
B_TK = 2048
B_CHAIN = 512
OUT_TM = 512

B_MAX_BOUND = 50.0
A_MAX_ABS_LOG2_DENOM = 90.0
B_BOUND_SLACK = 1.0 + 2.0 ** -6

BF16 = jnp.bfloat16
F32 = jnp.float32

_NT = (((1,), (1,)), ((), ()))


def _ada_kernel(c_ref, w_ref, b_ref, o_ref):
    c = c_ref[...]
    h = c * jax.nn.sigmoid(c)
    o_ref[0] = jnp.dot(h, w_ref[0], preferred_element_type=F32,
                       precision=lax.Precision.HIGHEST) + b_ref[0]


def _ada_call(c_pad, w_ada, b_ada):
    depth, d, n3 = w_ada.shape
    rows = c_pad.shape[0]
    return pl.pallas_call(
        _ada_kernel,
        out_shape=jax.ShapeDtypeStruct((depth, rows, n3), F32),
        grid=(depth, n3 // ADA_TN),
        in_specs=[
            pl.BlockSpec((rows, d), lambda l, j: (0, 0)),
            pl.BlockSpec((1, d, ADA_TN), lambda l, j: (l, 0, j)),
            pl.BlockSpec((1, 1, ADA_TN), lambda l, j: (l, 0, j)),
        ],
        out_specs=pl.BlockSpec((1, rows, ADA_TN), lambda l, j: (l, 0, j)),
        compiler_params=pltpu.CompilerParams(
            dimension_semantics=("arbitrary", "arbitrary")),
        name="ada_mod",
    )(c_pad, w_ada, b_ada.reshape(depth, 1, n3))


_A_SEG = 3 * A_WIDTH
_A_PART = len(A_GROUPS) * A_WIDTH
_OFF_AZ = 3 * _A_PART
_OFF_BQ = _OFF_AZ + A_WIDTH
_KV_PAD = B_KV_HEADS * LANES
_OFF_BZ = _OFF_BQ + B_WIDTH + 2 * B_KV_HEADS * HEAD_DIM
_OFF_GL = _OFF_BZ + B_WIDTH
_QK_W = B_WIDTH + _KV_PAD
_BD_W = 2 * LANES
_ONE_LANE = HEAD_DIM


def _inproj_kernel(x_ref, shift_ref, scale_ref, w_ref, wkv_ref, bgate_ref, gqk_ref,
                   cos_ref, sin_ref, bd_ref,
                   a0_ref, a1_ref, a2_ref, bqt_ref, qn_ref, bk_ref, bvt_ref, gz_ref, st_ref,
                   scr_ref):
    tm = x_ref.shape[1]
    d_model = x_ref.shape[2]
    u = (x_ref[0] * (1.0 + scale_ref[0]) + shift_ref[0]).astype(BF16)

    def proj(off, width):
        return jnp.dot(u, w_ref[:, off:off + width], preferred_element_type=F32)

    kv_lane = lax.broadcasted_iota(jnp.int32, (1, _KV_PAD), 1)
    one_col = ((kv_lane & (LANES - 1)) == _ONE_LANE).astype(F32)

    qk = jnp.concatenate(
        [proj(_OFF_BQ, B_WIDTH),
         jnp.dot(u, wkv_ref[:, 0:_KV_PAD], preferred_element_type=F32)], axis=1)
    sq = qk * qk
    sq_hi = sq.astype(BF16)
    sq_lo = (sq - sq_hi.astype(F32)).astype(BF16)
    def head_sums(part):
        return jnp.concatenate(
            [jnp.dot(part[:, c:c + _BD_W], bd_ref[...], preferred_element_type=F32)
             for c in range(0, _QK_W, _BD_W)], axis=1)
    ss = head_sums(sq_hi) + head_sums(sq_lo)
    y = qk * lax.rsqrt(ss * (1.0 / HEAD_DIM) + QK_EPS) * gqk_ref[...]
    reps = _QK_W // LANES
    cos = jnp.concatenate([cos_ref[...]] * reps, axis=1)
    sin = jnp.concatenate([sin_ref[...]] * reps, axis=1)
    lane = lax.broadcasted_iota(jnp.int32, y.shape, 1)
    first = (lane & 31) < 16
    swapped = jnp.where(first, pltpu.roll(y, _QK_W - 16, 1), pltpu.roll(y, 16, 1))
    rot = y * cos + swapped * sin
    q = rot[:, :B_WIDTH] * _Q_SCALE
    k = rot[:, B_WIDTH:]
    bk_ref[0] = (k + one_col).astype(BF16)
    q_t = q.T
    bqt_ref[0] = q_t.astype(BF16)
    qn2 = jnp.concatenate(
        [jnp.sum(jnp.square(q_t[h * HEAD_DIM:(h + 1) * HEAD_DIM]), axis=0, keepdims=True)
         for h in range(B_Q_HEADS)], axis=0)
    qn_ref[0] = qn2

    def tile_max(v2):
        return jnp.max(jnp.sum(v2, axis=1, keepdims=True), axis=0, keepdims=True)
    k2 = k * k
    kmax = [tile_max(k2[:, h * LANES:(h + 1) * LANES]) for h in range(B_KV_HEADS)]
    qmax = jnp.max(jnp.max(qn2, axis=1, keepdims=True), axis=0, keepdims=True)
    sub = lax.broadcasted_iota(jnp.int32, (SUBLANES, LANES), 0)
    st_ref[0, 0] = jnp.where(sub == 0, kmax[0], jnp.where(sub == 1, kmax[1], qmax))

    def proj_a(g):
        chunks = []
        for part in range(3):
            res = proj(part * _A_PART + g * A_WIDTH, A_WIDTH)
            if part == 0:
                res = res * _Q_SCALE
            chunks += [res[:, ch * LANES:(ch + 1) * LANES] for ch in range(A_WIDTH // LANES)]
        return chunks

    for g, a_ref in ((1, a1_ref), (2, a2_ref)):
        dil = A_GROUPS[g][1]
        for ch, chunk in enumerate(proj_a(g)):
            scr_ref[ch] = chunk
        for r in range(dil):
            for ch in range(_A_SEG // LANES):
                a_ref[0, r, :, ch * LANES:(ch + 1) * LANES] = (
                    scr_ref[ch, pl.ds(r, tm // dil, stride=dil), :].astype(BF16))

    az = proj(_OFF_AZ, A_WIDTH)
    gz_ref[0, :, 0:A_WIDTH] = (az * jax.nn.sigmoid(az)).astype(BF16)
    bz = proj(_OFF_BZ, B_WIDTH)
    gz_ref[0, :, A_WIDTH:A_WIDTH + B_WIDTH] = (bz * jax.nn.sigmoid(bz)).astype(BF16)
    for j in range(2):
        gl = proj(_OFF_GL + j * d_model, d_model) + bgate_ref[:, j * d_model:(j + 1) * d_model]
        o = A_WIDTH + B_WIDTH + j * d_model
        gz_ref[0, :, o:o + d_model] = jax.nn.sigmoid(gl).astype(BF16)

    a0_ref[0] = jnp.concatenate(proj_a(0), axis=1).astype(BF16)
    v_pad = jnp.dot(u, wkv_ref[:, _KV_PAD:2 * _KV_PAD], preferred_element_type=F32)
    bvt_ref[0] = (v_pad + one_col).T.astype(BF16)


def _inproj_call(x, shift, scale, w_all, layer, w_kv, b_gate, gqk, cos_t, sin_t, bd):
    b, s, d = x.shape
    tm = IN_TM
    n_cols = w_all.shape[2]
    d1, d2 = A_GROUPS[1][1], A_GROUPS[2][1]
    const = dict(pipeline_mode=pl.Buffered(1))
    out_shape = (
        jax.ShapeDtypeStruct((b, s, _A_SEG), BF16),
        jax.ShapeDtypeStruct((b, d1, s // d1, _A_SEG), BF16),
        jax.ShapeDtypeStruct((b, d2, s // d2, _A_SEG), BF16),
        jax.ShapeDtypeStruct((b, B_WIDTH, s), BF16),
        jax.ShapeDtypeStruct((b, B_Q_HEADS, s), F32),
        jax.ShapeDtypeStruct((b, s, _KV_PAD), BF16),
        jax.ShapeDtypeStruct((b, _KV_PAD, s), BF16),
        jax.ShapeDtypeStruct((b, s, A_WIDTH + B_WIDTH + 2 * d), BF16),
        jax.ShapeDtypeStruct((b, s // tm, SUBLANES, LANES), F32),
    )
    row = lambda bi, i: (bi, i, 0)
    out_specs = (
        pl.BlockSpec((1, tm, _A_SEG), row),
        pl.BlockSpec((1, d1, tm // d1, _A_SEG), lambda bi, i: (bi, 0, i, 0)),
        pl.BlockSpec((1, d2, tm // d2, _A_SEG), lambda bi, i: (bi, 0, i, 0)),
        pl.BlockSpec((1, B_WIDTH, tm), lambda bi, i: (bi, 0, i)),
        pl.BlockSpec((1, B_Q_HEADS, tm), lambda bi, i: (bi, 0, i)),
        pl.BlockSpec((1, tm, _KV_PAD), row),
        pl.BlockSpec((1, _KV_PAD, tm), lambda bi, i: (bi, 0, i)),
        pl.BlockSpec((1, tm, A_WIDTH + B_WIDTH + 2 * d), row),
        pl.BlockSpec((1, 1, SUBLANES, LANES), lambda bi, i: (bi, i, 0, 0)),
    )
    in_specs = [
        pl.BlockSpec((1, tm, d), row),
        pl.BlockSpec((1, 1, d), lambda bi, i: (bi, 0, 0)),
        pl.BlockSpec((1, 1, d), lambda bi, i: (bi, 0, 0)),
        pl.BlockSpec((None, d, n_cols), lambda bi, i: (layer, 0, 0), **const),
        pl.BlockSpec((d, 2 * _KV_PAD), lambda bi, i: (0, 0), **const),
        pl.BlockSpec((1, 2 * d), lambda bi, i: (0, 0), **const),
        pl.BlockSpec((1, _QK_W), lambda bi, i: (0, 0), **const),
        pl.BlockSpec((tm, LANES), lambda bi, i: (i, 0)),
        pl.BlockSpec((tm, LANES), lambda bi, i: (i, 0)),
        pl.BlockSpec((_BD_W, _BD_W), lambda bi, i: (0, 0), **const),
    ]
    return pl.pallas_call(
        _inproj_kernel,
        out_shape=out_shape,
        grid=(b, s // tm),
        in_specs=in_specs,
        out_specs=out_specs,
        scratch_shapes=[pltpu.VMEM((_A_SEG // LANES, tm, LANES), F32)],
        compiler_params=pltpu.CompilerParams(
            dimension_semantics=("arbitrary", "arbitrary"),
            vmem_limit_bytes=VMEM_LIMIT_BYTES),
        name="inproj",
    )(x, shift, scale, w_all, w_kv, b_gate, gqk, cos_t, sin_t, bd)


def _attn_a_kernel(q_ref, kp_ref, kc_ref, kn_ref, vp_ref, vc_ref, vn_ref, bias_ref,
                   o_ref, lse_ref, range_ref, *, phase_len, stabilise):
    n_ph, tq = q_ref.shape[1], q_ref.shape[2]
    i = pl.program_id(2)
    tk = A_SUB + 2 * A_RADIUS
    n_sub = tq // A_SUB
    lane = lax.broadcasted_iota(jnp.int32, (A_SUB, LANES), 1)
    low = lane < HEAD_DIM
    kcol = lax.broadcasted_iota(jnp.int32, (1, tk), 1)
    worst = jnp.zeros((A_SUB, LANES), F32)
    for ph, sub in [(ph, sub) for ph in range(n_ph) for sub in range(n_sub)]:
        if sub == 0:
            q = q_ref[0, ph]
            k_all = jnp.concatenate([kp_ref[0, ph], kc_ref[0, ph], kn_ref[0, ph]], axis=0)
            v_all = jnp.concatenate([vp_ref[0, ph], vc_ref[0, ph], vn_ref[0, ph]], axis=0)
        r0 = sub * A_SUB
        at_edge = sub == 0 or sub == n_sub - 1
        if at_edge:
            krow = i * tq + (r0 - A_RADIUS) + kcol
            valid = (krow >= 0) & (krow < phase_len)
        for pair in range(A_HEADS_PER_GROUP // 2):
            c0 = pair * LANES
            qp = q[r0:r0 + A_SUB, c0:c0 + LANES]
            kp = k_all[r0:r0 + tk, c0:c0 + LANES]
            vp = v_all[r0:r0 + tk, c0:c0 + LANES]
            zero = jnp.zeros_like(qp)
            qs = jnp.concatenate([jnp.where(low, qp, zero), jnp.where(low, zero, qp)], axis=0)
            s = lax.dot_general(qs, kp, _NT, preferred_element_type=F32) + bias_ref[pair]
            if at_edge:
                s = jnp.where(valid, s, NEG_INF)
            if stabilise:
                m = jnp.max(s, axis=1, keepdims=True)
                s = s - m
            p = jnp.exp2(s)
            l = jnp.sum(p, axis=1, keepdims=True)
            o = jnp.dot(p.astype(BF16), vp, preferred_element_type=F32)
            pick = lambda a: jnp.where(low, a[:A_SUB], a[A_SUB:])
            l = pick(l)
            lse = jnp.log2(l)
            worst = jnp.maximum(worst, jnp.abs(lse))
            if stabilise:
                lse = lse + pick(m)
            o_ref[0, ph, r0:r0 + A_SUB, c0:c0 + LANES] = (pick(o) * (1.0 / l)).astype(BF16)
            lse_ref[0, ph, r0:r0 + A_SUB, c0:c0 + LANES] = lse
    range_ref[0, 0, 0] = jnp.broadcast_to(
        jnp.max(jnp.max(worst, axis=1, keepdims=True), axis=0, keepdims=True), (SUBLANES, LANES))


def _attn_a_call(a_g, bias_g, *, stabilise):
    b, dil, phase_len, _ = a_g.shape
    tq = min(A_TQ, phase_len)
    n_steps = phase_len // tq
    n_ph = min(dil, A_TQ // tq)
    halo = A_RADIUS
    nh = phase_len // halo
    per = tq // halo
    cur = lambda col: (lambda bi, r, i: (bi, r, i, col))
    prev = lambda col: (lambda bi, r, i: (bi, r, jnp.maximum(i * per - 1, 0), col))
    nxt = lambda col: (lambda bi, r, i: (bi, r, jnp.minimum((i + 1) * per, nh - 1), col))
    blk = (1, n_ph, tq, A_WIDTH)
    hblk = (1, n_ph, halo, A_WIDTH)
    o_sds = jax.ShapeDtypeStruct((b, dil, phase_len, A_WIDTH), BF16)
    lse_sds = jax.ShapeDtypeStruct((b, dil, phase_len, A_WIDTH), F32)
    range_sds = jax.ShapeDtypeStruct((b, dil // n_ph, n_steps, SUBLANES, LANES), F32)
    return pl.pallas_call(
        functools.partial(_attn_a_kernel, phase_len=phase_len, stabilise=stabilise),
        out_shape=(o_sds, lse_sds, range_sds),
        grid=(b, dil // n_ph, n_steps),
        in_specs=[
            pl.BlockSpec(blk, cur(0)),
            pl.BlockSpec(hblk, prev(1)), pl.BlockSpec(blk, cur(1)), pl.BlockSpec(hblk, nxt(1)),
            pl.BlockSpec(hblk, prev(2)), pl.BlockSpec(blk, cur(2)), pl.BlockSpec(hblk, nxt(2)),
            pl.BlockSpec(bias_g.shape, lambda bi, r, i: (0, 0, 0)),
        ],
        out_specs=(pl.BlockSpec(blk, cur(0)), pl.BlockSpec(blk, cur(0)),
                   pl.BlockSpec((1, 1, 1, SUBLANES, LANES), lambda bi, r, i: (bi, r, i, 0, 0))),
        compiler_params=pltpu.CompilerParams(
            dimension_semantics=("arbitrary", "arbitrary", "arbitrary"),
            vmem_limit_bytes=VMEM_LIMIT_BYTES),
        name=f"attn_a_d{dil}" + ("_stab" if stabilise else ""),
    )(a_g, a_g, a_g, a_g, a_g, a_g, a_g, bias_g)


def _attn_a(a_g, bias_g):
    o, lse, worst = _attn_a_call(a_g, bias_g, stabilise=False)
    in_range = jnp.max(worst) <= A_MAX_ABS_LOG2_DENOM
    return lax.cond(in_range,
                    lambda: (o, lse),
                    lambda: _attn_a_call(a_g, bias_g, stabilise=True)[:2])


_VT_ROWS = 80


def _attn_b_kernel(kmax_ref, qt_ref, qn_ref, k_ref, vt_ref, o_ref, qst_ref, acc_ref, *m_scratch,
                   online):
    tq = qt_ref.shape[2]
    seq = k_ref.shape[1]
    kmax = kmax_ref[pl.program_id(0) * B_KV_HEADS + pl.program_id(1)]
    tail_row = lax.broadcasted_iota(jnp.int32, (LANES - HEAD_DIM, tq), 0)
    for h in range(B_GROUP):
        cols = slice(h * tq, (h + 1) * tq)
        qst_ref[0:HEAD_DIM, cols] = qt_ref[0, h * HEAD_DIM:(h + 1) * HEAD_DIM, :]
        if online:
            tail = jnp.zeros(tail_row.shape, F32)
        else:
            bound = jnp.sqrt(qn_ref[0, 0, h:h + 1, :]) * (kmax * B_BOUND_SLACK)
            tail = jnp.where(tail_row == _ONE_LANE - HEAD_DIM, -bound, 0.0)
        qst_ref[HEAD_DIM:LANES, cols] = tail.astype(BF16)
    acc_ref[...] = jnp.zeros(acc_ref.shape, F32)
    if online:
        m_ref, = m_scratch
        m_ref[...] = jnp.full(m_ref.shape, NEG_INF, F32)

    n_chunks = seq // B_TK

    def scores_t(c):
        start = pl.multiple_of(c * B_TK, B_TK)
        return jnp.dot(k_ref[0, pl.ds(start, B_TK), :], qst_ref[...],
                       preferred_element_type=F32)

    def values_t(c):
        start = pl.multiple_of(c * B_TK, B_TK)
        return vt_ref[0, 0:_VT_ROWS, pl.ds(start, B_TK)]

    if online:
        def body(c, carry):
            s_t = scores_t(c)
            m_prev = m_ref[...]
            m_new = jnp.maximum(m_prev, jnp.max(s_t, axis=0, keepdims=True))
            p_t = jnp.exp2(s_t - m_new).astype(BF16)
            acc_ref[...] = (jnp.exp2(m_prev - m_new) * acc_ref[...]
                            + jnp.dot(values_t(c), p_t, preferred_element_type=F32))
            m_ref[...] = m_new
            return carry

        lax.fori_loop(0, n_chunks, body, 0)
    else:
        def body(c, carry):
            start = pl.multiple_of(c * B_TK, B_TK)
            kc = k_ref[0, pl.ds(start, B_TK), :]
            vtc = values_t(c)
            for blk in range(B_GROUP * tq // B_CHAIN):
                cols = slice(blk * B_CHAIN, (blk + 1) * B_CHAIN)
                p_t = jnp.exp2(jnp.dot(kc, qst_ref[:, cols],
                                       preferred_element_type=F32)).astype(BF16)
                acc_ref[:, cols] += jnp.dot(vtc, p_t, preferred_element_type=F32)
            return carry

        lax.fori_loop(0, n_chunks, body, 0)
    acc = acc_ref[...]
    o_t = acc[0:HEAD_DIM, :] * (1.0 / acc[_ONE_LANE:_ONE_LANE + 1, :])
    for h in range(B_GROUP):
        o_ref[0, h * HEAD_DIM:(h + 1) * HEAD_DIM, :] = o_t[:, h * tq:(h + 1) * tq].astype(BF16)


def _attn_b_call(kmax, bqt, qn2, bk, bvt, *, online):
    b, _, s = bqt.shape
    qw = B_GROUP * HEAD_DIM
    cols = B_GROUP * B_TQ
    scratch = [pltpu.VMEM((LANES, cols), BF16), pltpu.VMEM((_VT_ROWS, cols), F32)]
    if online:
        scratch.append(pltpu.VMEM((1, cols), F32))
    return pl.pallas_call(
        functools.partial(_attn_b_kernel, online=online),
        out_shape=jax.ShapeDtypeStruct((b, B_WIDTH, s), BF16),
        grid_spec=pltpu.PrefetchScalarGridSpec(
            num_scalar_prefetch=1,
            grid=(b, B_KV_HEADS, s // B_TQ),
            in_specs=[
                pl.BlockSpec((1, qw, B_TQ), lambda bi, h, i, km: (bi, h, i)),
                pl.BlockSpec((1, 1, B_GROUP, B_TQ), lambda bi, h, i, km: (bi, h, 0, i)),
                pl.BlockSpec((1, s, LANES), lambda bi, h, i, km: (bi, 0, h)),
                pl.BlockSpec((1, LANES, s), lambda bi, h, i, km: (bi, h, 0)),
            ],
            out_specs=pl.BlockSpec((1, qw, B_TQ), lambda bi, h, i, km: (bi, h, i)),
            scratch_shapes=scratch,
        ),
        compiler_params=pltpu.CompilerParams(
            dimension_semantics=("arbitrary", "arbitrary", "arbitrary"),
            vmem_limit_bytes=VMEM_LIMIT_BYTES),
        name="attn_b_online" if online else "attn_b",
    )(kmax, bqt, qn2.reshape(b, B_KV_HEADS, B_GROUP, s), bk, bvt)


def _outproj_kernel(x_ref, gate_ref, o0_ref, l0_ref, o1_ref, l1_ref, o2_ref, l2_ref,
                    ybt_ref, gz_ref, wpa_ref, wpb_ref, wo_ref, lng_ref, lnb_ref,
                    out_ref, so1, sl1, so2, sl2, *, alpha):
    tm = x_ref.shape[1]
    d_model = x_ref.shape[2]
    for (o_ref, l_ref, so, sl, g) in ((o1_ref, l1_ref, so1, sl1, 1), (o2_ref, l2_ref, so2, sl2, 2)):
        dil = A_GROUPS[g][1]
        for r in range(dil):
            for ch in range(A_WIDTH // LANES):
                cs = slice(ch * LANES, (ch + 1) * LANES)
                so[ch, pl.ds(r, tm // dil, stride=dil), :] = o_ref[0, r, :, cs].astype(F32)
                sl[ch, pl.ds(r, tm // dil, stride=dil), :] = l_ref[0, r, :, cs]
    cat = lambda ref: jnp.concatenate([ref[ch] for ch in range(A_WIDTH // LANES)], axis=1)
    l0, l1, l2 = l0_ref[0], cat(sl1), cat(sl2)
    mx = jnp.maximum(jnp.maximum(l0, l1), l2)
    e0, e1, e2 = jnp.exp2(l0 - mx), jnp.exp2(l1 - mx), jnp.exp2(l2 - mx)
    y_a = ((e0 * o0_ref[0].astype(F32) + e1 * cat(so1) + e2 * cat(so2))
           * (1.0 / (e0 + e1 + e2)))
    gz = gz_ref[0]
    ya = (y_a * gz[:, 0:A_WIDTH].astype(F32)).astype(BF16)
    y_b = ybt_ref[0].astype(F32).T
    yb = (y_b * gz[:, A_WIDTH:A_WIDTH + B_WIDTH].astype(F32)).astype(BF16)
    pa = jnp.dot(ya, wpa_ref[...], preferred_element_type=F32)
    pb = jnp.dot(yb, wpb_ref[...], preferred_element_type=F32)
    o = A_WIDTH + B_WIDTH
    g_a = gz[:, o:o + d_model].astype(F32)
    g_b = gz[:, o + d_model:o + 2 * d_model].astype(F32)
    merged = (g_a * pa + g_b * pb).astype(BF16)
    out = jnp.dot(merged, wo_ref[...], preferred_element_type=F32)
    h = alpha * x_ref[0] + gate_ref[0] * out
    mu = jnp.mean(h, axis=-1, keepdims=True)
    hc = h - mu
    var = jnp.mean(hc * hc, axis=-1, keepdims=True)
    out_ref[0] = hc * lax.rsqrt(var + LN_EPS) * lng_ref[...] + lnb_ref[...]


def _outproj_call(x, gate, oa, ybt, gz, w_pa, w_pb, w_o, ln_g, ln_b, alpha):
    b, s, d = x.shape
    tm = OUT_TM
    (o0, l0), (o1, l1), (o2, l2) = oa
    d1, d2 = A_GROUPS[1][1], A_GROUPS[2][1]
    row = lambda bi, i: (bi, i, 0)
    ph = lambda bi, i: (bi, 0, i, 0)
    fixed = lambda bi, i: (0, 0)
    s0 = pl.BlockSpec((1, 1, tm, A_WIDTH), ph)
    s1 = pl.BlockSpec((1, d1, tm // d1, A_WIDTH), ph)
    s2 = pl.BlockSpec((1, d2, tm // d2, A_WIDTH), ph)

    def kern(x_ref, gate_ref, o0_ref, l0_ref, *rest):
        return _outproj_kernel(x_ref, gate_ref, o0_ref.at[0], l0_ref.at[0], *rest, alpha=alpha)

    return pl.pallas_call(
        kern,
        out_shape=jax.ShapeDtypeStruct((b, s, d), F32),
        grid=(b, s // tm),
        in_specs=[
            pl.BlockSpec((1, tm, d), row),
            pl.BlockSpec((1, 1, d), lambda bi, i: (bi, 0, 0)),
            s0, s0, s1, s1, s2, s2,
            pl.BlockSpec((1, B_WIDTH, tm), lambda bi, i: (bi, 0, i)),
            pl.BlockSpec((1, tm, gz.shape[2]), row),
            pl.BlockSpec(w_pa.shape, fixed),
            pl.BlockSpec(w_pb.shape, fixed),
            pl.BlockSpec(w_o.shape, fixed),
            pl.BlockSpec((1, d), fixed),
            pl.BlockSpec((1, d), fixed),
        ],
        out_specs=pl.BlockSpec((1, tm, d), row),
        scratch_shapes=[pltpu.VMEM((A_WIDTH // LANES, tm, LANES), F32)] * 4,
        compiler_params=pltpu.CompilerParams(
            dimension_semantics=("arbitrary", "arbitrary"),
            vmem_limit_bytes=VMEM_LIMIT_BYTES),
        name="outproj",
    )(x, gate, o0, l0, o1, l1, o2, l2, ybt, gz, w_pa, w_pb, w_o, ln_g, ln_b)


def _t5_bucket(rel):
    half = REL_BUCKETS // 2
    max_exact = half // 2
    ret = jnp.where(rel > 0, half, 0)
    a = jnp.abs(rel)
    af = jnp.maximum(a, 1).astype(F32)
    large = max_exact + (jnp.log(af / max_exact) / math.log(REL_MAX_DISTANCE / max_exact)
                         * (half - max_exact)).astype(jnp.int32)
    large = jnp.minimum(large, half - 1)
    return ret + jnp.where(a < max_exact, a, large)


def _window_bias(rel_table, g):
    dil = A_GROUPS[g][1]
    tk = A_SUB + 2 * A_RADIUS
    rel = jnp.arange(tk)[None, :] - A_RADIUS - jnp.arange(A_SUB)[:, None]
    table_g = rel_table[:, g * A_HEADS_PER_GROUP:(g + 1) * A_HEADS_PER_GROUP]
    onehot = (_t5_bucket(rel * dil)[..., None] == jnp.arange(REL_BUCKETS)).astype(F32)
    bias = jnp.einsum("qkb,bh->hqk", onehot, table_g.astype(F32), precision=lax.Precision.HIGHEST)
    bias = jnp.where((jnp.abs(rel) <= A_RADIUS)[None], bias * LOG2_E, NEG_INF)
    return bias.reshape(A_HEADS_PER_GROUP // 2, 2 * A_SUB, tk)


def _rope_tables(seq):
    t = jnp.arange(seq)
    row = (t // GRID_W).astype(F32)
    col = (t % GRID_W).astype(F32)
    half = HEAD_DIM // 2
    inv = ROPE_THETA ** (-jnp.arange(0, half, 2, dtype=F32) / half)
    ar, ac = row[:, None] * inv[None], col[:, None] * inv[None]
    cos = jnp.concatenate([jnp.cos(ar), jnp.cos(ar), jnp.cos(ac), jnp.cos(ac)], axis=1)
    sin = jnp.concatenate([-jnp.sin(ar), jnp.sin(ar), -jnp.sin(ac), jnp.sin(ac)], axis=1)
    return jnp.tile(cos, (1, LANES // HEAD_DIM)), jnp.tile(sin, (1, LANES // HEAD_DIM))


def _padded_kv_weights(w):
    d = w.shape[0]
    off = _OFF_BQ + B_WIDTH
    zeros = jnp.zeros((d, LANES - HEAD_DIM), w.dtype)
    blocks = []
    for h in range(2 * B_KV_HEADS):
        blocks += [w[:, off + h * HEAD_DIM:off + (h + 1) * HEAD_DIM], zeros]
    return jnp.concatenate(blocks, axis=1).astype(BF16)


def _pad_heads(g):
    return jnp.concatenate([g, jnp.zeros((LANES - HEAD_DIM,), g.dtype)])


def kernel(x, c, rel_table, ln_g, ln_b, w_ada, b_ada, w_in, b_gate, q_norm_g, k_norm_g, w_pa, w_pb, w_o):
    depth = w_in.shape[0]
    b, s, d = x.shape
    alpha = float((2 * depth) ** 0.25)

    c_pad = jnp.zeros((SUBLANES, d), F32).at[:b].set(c)
    mod = _ada_call(c_pad, w_ada, b_ada)[:, :b]
    cos_t, sin_t = _rope_tables(s)
    idx = jnp.arange(_BD_W) // HEAD_DIM
    bd = (idx[:, None] == idx[None, :]).astype(BF16)
    biases = [_window_bias(rel_table, g) for g in range(len(A_GROUPS))]
    w_all = w_in.astype(BF16)

    for l in range(depth):
        shift = mod[l, :, 0:d].reshape(b, 1, d)
        scale = mod[l, :, d:2 * d].reshape(b, 1, d)
        gate = mod[l, :, 2 * d:3 * d].reshape(b, 1, d)
        gqk = jnp.concatenate([jnp.tile(q_norm_g[l], B_Q_HEADS),
                               jnp.tile(_pad_heads(k_norm_g[l]), B_KV_HEADS)]).reshape(1, _QK_W)
        a0, a1, a2, bqt, qn2, bk, bvt, gz, st = _inproj_call(
            x, shift, scale, w_all, l, _padded_kv_weights(w_in[l]), b_gate[l].reshape(1, 2 * d), gqk,
            cos_t, sin_t, bd)
        a0 = a0.reshape(b, 1, s, _A_SEG)
        oa = [_attn_a(a_g, biases[g]) for g, a_g in enumerate((a0, a1, a2))]
        kmax = jnp.sqrt(jnp.max(st[:, :, 0:B_KV_HEADS, 0], axis=1))
        qmax = jnp.sqrt(jnp.max(st[:, :, B_KV_HEADS, 0]))
        bound_ok = qmax * jnp.max(kmax) * B_BOUND_SLACK <= B_MAX_BOUND
        ybt = lax.cond(
            bound_ok,
            functools.partial(_attn_b_call, online=False),
            functools.partial(_attn_b_call, online=True),
            kmax.reshape(-1), bqt, qn2, bk, bvt)
        x = _outproj_call(x, gate, oa, ybt, gz, w_pa[l].astype(BF16), w_pb[l].astype(BF16),
                          w_o[l].astype(BF16), ln_g[l].reshape(1, d), ln_b[l].reshape(1, d), alpha)
    return x
```

```python
import functools
import math

import jax
import jax.numpy as jnp
from jax import lax
from jax.experimental import pallas as pl
from jax.experimental.pallas import tpu as pltpu

HEAD_DIM = 64
A_GROUPS = ((128, 1), (512, 4), (2048, 16))
A_HEADS_PER_GROUP = 8
A_WIDTH = A_HEADS_PER_GROUP * HEAD_DIM
A_RADIUS = 64
B_Q_HEADS = 8
B_KV_HEADS = 2
B_GROUP = B_Q_HEADS // B_KV_HEADS
B_WIDTH = B_Q_HEADS * HEAD_DIM
GRID_W = 64
ROPE_THETA = 10000.0
REL_BUCKETS = 32
REL_MAX_DISTANCE = 1024
LN_EPS = 1e-5
QK_EPS = 1e-6
NEG_INF = -1e30
LOG2_E = math.log2(math.e)
_Q_SCALE = HEAD_DIM ** -0.5 * LOG2_E

LANES = 128
SUBLANES = 8
VMEM_LIMIT_BYTES = 56 * 1024 * 1024

ADA_TN = 1024
IN_TM = 256
A_TQ = 4096
A_SUB = 128
B_TQ = 1024
B_TK = 2048
B_CHAIN = 1024
OUT_TM = 512

B_MAX_BOUND = 50.0
A_MAX_ABS_LOG2_DENOM = 90.0
B_BOUND_SLACK = 1.0 + 2.0 ** -6

BF16 = jnp.bfloat16
F32 = jnp.float32

_NT = (((1,), (1,)), ((), ()))


def _ada_kernel(c_ref, w_ref, b_ref, o_ref):
    c = c_ref[...]
    h = c * jax.nn.sigmoid(c)
    o_ref[0] = jnp.dot(h, w_ref[0], preferred_element_type=F32,
                       precision=lax.Precision.HIGHEST) + b_ref[0]


def _ada_call(c_pad, w_ada, b_ada):
    depth, d, n3 = w_ada.shape
    rows = c_pad.shape[0]
    return pl.pallas_call(
        _ada_kernel,
        out_shape=jax.ShapeDtypeStruct((depth, rows, n3), F32),
        grid=(depth, n3 // ADA_TN),
        in_specs=[
            pl.BlockSpec((rows, d), lambda l, j: (0, 0)),
            pl.BlockSpec((1, d, ADA_TN), lambda l, j: (l, 0, j)),
            pl.BlockSpec((1, 1, ADA_TN), lambda l, j: (l, 0, j)),
        ],
        out_specs=pl.BlockSpec((1, rows, ADA_TN), lambda l, j: (l, 0, j)),
        compiler_params=pltpu.CompilerParams(
            dimension_semantics=("arbitrary", "arbitrary")),
        name="ada_mod",
    )(c_pad, w_ada, b_ada.reshape(depth, 1, n3))


_A_SEG = 3 * A_WIDTH
_A_PART = len(A_GROUPS) * A_WIDTH
_OFF_AZ = 3 * _A_PART
_OFF_BQ = _OFF_AZ + A_WIDTH
_KV_PAD = B_KV_HEADS * LANES
_OFF_BZ = _OFF_BQ + B_WIDTH + 2 * B_KV_HEADS * HEAD_DIM
_OFF_GL = _OFF_BZ + B_WIDTH
_QK_W = B_WIDTH + _KV_PAD
_BD_W = 2 * LANES
_ONE_LANE = HEAD_DIM


def _inproj_kernel(x_ref, shift_ref, scale_ref, w_ref, wkv_ref, bgate_ref, gqk_ref,
                   cos_ref, sin_ref, bd_ref,
                   a0_ref, a1_ref, a2_ref, bqt_ref, qn_ref, bk_ref, bvt_ref, gz_ref, st_ref,
                   scr_ref):
    tm = x_ref.shape[1]
    d_model = x_ref.shape[2]
    u = (x_ref[0] * (1.0 + scale_ref[0]) + shift_ref[0]).astype(BF16)

    def proj(off, width):
        return jnp.dot(u, w_ref[:, off:off + width], preferred_element_type=F32)

    kv_lane = lax.broadcasted_iota(jnp.int32, (1, _KV_PAD), 1)
    one_col = ((kv_lane & (LANES - 1)) == _ONE_LANE).astype(F32)

    qk = jnp.concatenate(
        [proj(_OFF_BQ, B_WIDTH),
         jnp.dot(u, wkv_ref[:, 0:_KV_PAD], preferred_element_type=F32)], axis=1)
    sq = qk * qk
    sq_hi = sq.astype(BF16)
    sq_lo = (sq - sq_hi.astype(F32)).astype(BF16)
    def head_sums(part):
        return jnp.concatenate(
            [jnp.dot(part[:, c:c + _BD_W], bd_ref[...], preferred_element_type=F32)
             for c in range(0, _QK_W, _BD_W)], axis=1)
    ss = head_sums(sq_hi) + head_sums(sq_lo)
    y = qk * lax.rsqrt(ss * (1.0 / HEAD_DIM) + QK_EPS) * gqk_ref[...]
    reps = _QK_W // LANES
    cos = jnp.concatenate([cos_ref[...]] * reps, axis=1)
    sin = jnp.concatenate([sin_ref[...]] * reps, axis=1)
    lane = lax.broadcasted_iota(jnp.int32, y.shape, 1)
    first = (lane & 31) < 16
    swapped = jnp.where(first, pltpu.roll(y, _QK_W - 16, 1), pltpu.roll(y, 16, 1))
    rot = y * cos + swapped * sin
    q = rot[:, :B_WIDTH] * _Q_SCALE
    k = rot[:, B_WIDTH:]
    bk_ref[0] = (k + one_col).astype(BF16)
    q_t = q.T
    bqt_ref[0] = q_t.astype(BF16)
    qn2 = jnp.concatenate(
        [jnp.sum(jnp.square(q_t[h * HEAD_DIM:(h + 1) * HEAD_DIM]), axis=0, keepdims=True)
         for h in range(B_Q_HEADS)], axis=0)
    qn_ref[0] = qn2

    def tile_max(v2):
        return jnp.max(jnp.sum(v2, axis=1, keepdims=True), axis=0, keepdims=True)
    k2 = k * k
    kmax = [tile_max(k2[:, h * LANES:(h + 1) * LANES]) for h in range(B_KV_HEADS)]
    qmax = jnp.max(jnp.max(qn2, axis=1, keepdims=True), axis=0, keepdims=True)
    sub = lax.broadcasted_iota(jnp.int32, (SUBLANES, LANES), 0)
    st_ref[0, 0] = jnp.where(sub == 0, kmax[0], jnp.where(sub == 1, kmax[1], qmax))

    def proj_a(g):
        chunks = []
        for part in range(3):
            res = proj(part * _A_PART + g * A_WIDTH, A_WIDTH)
            if part == 0:
                res = res * _Q_SCALE
            chunks += [res[:, ch * LANES:(ch + 1) * LANES] for ch in range(A_WIDTH // LANES)]
        return chunks

    def split_group(g, a_ref):
        dil = A_GROUPS[g][1]
        for ch, chunk in enumerate(proj_a(g)):
            scr_ref[ch] = chunk
        for r in range(dil):
            for ch in range(_A_SEG // LANES):
                a_ref[0, r, :, ch * LANES:(ch + 1) * LANES] = (
                    scr_ref[ch, pl.ds(r, tm // dil, stride=dil), :].astype(BF16))

    def branch_gates():
        az = proj(_OFF_AZ, A_WIDTH)
        gz_ref[0, :, 0:A_WIDTH] = (az * jax.nn.sigmoid(az)).astype(BF16)
        bz = proj(_OFF_BZ, B_WIDTH)
        gz_ref[0, :, A_WIDTH:A_WIDTH + B_WIDTH] = (bz * jax.nn.sigmoid(bz)).astype(BF16)

    def merge_gates():
        for j in range(2):
            gl = proj(_OFF_GL + j * d_model, d_model) + bgate_ref[:, j * d_model:(j + 1) * d_model]
            o = A_WIDTH + B_WIDTH + j * d_model
            gz_ref[0, :, o:o + d_model] = jax.nn.sigmoid(gl).astype(BF16)

    split_group(1, a1_ref)
    split_group(2, a2_ref)
    branch_gates()
    merge_gates()

    a0_ref[0] = jnp.concatenate(proj_a(0), axis=1).astype(BF16)
    v_pad = jnp.dot(u, wkv_ref[:, _KV_PAD:2 * _KV_PAD], preferred_element_type=F32)
    bvt_ref[0] = (v_pad + one_col).T.astype(BF16)


def _inproj_call(x, shift, scale, w_all, layer, w_kv, b_gate, gqk, cos_t, sin_t, bd):
    b, s, d = x.shape
    tm = IN_TM
    n_cols = w_all.shape[2]
    d1, d2 = A_GROUPS[1][1], A_GROUPS[2][1]
    const = dict(pipeline_mode=pl.Buffered(1))
    out_shape = (
        jax.ShapeDtypeStruct((b, s, _A_SEG), BF16),
        jax.ShapeDtypeStruct((b, d1, s // d1, _A_SEG), BF16),
        jax.ShapeDtypeStruct((b, d2, s // d2, _A_SEG), BF16),
        jax.ShapeDtypeStruct((b, B_WIDTH, s), BF16),
        jax.ShapeDtypeStruct((b, B_Q_HEADS, s), F32),
        jax.ShapeDtypeStruct((b, s, _KV_PAD), BF16),
        jax.ShapeDtypeStruct((b, _KV_PAD, s), BF16),
        jax.ShapeDtypeStruct((b, s, A_WIDTH + B_WIDTH + 2 * d), BF16),
        jax.ShapeDtypeStruct((b, s // tm, SUBLANES, LANES), F32),
    )
    row = lambda bi, i: (bi, i, 0)
    out_specs = (
        pl.BlockSpec((1, tm, _A_SEG), row),
        pl.BlockSpec((1, d1, tm // d1, _A_SEG), lambda bi, i: (bi, 0, i, 0)),
        pl.BlockSpec((1, d2, tm // d2, _A_SEG), lambda bi, i: (bi, 0, i, 0)),
        pl.BlockSpec((1, B_WIDTH, tm), lambda bi, i: (bi, 0, i)),
        pl.BlockSpec((1, B_Q_HEADS, tm), lambda bi, i: (bi, 0, i)),
        pl.BlockSpec((1, tm, _KV_PAD), row),
        pl.BlockSpec((1, _KV_PAD, tm), lambda bi, i: (bi, 0, i)),
        pl.BlockSpec((1, tm, A_WIDTH + B_WIDTH + 2 * d), row),
        pl.BlockSpec((1, 1, SUBLANES, LANES), lambda bi, i: (bi, i, 0, 0)),
    )
    in_specs = [
        pl.BlockSpec((1, tm, d), row),
        pl.BlockSpec((1, 1, d), lambda bi, i: (bi, 0, 0)),
        pl.BlockSpec((1, 1, d), lambda bi, i: (bi, 0, 0)),
        pl.BlockSpec((None, d, n_cols), lambda bi, i: (layer, 0, 0), **const),
        pl.BlockSpec((d, 2 * _KV_PAD), lambda bi, i: (0, 0), **const),
        pl.BlockSpec((1, 2 * d), lambda bi, i: (0, 0), **const),
        pl.BlockSpec((1, _QK_W), lambda bi, i: (0, 0), **const),
        pl.BlockSpec((tm, LANES), lambda bi, i: (i, 0)),
        pl.BlockSpec((tm, LANES), lambda bi, i: (i, 0)),
        pl.BlockSpec((_BD_W, _BD_W), lambda bi, i: (0, 0), **const),
    ]
    return pl.pallas_call(
        _inproj_kernel,
        out_shape=out_shape,
        grid=(b, s // tm),
        in_specs=in_specs,
        out_specs=out_specs,
        scratch_shapes=[pltpu.VMEM((_A_SEG // LANES, tm, LANES), F32)],
        compiler_params=pltpu.CompilerParams(
            dimension_semantics=("arbitrary", "arbitrary"),
            vmem_limit_bytes=VMEM_LIMIT_BYTES),
        name="inproj",
    )(x, shift, scale, w_all, w_kv, b_gate, gqk, cos_t, sin_t, bd)


def _attn_a_kernel(q_ref, kp_ref, kc_ref, kn_ref, vp_ref, vc_ref, vn_ref, bias_ref,
                   o_ref, lse_ref, range_ref, *, phase_len, stabilise):
    n_ph, tq = q_ref.shape[1], q_ref.shape[2]
    i = pl.program_id(2)
    tk = A_SUB + 2 * A_RADIUS
    n_sub = tq // A_SUB
    lane = lax.broadcasted_iota(jnp.int32, (A_SUB, LANES), 1)
    low = lane < HEAD_DIM
    kcol = lax.broadcasted_iota(jnp.int32, (1, tk), 1)
    worst = jnp.zeros((A_SUB, LANES), F32)
    for ph, sub in [(ph, sub) for ph in range(n_ph) for sub in range(n_sub)]:
        if sub == 0:
            q = q_ref[0, ph]
            k_all = jnp.concatenate([kp_ref[0, ph], kc_ref[0, ph], kn_ref[0, ph]], axis=0)
            v_all = jnp.concatenate([vp_ref[0, ph], vc_ref[0, ph], vn_ref[0, ph]], axis=0)
        r0 = sub * A_SUB
        at_edge = sub == 0 or sub == n_sub - 1
        if at_edge:
            krow = i * tq + (r0 - A_RADIUS) + kcol
            valid = (krow >= 0) & (krow < phase_len)
        for pair in range(A_HEADS_PER_GROUP // 2):
            c0 = pair * LANES
            qp = q[r0:r0 + A_SUB, c0:c0 + LANES]
            kp = k_all[r0:r0 + tk, c0:c0 + LANES]
            vp = v_all[r0:r0 + tk, c0:c0 + LANES]
            zero = jnp.zeros_like(qp)
            qs = jnp.concatenate([jnp.where(low, qp, zero), jnp.where(low, zero, qp)], axis=0)
            s = lax.dot_general(qs, kp, _NT, preferred_element_type=F32) + bias_ref[pair]
            if at_edge:
                s = jnp.where(valid, s, NEG_INF)
            if stabilise:
                m = jnp.max(s, axis=1, keepdims=True)
                s = s - m
            p = jnp.exp2(s)
            l = jnp.sum(p, axis=1, keepdims=True)
            o = jnp.dot(p.astype(BF16), vp, preferred_element_type=F32)
            pick = lambda a: jnp.where(low, a[:A_SUB], a[A_SUB:])
            l = pick(l)
            lse = jnp.log2(l)
            worst = jnp.maximum(worst, jnp.abs(lse))
            if stabilise:
                lse = lse + pick(m)
            o_ref[0, ph, r0:r0 + A_SUB, c0:c0 + LANES] = (pick(o) * (1.0 / l)).astype(BF16)
            lse_ref[0, ph, r0:r0 + A_SUB, c0:c0 + LANES] = lse
    range_ref[0, 0, 0] = jnp.broadcast_to(
        jnp.max(jnp.max(worst, axis=1, keepdims=True), axis=0, keepdims=True), (SUBLANES, LANES))


def _attn_a_call(a_g, bias_g, *, stabilise):
    b, dil, phase_len, _ = a_g.shape
    tq = min(A_TQ, phase_len)
    n_steps = phase_len // tq
    n_ph = min(dil, A_TQ // tq)
    halo = A_RADIUS
    nh = phase_len // halo
    per = tq // halo
    cur = lambda col: (lambda bi, r, i: (bi, r, i, col))
    prev = lambda col: (lambda bi, r, i: (bi, r, jnp.maximum(i * per - 1, 0), col))
    nxt = lambda col: (lambda bi, r, i: (bi, r, jnp.minimum((i + 1) * per, nh - 1), col))
    blk = (1, n_ph, tq, A_WIDTH)
    hblk = (1, n_ph, halo, A_WIDTH)
    o_sds = jax.ShapeDtypeStruct((b, dil, phase_len, A_WIDTH), BF16)
    lse_sds = jax.ShapeDtypeStruct((b, dil, phase_len, A_WIDTH), F32)
    range_sds = jax.ShapeDtypeStruct((b, dil // n_ph, n_steps, SUBLANES, LANES), F32)
    return pl.pallas_call(
        functools.partial(_attn_a_kernel, phase_len=phase_len, stabilise=stabilise),
        out_shape=(o_sds, lse_sds, range_sds),
        grid=(b, dil // n_ph, n_steps),
        in_specs=[
            pl.BlockSpec(blk, cur(0)),
            pl.BlockSpec(hblk, prev(1)), pl.BlockSpec(blk, cur(1)), pl.BlockSpec(hblk, nxt(1)),
            pl.BlockSpec(hblk, prev(2)), pl.BlockSpec(blk, cur(2)), pl.BlockSpec(hblk, nxt(2)),
            pl.BlockSpec(bias_g.shape, lambda bi, r, i: (0, 0, 0)),
        ],
        out_specs=(pl.BlockSpec(blk, cur(0)), pl.BlockSpec(blk, cur(0)),
                   pl.BlockSpec((1, 1, 1, SUBLANES, LANES), lambda bi, r, i: (bi, r, i, 0, 0))),
        compiler_params=pltpu.CompilerParams(
            dimension_semantics=("arbitrary", "arbitrary", "arbitrary"),
            vmem_limit_bytes=VMEM_LIMIT_BYTES),
        name=f"attn_a_d{dil}" + ("_stab" if stabilise else ""),
    )(a_g, a_g, a_g, a_g, a_g, a_g, a_g, bias_g)


def _attn_a(a_g, bias_g):
    o, lse, worst = _attn_a_call(a_g, bias_g, stabilise=False)
    in_range = jnp.max(worst) <= A_MAX_ABS_LOG2_DENOM
    return lax.cond(in_range,
                    lambda: (o, lse),
                    lambda: _attn_a_call(a_g, bias_g, stabilise=True)[:2])


_VT_ROWS = 80


def _attn_b_kernel(kmax_ref, qt_ref, qn_ref, k_ref, vt_ref, o_ref, qst_ref, acc_ref, *m_scratch,
                   online):
    tq = qt_ref.shape[2]
    seq = k_ref.shape[1]
    kmax = kmax_ref[pl.program_id(0) * B_KV_HEADS + pl.program_id(1)]
    tail_row = lax.broadcasted_iota(jnp.int32, (LANES - HEAD_DIM, tq), 0)
    for h in range(B_GROUP):
        cols = slice(h * tq, (h + 1) * tq)
        qst_ref[0:HEAD_DIM, cols] = qt_ref[0, h * HEAD_DIM:(h + 1) * HEAD_DIM, :]
        if online:
            tail = jnp.zeros(tail_row.shape, F32)
        else:
            bound = jnp.sqrt(qn_ref[0, 0, h:h + 1, :]) * (kmax * B_BOUND_SLACK)
            tail = jnp.where(tail_row == _ONE_LANE - HEAD_DIM, -bound, 0.0)
        qst_ref[HEAD_DIM:LANES, cols] = tail.astype(BF16)
    acc_ref[...] = jnp.zeros(acc_ref.shape, F32)
    if online:
        m_ref, = m_scratch
        m_ref[...] = jnp.full(m_ref.shape, NEG_INF, F32)

    n_chunks = seq // B_TK

    def scores_t(c):
        start = pl.multiple_of(c * B_TK, B_TK)
        return jnp.dot(k_ref[0, pl.ds(start, B_TK), :], qst_ref[...],
                       preferred_element_type=F32)

    def values_t(c):
        start = pl.multiple_of(c * B_TK, B_TK)
        return vt_ref[0, 0:_VT_ROWS, pl.ds(start, B_TK)]

    if online:
        def body(c, carry):
            s_t = scores_t(c)
            m_prev = m_ref[...]
            m_new = jnp.maximum(m_prev, jnp.max(s_t, axis=0, keepdims=True))
            p_t = jnp.exp2(s_t - m_new).astype(BF16)
            acc_ref[...] = (jnp.exp2(m_prev - m_new) * acc_ref[...]
                            + jnp.dot(values_t(c), p_t, preferred_element_type=F32))
            m_ref[...] = m_new
            return carry

        lax.fori_loop(0, n_chunks, body, 0)
    else:
        def body(c, carry):
            start = pl.multiple_of(c * B_TK, B_TK)
            kc = k_ref[0, pl.ds(start, B_TK), :]
            vtc = values_t(c)
            for blk in range(B_GROUP * tq // B_CHAIN):
                cols = slice(blk * B_CHAIN, (blk + 1) * B_CHAIN)
                p_t = jnp.exp2(jnp.dot(kc, qst_ref[:, cols],
                                       preferred_element_type=F32)).astype(BF16)
                acc_ref[:, cols] += jnp.dot(vtc, p_t, preferred_element_type=F32)
            return carry

        lax.fori_loop(0, n_chunks, body, 0)
    acc = acc_ref[...]
    o_t = acc[0:HEAD_DIM, :] * (1.0 / acc[_ONE_LANE:_ONE_LANE + 1, :])
    for h in range(B_GROUP):
        o_ref[0, h * HEAD_DIM:(h + 1) * HEAD_DIM, :] = o_t[:, h * tq:(h + 1) * tq].astype(BF16)


def _attn_b_call(kmax, bqt, qn2, bk, bvt, *, online):
    b, _, s = bqt.shape
    qw = B_GROUP * HEAD_DIM
    cols = B_GROUP * B_TQ
    scratch = [pltpu.VMEM((LANES, cols), BF16), pltpu.VMEM((_VT_ROWS, cols), F32)]
    if online:
        scratch.append(pltpu.VMEM((1, cols), F32))
    return pl.pallas_call(
        functools.partial(_attn_b_kernel, online=online),
        out_shape=jax.ShapeDtypeStruct((b, B_WIDTH, s), BF16),
        grid_spec=pltpu.PrefetchScalarGridSpec(
            num_scalar_prefetch=1,
            grid=(b, B_KV_HEADS, s // B_TQ),
            in_specs=[
                pl.BlockSpec((1, qw, B_TQ), lambda bi, h, i, km: (bi, h, i)),
                pl.BlockSpec((1, 1, B_GROUP, B_TQ), lambda bi, h, i, km: (bi, h, 0, i)),
                pl.BlockSpec((1, s, LANES), lambda bi, h, i, km: (bi, 0, h)),
                pl.BlockSpec((1, LANES, s), lambda bi, h, i, km: (bi, h, 0)),
            ],
            out_specs=pl.BlockSpec((1, qw, B_TQ), lambda bi, h, i, km: (bi, h, i)),
            scratch_shapes=scratch,
        ),
        compiler_params=pltpu.CompilerParams(
            dimension_semantics=("arbitrary", "arbitrary", "arbitrary"),
            vmem_limit_bytes=VMEM_LIMIT_BYTES),
        name="attn_b_online" if online else "attn_b",
    )(kmax, bqt, qn2.reshape(b, B_KV_HEADS, B_GROUP, s), bk, bvt)


def _outproj_kernel(x_ref, gate_ref, o0_ref, l0_ref, o1_ref, l1_ref, o2_ref, l2_ref,
                    ybt_ref, gz_ref, wpa_ref, wpb_ref, wo_ref, lng_ref, lnb_ref,
                    out_ref, so1, sl1, so2, sl2, *, alpha):
    tm = x_ref.shape[1]
    d_model = x_ref.shape[2]
    for (o_ref, l_ref, so, sl, g) in ((o1_ref, l1_ref, so1, sl1, 1), (o2_ref, l2_ref, so2, sl2, 2)):
        dil = A_GROUPS[g][1]
        for r in range(dil):
            for ch in range(A_WIDTH // LANES):
                cs = slice(ch * LANES, (ch + 1) * LANES)
                so[ch, pl.ds(r, tm // dil, stride=dil), :] = o_ref[0, r, :, cs].astype(F32)
                sl[ch, pl.ds(r, tm // dil, stride=dil), :] = l_ref[0, r, :, cs]
    cat = lambda ref: jnp.concatenate([ref[ch] for ch in range(A_WIDTH // LANES)], axis=1)
    l0, l1, l2 = l0_ref[0], cat(sl1), cat(sl2)
    mx = jnp.maximum(jnp.maximum(l0, l1), l2)
    e0, e1, e2 = jnp.exp2(l0 - mx), jnp.exp2(l1 - mx), jnp.exp2(l2 - mx)
    y_a = ((e0 * o0_ref[0].astype(F32) + e1 * cat(so1) + e2 * cat(so2))
           * (1.0 / (e0 + e1 + e2)))
    gz = gz_ref[0]
    ya = (y_a * gz[:, 0:A_WIDTH].astype(F32)).astype(BF16)
    y_b = ybt_ref[0].astype(F32).T
    yb = (y_b * gz[:, A_WIDTH:A_WIDTH + B_WIDTH].astype(F32)).astype(BF16)
    pa = jnp.dot(ya, wpa_ref[...], preferred_element_type=F32)
    pb = jnp.dot(yb, wpb_ref[...], preferred_element_type=F32)
    o = A_WIDTH + B_WIDTH
    g_a = gz[:, o:o + d_model].astype(F32)
    g_b = gz[:, o + d_model:o + 2 * d_model].astype(F32)
    merged = (g_a * pa + g_b * pb).astype(BF16)
    out = jnp.dot(merged, wo_ref[...], preferred_element_type=F32)
    h = alpha * x_ref[0] + gate_ref[0] * out
    mu = jnp.mean(h, axis=-1, keepdims=True)
    hc = h - mu
    var = jnp.mean(hc * hc, axis=-1, keepdims=True)
    out_ref[0] = hc * lax.rsqrt(var + LN_EPS) * lng_ref[...] + lnb_ref[...]


def _outproj_call(x, gate, oa, ybt, gz, w_pa, w_pb, w_o, ln_g, ln_b, alpha):
    b, s, d = x.shape
    tm = OUT_TM
    (o0, l0), (o1, l1), (o2, l2) = oa
    d1, d2 = A_GROUPS[1][1], A_GROUPS[2][1]
    row = lambda bi, i: (bi, i, 0)
    ph = lambda bi, i: (bi, 0, i, 0)
    fixed = lambda bi, i: (0, 0)
    s0 = pl.BlockSpec((1, 1, tm, A_WIDTH), ph)
    s1 = pl.BlockSpec((1, d1, tm // d1, A_WIDTH), ph)
    s2 = pl.BlockSpec((1, d2, tm // d2, A_WIDTH), ph)

    def kern(x_ref, gate_ref, o0_ref, l0_ref, *rest):
        return _outproj_kernel(x_ref, gate_ref, o0_ref.at[0], l0_ref.at[0], *rest, alpha=alpha)

    return pl.pallas_call(
        kern,
        out_shape=jax.ShapeDtypeStruct((b, s, d), F32),
        grid=(b, s // tm),
        in_specs=[
            pl.BlockSpec((1, tm, d), row),
            pl.BlockSpec((1, 1, d), lambda bi, i: (bi, 0, 0)),
            s0, s0, s1, s1, s2, s2,
            pl.BlockSpec((1, B_WIDTH, tm), lambda bi, i: (bi, 0, i)),
            pl.BlockSpec((1, tm, gz.shape[2]), row),
            pl.BlockSpec(w_pa.shape, fixed),
            pl.BlockSpec(w_pb.shape, fixed),
            pl.BlockSpec(w_o.shape, fixed),
            pl.BlockSpec((1, d), fixed),
            pl.BlockSpec((1, d), fixed),
        ],
        out_specs=pl.BlockSpec((1, tm, d), row),
        scratch_shapes=[pltpu.VMEM((A_WIDTH // LANES, tm, LANES), F32)] * 4,
        compiler_params=pltpu.CompilerParams(
            dimension_semantics=("arbitrary", "arbitrary"),
            vmem_limit_bytes=VMEM_LIMIT_BYTES),
        name="outproj",
    )(x, gate, o0, l0, o1, l1, o2, l2, ybt, gz, w_pa, w_pb, w_o, ln_g, ln_b)


def _t5_bucket(rel):
    half = REL_BUCKETS // 2
    max_exact = half // 2
    ret = jnp.where(rel > 0, half, 0)
    a = jnp.abs(rel)
    af = jnp.maximum(a, 1).astype(F32)
    large = max_exact + (jnp.log(af / max_exact) / math.log(REL_MAX_DISTANCE / max_exact)
                         * (half - max_exact)).astype(jnp.int32)
    large = jnp.minimum(large, half - 1)
    return ret + jnp.where(a < max_exact, a, large)


def _window_bias(rel_table, g):
    dil = A_GROUPS[g][1]
    tk = A_SUB + 2 * A_RADIUS
    rel = jnp.arange(tk)[None, :] - A_RADIUS - jnp.arange(A_SUB)[:, None]
    table_g = rel_table[:, g * A_HEADS_PER_GROUP:(g + 1) * A_HEADS_PER_GROUP]
    onehot = (_t5_bucket(rel * dil)[..., None] == jnp.arange(REL_BUCKETS)).astype(F32)
    bias = jnp.einsum("qkb,bh->hqk", onehot, table_g.astype(F32), precision=lax.Precision.HIGHEST)
    bias = jnp.where((jnp.abs(rel) <= A_RADIUS)[None], bias * LOG2_E, NEG_INF)
    return bias.reshape(A_HEADS_PER_GROUP // 2, 2 * A_SUB, tk)


def _rope_tables(seq):
    t = jnp.arange(seq)
    row = (t // GRID_W).astype(F32)
    col = (t % GRID_W).astype(F32)
    half = HEAD_DIM // 2
    inv = ROPE_THETA ** (-jnp.arange(0, half, 2, dtype=F32) / half)
    ar, ac = row[:, None] * inv[None], col[:, None] * inv[None]
    cos = jnp.concatenate([jnp.cos(ar), jnp.cos(ar), jnp.cos(ac), jnp.cos(ac)], axis=1)
    sin = jnp.concatenate([-jnp.sin(ar), jnp.sin(ar), -jnp.sin(ac), jnp.sin(ac)], axis=1)
    return jnp.tile(cos, (1, LANES // HEAD_DIM)), jnp.tile(sin, (1, LANES // HEAD_DIM))


def _padded_kv_weights(w):
    d = w.shape[0]
    off = _OFF_BQ + B_WIDTH
    zeros = jnp.zeros((d, LANES - HEAD_DIM), w.dtype)
    blocks = []
    for h in range(2 * B_KV_HEADS):
        blocks += [w[:, off + h * HEAD_DIM:off + (h + 1) * HEAD_DIM], zeros]
    return jnp.concatenate(blocks, axis=1).astype(BF16)


def _pad_heads(g):
    return jnp.concatenate([g, jnp.zeros((LANES - HEAD_DIM,), g.dtype)])


def kernel(x, c, rel_table, ln_g, ln_b, w_ada, b_ada, w_in, b_gate, q_norm_g, k_norm_g, w_pa, w_pb, w_o):
    depth = w_in.shape[0]
    b, s, d = x.shape
    alpha = float((2 * depth) ** 0.25)

    c_pad = jnp.zeros((SUBLANES, d), F32).at[:b].set(c)
    mod = _ada_call(c_pad, w_ada, b_ada)[:, :b]
    cos_t, sin_t = _rope_tables(s)
    idx = jnp.arange(_BD_W) // HEAD_DIM
    bd = (idx[:, None] == idx[None, :]).astype(BF16)
    biases = [_window_bias(rel_table, g) for g in range(len(A_GROUPS))]
    w_all = w_in.astype(BF16)

    for l in range(depth):
        shift = mod[l, :, 0:d].reshape(b, 1, d)
        scale = mod[l, :, d:2 * d].reshape(b, 1, d)
        gate = mod[l, :, 2 * d:3 * d].reshape(b, 1, d)
        gqk = jnp.concatenate([jnp.tile(q_norm_g[l], B_Q_HEADS),
                               jnp.tile(_pad_heads(k_norm_g[l]), B_KV_HEADS)]).reshape(1, _QK_W)
        a0, a1, a2, bqt, qn2, bk, bvt, gz, st = _inproj_call(
            x, shift, scale, w_all, l, _padded_kv_weights(w_in[l]), b_gate[l].reshape(1, 2 * d), gqk,
            cos_t, sin_t, bd)
        a0 = a0.reshape(b, 1, s, _A_SEG)
        oa = [_attn_a(a_g, biases[g]) for g, a_g in enumerate((a0, a1, a2))]
        kmax = jnp.sqrt(jnp.max(st[:, :, 0:B_KV_HEADS, 0], axis=1))
        qmax = jnp.sqrt(jnp.max(st[:, :, B_KV_HEADS, 0]))
        bound_ok = qmax * jnp.max(kmax) * B_BOUND_SLACK <= B_MAX_BOUND
        ybt = lax.cond(
            bound_ok,
            functools.partial(_attn_b_call, online=False),
            functools.partial(_attn_b_call, online=True),
            kmax.reshape(-1), bqt, qn2, bk, bvt)
        x = _outproj_call(x, gate, oa, ybt, gz, w_pa[l].astype(BF16), w_pb[l].astype(BF16),
                          w_o[l].astype(BF16), ln_g[l].reshape(1, d), ln_b[l].reshape(1, d), alpha)
    return x
```

```python
import functools
import math

import jax
import jax.numpy as jnp
from jax import lax
from jax.experimental import pallas as pl
from jax.experimental.pallas import tpu as pltpu

HEAD_DIM = 64
A_GROUPS = ((128, 1), (512, 4), (2048, 16))
A_HEADS_PER_GROUP = 8
A_WIDTH = A_HEADS_PER_GROUP * HEAD_DIM
A_RADIUS = 64
B_Q_HEADS = 8
B_KV_HEADS = 2
B_GROUP = B_Q_HEADS // B_KV_HEADS
B_WIDTH = B_Q_HEADS * HEAD_DIM
GRID_W = 64
ROPE_THETA = 10000.0
REL_BUCKETS = 32
REL_MAX_DISTANCE = 1024
LN_EPS = 1e-5
QK_EPS = 1e-6
NEG_INF = -1e30
LOG2_E = math.log2(math.e)
_Q_SCALE = HEAD_DIM ** -0.5 * LOG2_E

LANES = 128
SUBLANES = 8
VMEM_LIMIT_BYTES = 56 * 1024 * 1024

ADA_TN = 1024
IN_TM = 256
A_TQ = 2048
A_SUB = 128
B_TQ = 2048
B_TK = 2048
B_TK_ONLINE = 512
OUT_TM = 512
OUT_SPLIT = 1

B_MAX_BOUND = 50.0
A_MAX_ABS_LOG2_DENOM = 90.0
B_BOUND_SLACK = 1.0 + 2.0 ** -6

BF16 = jnp.bfloat16
F32 = jnp.float32

_NT = (((1,), (1,)), ((), ()))


def _ada_kernel(c_ref, w_ref, b_ref, o_ref):
    c = c_ref[...]
    h = c * jax.nn.sigmoid(c)
    o_ref[0] = jnp.dot(h, w_ref[0], preferred_element_type=F32,
                       precision=lax.Precision.HIGHEST) + b_ref[0]


def _ada_call(c_pad, w_ada, b_ada):
    depth, d, n3 = w_ada.shape
    rows = c_pad.shape[0]
    return pl.pallas_call(
        _ada_kernel,
        out_shape=jax.ShapeDtypeStruct((depth, rows, n3), F32),
        grid=(depth, n3 // ADA_TN),
        in_specs=[
            pl.BlockSpec((rows, d), lambda l, j: (0, 0)),
            pl.BlockSpec((1, d, ADA_TN), lambda l, j: (l, 0, j)),
            pl.BlockSpec((1, 1, ADA_TN), lambda l, j: (l, 0, j)),
        ],
        out_specs=pl.BlockSpec((1, rows, ADA_TN), lambda l, j: (l, 0, j)),
        compiler_params=pltpu.CompilerParams(
            dimension_semantics=("arbitrary", "arbitrary")),
        name="ada_mod",
    )(c_pad, w_ada, b_ada.reshape(depth, 1, n3))


_A_SEG = 3 * A_WIDTH
_A_PART = len(A_GROUPS) * A_WIDTH
_OFF_AZ = 3 * _A_PART
_OFF_BQ = _OFF_AZ + A_WIDTH
_KV_PAD = B_KV_HEADS * LANES
_OFF_BZ = _OFF_BQ + B_WIDTH + 2 * B_KV_HEADS * HEAD_DIM
_OFF_GL = _OFF_BZ + B_WIDTH
_QK_W = B_WIDTH + _KV_PAD
_BD_W = 2 * LANES
_ONE_LANE = HEAD_DIM


def _inproj_kernel(x_ref, shift_ref, scale_ref, w_ref, wkv_ref, bgate_ref, gqk_ref,
                   cos_ref, sin_ref, bd_ref,
                   a0_ref, a1_ref, a2_ref, bqt_ref, qn_ref, bk_ref, bvt_ref, gz_ref, st_ref,
                   scr_ref):
    tm = x_ref.shape[1]
    d_model = x_ref.shape[2]
    u = (x_ref[0] * (1.0 + scale_ref[0]) + shift_ref[0]).astype(BF16)

    def proj(off, width):
        return jnp.dot(u, w_ref[:, off:off + width], preferred_element_type=F32)

    kv_lane = lax.broadcasted_iota(jnp.int32, (1, _KV_PAD), 1)
    one_col = ((kv_lane & (LANES - 1)) == _ONE_LANE).astype(F32)

    qk = jnp.concatenate(
        [proj(_OFF_BQ, B_WIDTH),
         jnp.dot(u, wkv_ref[:, 0:_KV_PAD], preferred_element_type=F32)], axis=1)
    sq = qk * qk
    sq_hi = sq.astype(BF16)
    sq_lo = (sq - sq_hi.astype(F32)).astype(BF16)
    def head_sums(part):
        return jnp.concatenate(
            [jnp.dot(part[:, c:c + _BD_W], bd_ref[...], preferred_element_type=F32)
             for c in range(0, _QK_W, _BD_W)], axis=1)
    ss = head_sums(sq_hi) + head_sums(sq_lo)
    y = qk * lax.rsqrt(ss * (1.0 / HEAD_DIM) + QK_EPS) * gqk_ref[...]
    reps = _QK_W // LANES
    cos = jnp.concatenate([cos_ref[...]] * reps, axis=1)
    sin = jnp.concatenate([sin_ref[...]] * reps, axis=1)
    lane = lax.broadcasted_iota(jnp.int32, y.shape, 1)
    first = (lane & 31) < 16
    swapped = jnp.where(first, pltpu.roll(y, _QK_W - 16, 1), pltpu.roll(y, 16, 1))
    rot = y * cos + swapped * sin
    q = rot[:, :B_WIDTH] * _Q_SCALE
    k = rot[:, B_WIDTH:]
    bk_ref[0] = (k + one_col).astype(BF16)
    q_t = q.T
    bqt_ref[0] = q_t.astype(BF16)
    qn2 = jnp.concatenate(
        [jnp.sum(jnp.square(q_t[h * HEAD_DIM:(h + 1) * HEAD_DIM]), axis=0, keepdims=True)
         for h in range(B_Q_HEADS)], axis=0)
    qn_ref[0] = qn2

    def tile_max(v2):
        return jnp.max(jnp.sum(v2, axis=1, keepdims=True), axis=0, keepdims=True)
    k2 = k * k
    kmax = [tile_max(k2[:, h * LANES:(h + 1) * LANES]) for h in range(B_KV_HEADS)]
    qmax = jnp.max(jnp.max(qn2, axis=1, keepdims=True), axis=0, keepdims=True)
    sub = lax.broadcasted_iota(jnp.int32, (SUBLANES, LANES), 0)
    st_ref[0, 0] = jnp.where(sub == 0, kmax[0], jnp.where(sub == 1, kmax[1], qmax))

    def proj_a(g):
        chunks = []
        for part in range(3):
            res = proj(part * _A_PART + g * A_WIDTH, A_WIDTH)
            if part == 0:
                res = res * _Q_SCALE
            chunks += [res[:, ch * LANES:(ch + 1) * LANES] for ch in range(A_WIDTH // LANES)]
        return chunks

    for g, a_ref in ((1, a1_ref), (2, a2_ref)):
        dil = A_GROUPS[g][1]
        for ch, chunk in enumerate(proj_a(g)):
            scr_ref[ch] = chunk
        for r in range(dil):
            for ch in range(_A_SEG // LANES):
                a_ref[0, r, :, ch * LANES:(ch + 1) * LANES] = (
                    scr_ref[ch, pl.ds(r, tm // dil, stride=dil), :].astype(BF16))

    az = proj(_OFF_AZ, A_WIDTH)
    gz_ref[0, :, 0:A_WIDTH] = (az * jax.nn.sigmoid(az)).astype(BF16)
    bz = proj(_OFF_BZ, B_WIDTH)
    gz_ref[0, :, A_WIDTH:A_WIDTH + B_WIDTH] = (bz * jax.nn.sigmoid(bz)).astype(BF16)
    for j in range(2):
        gl = proj(_OFF_GL + j * d_model, d_model) + bgate_ref[:, j * d_model:(j + 1) * d_model]
        o = A_WIDTH + B_WIDTH + j * d_model
        gz_ref[0, :, o:o + d_model] = jax.nn.sigmoid(gl).astype(BF16)

    a0_ref[0] = jnp.concatenate(proj_a(0), axis=1).astype(BF16)
    v_pad = jnp.dot(u, wkv_ref[:, _KV_PAD:2 * _KV_PAD], preferred_element_type=F32)
    bvt_ref[0] = (v_pad + one_col).T.astype(BF16)


def _inproj_call(x, shift, scale, w_all, layer, w_kv, b_gate, gqk, cos_t, sin_t, bd):
    b, s, d = x.shape
    tm = IN_TM
    n_cols = w_all.shape[2]
    d1, d2 = A_GROUPS[1][1], A_GROUPS[2][1]
    const = dict(pipeline_mode=pl.Buffered(1))
    out_shape = (
        jax.ShapeDtypeStruct((b, s, _A_SEG), BF16),
        jax.ShapeDtypeStruct((b, d1, s // d1, _A_SEG), BF16),
        jax.ShapeDtypeStruct((b, d2, s // d2, _A_SEG), BF16),
        jax.ShapeDtypeStruct((b, B_WIDTH, s), BF16),
        jax.ShapeDtypeStruct((b, B_Q_HEADS, s), F32),
        jax.ShapeDtypeStruct((b, s, _KV_PAD), BF16),
        jax.ShapeDtypeStruct((b, _KV_PAD, s), BF16),
        jax.ShapeDtypeStruct((b, s, A_WIDTH + B_WIDTH + 2 * d), BF16),
        jax.ShapeDtypeStruct((b, s // tm, SUBLANES, LANES), F32),
    )
    row = lambda bi, i: (bi, i, 0)
    out_specs = (
        pl.BlockSpec((1, tm, _A_SEG), row),
        pl.BlockSpec((1, d1, tm // d1, _A_SEG), lambda bi, i: (bi, 0, i, 0)),
        pl.BlockSpec((1, d2, tm // d2, _A_SEG), lambda bi, i: (bi, 0, i, 0)),
        pl.BlockSpec((1, B_WIDTH, tm), lambda bi, i: (bi, 0, i)),
        pl.BlockSpec((1, B_Q_HEADS, tm), lambda bi, i: (bi, 0, i)),
        pl.BlockSpec((1, tm, _KV_PAD), row),
        pl.BlockSpec((1, _KV_PAD, tm), lambda bi, i: (bi, 0, i)),
        pl.BlockSpec((1, tm, A_WIDTH + B_WIDTH + 2 * d), row),
        pl.BlockSpec((1, 1, SUBLANES, LANES), lambda bi, i: (bi, i, 0, 0)),
    )
    in_specs = [
        pl.BlockSpec((1, tm, d), row),
        pl.BlockSpec((1, 1, d), lambda bi, i: (bi, 0, 0)),
        pl.BlockSpec((1, 1, d), lambda bi, i: (bi, 0, 0)),
        pl.BlockSpec((None, d, n_cols), lambda bi, i: (layer, 0, 0), **const),
        pl.BlockSpec((d, 2 * _KV_PAD), lambda bi, i: (0, 0), **const),
        pl.BlockSpec((1, 2 * d), lambda bi, i: (0, 0), **const),
        pl.BlockSpec((1, _QK_W), lambda bi, i: (0, 0), **const),
        pl.BlockSpec((tm, LANES), lambda bi, i: (i, 0)),
        pl.BlockSpec((tm, LANES), lambda bi, i: (i, 0)),
        pl.BlockSpec((_BD_W, _BD_W), lambda bi, i: (0, 0), **const),
    ]
    return pl.pallas_call(
        _inproj_kernel,
        out_shape=out_shape,
        grid=(b, s // tm),
        in_specs=in_specs,
        out_specs=out_specs,
        scratch_shapes=[pltpu.VMEM((_A_SEG // LANES, tm, LANES), F32)],
        compiler_params=pltpu.CompilerParams(
            dimension_semantics=("arbitrary", "arbitrary"),
            vmem_limit_bytes=VMEM_LIMIT_BYTES),
        name="inproj",
    )(x, shift, scale, w_all, w_kv, b_gate, gqk, cos_t, sin_t, bd)


def _attn_a_kernel(q_ref, kp_ref, kc_ref, kn_ref, vp_ref, vc_ref, vn_ref, bias_ref,
                   o_ref, lse_ref, range_ref, *, phase_len, stabilise):
    n_ph, tq = q_ref.shape[1], q_ref.shape[2]
    i = pl.program_id(2)
    tk = A_SUB + 2 * A_RADIUS
    n_sub = tq // A_SUB
    lane = lax.broadcasted_iota(jnp.int32, (A_SUB, LANES), 1)
    low = lane < HEAD_DIM
    kcol = lax.broadcasted_iota(jnp.int32, (1, tk), 1)
    worst = jnp.zeros((A_SUB, LANES), F32)
    for ph, sub in [(ph, sub) for ph in range(n_ph) for sub in range(n_sub)]:
        if sub == 0:
            q = q_ref[0, ph]
            k_all = jnp.concatenate([kp_ref[0, ph], kc_ref[0, ph], kn_ref[0, ph]], axis=0)
            v_all = jnp.concatenate([vp_ref[0, ph], vc_ref[0, ph], vn_ref[0, ph]], axis=0)
        r0 = sub * A_SUB
        at_edge = sub == 0 or sub == n_sub - 1
        if at_edge:
            krow = i * tq + (r0 - A_RADIUS) + kcol
            valid = (krow >= 0) & (krow < phase_len)
        for pair in range(A_HEADS_PER_GROUP // 2):
            c0 = pair * LANES
            qp = q[r0:r0 + A_SUB, c0:c0 + LANES]
            kp = k_all[r0:r0 + tk, c0:c0 + LANES]
            vp = v_all[r0:r0 + tk, c0:c0 + LANES]
            zero = jnp.zeros_like(qp)
            qs = jnp.concatenate([jnp.where(low, qp, zero), jnp.where(low, zero, qp)], axis=0)
            s = lax.dot_general(qs, kp, _NT, preferred_element_type=F32) + bias_ref[pair]
            if at_edge:
                s = jnp.where(valid, s, NEG_INF)
            if stabilise:
                m = jnp.max(s, axis=1, keepdims=True)
                s = s - m
            p = jnp.exp2(s)
            l = jnp.sum(p, axis=1, keepdims=True)
            o = jnp.dot(p.astype(BF16), vp, preferred_element_type=F32)
            pick = lambda a: jnp.where(low, a[:A_SUB], a[A_SUB:])
            l = pick(l)
            lse = jnp.log2(l)
            worst = jnp.maximum(worst, jnp.abs(lse))
            if stabilise:
                lse = lse + pick(m)
            o_ref[0, ph, r0:r0 + A_SUB, c0:c0 + LANES] = (pick(o) * (1.0 / l)).astype(BF16)
            lse_ref[0, ph, r0:r0 + A_SUB, c0:c0 + LANES] = lse
    range_ref[0, 0, 0] = jnp.broadcast_to(
        jnp.max(jnp.max(worst, axis=1, keepdims=True), axis=0, keepdims=True), (SUBLANES, LANES))


def _attn_a_call(a_g, bias_g, *, stabilise):
    b, dil, phase_len, _ = a_g.shape
    tq = min(A_TQ, phase_len)
    n_steps = phase_len // tq
    n_ph = min(dil, A_TQ // tq)
    halo = A_RADIUS
    nh = phase_len // halo
    per = tq // halo
    cur = lambda col: (lambda bi, r, i: (bi, r, i, col))
    prev = lambda col: (lambda bi, r, i: (bi, r, jnp.maximum(i * per - 1, 0), col))
    nxt = lambda col: (lambda bi, r, i: (bi, r, jnp.minimum((i + 1) * per, nh - 1), col))
    blk = (1, n_ph, tq, A_WIDTH)
    hblk = (1, n_ph, halo, A_WIDTH)
    o_sds = jax.ShapeDtypeStruct((b, dil, phase_len, A_WIDTH), BF16)
    lse_sds = jax.ShapeDtypeStruct((b, dil, phase_len, A_WIDTH), F32)
    range_sds = jax.ShapeDtypeStruct((b, dil // n_ph, n_steps, SUBLANES, LANES), F32)
    return pl.pallas_call(
        functools.partial(_attn_a_kernel, phase_len=phase_len, stabilise=stabilise),
        out_shape=(o_sds, lse_sds, range_sds),
        grid=(b, dil // n_ph, n_steps),
        in_specs=[
            pl.BlockSpec(blk, cur(0)),
            pl.BlockSpec(hblk, prev(1)), pl.BlockSpec(blk, cur(1)), pl.BlockSpec(hblk, nxt(1)),
            pl.BlockSpec(hblk, prev(2)), pl.BlockSpec(blk, cur(2)), pl.BlockSpec(hblk, nxt(2)),
            pl.BlockSpec(bias_g.shape, lambda bi, r, i: (0, 0, 0)),
        ],
        out_specs=(pl.BlockSpec(blk, cur(0)), pl.BlockSpec(blk, cur(0)),
                   pl.BlockSpec((1, 1, 1, SUBLANES, LANES), lambda bi, r, i: (bi, r, i, 0, 0))),
        compiler_params=pltpu.CompilerParams(
            dimension_semantics=("arbitrary", "arbitrary", "arbitrary"),
            vmem_limit_bytes=VMEM_LIMIT_BYTES),
        name=f"attn_a_d{dil}" + ("_stab" if stabilise else ""),
    )(a_g, a_g, a_g, a_g, a_g, a_g, a_g, bias_g)


def _attn_a(a_g, bias_g):
    o, lse, worst = _attn_a_call(a_g, bias_g, stabilise=False)
    in_range = jnp.max(worst) <= A_MAX_ABS_LOG2_DENOM
    return lax.cond(in_range,
                    lambda: (o, lse),
                    lambda: _attn_a_call(a_g, bias_g, stabilise=True)[:2])


_VT_ROWS = 80


def _attn_b_kernel(kmax_ref, qt_ref, qn_ref, k_ref, vt_ref, o_ref, qst_ref, acc_ref, *m_scratch,
                   online):
    tq = qt_ref.shape[2]
    seq = k_ref.shape[1]
    kmax = kmax_ref[pl.program_id(0) * B_KV_HEADS + pl.program_id(1)]
    tail_row = lax.broadcasted_iota(jnp.int32, (LANES - HEAD_DIM, tq), 0)
    for h in range(B_GROUP):
        cols = slice(h * tq, (h + 1) * tq)
        qst_ref[0:HEAD_DIM, cols] = qt_ref[0, h * HEAD_DIM:(h + 1) * HEAD_DIM, :]
        if online:
            tail = jnp.zeros(tail_row.shape, F32)
        else:
            bound = jnp.sqrt(qn_ref[0, 0, h:h + 1, :]) * (kmax * B_BOUND_SLACK)
            tail = jnp.where(tail_row == _ONE_LANE - HEAD_DIM, -bound, 0.0)
        qst_ref[HEAD_DIM:LANES, cols] = tail.astype(BF16)
    acc_ref[...] = jnp.zeros(acc_ref.shape, F32)
    if online:
        m_ref, = m_scratch
        m_ref[...] = jnp.full(m_ref.shape, NEG_INF, F32)

    tk = B_TK_ONLINE if online else B_TK
    n_chunks = seq // tk

    def scores_t(c):
        start = pl.multiple_of(c * tk, tk)
        return jnp.dot(k_ref[0, pl.ds(start, tk), :], qst_ref[...],
                       preferred_element_type=F32)

    def values_t(c):
        start = pl.multiple_of(c * tk, tk)
        return vt_ref[0, 0:_VT_ROWS, pl.ds(start, tk)]

    if online:
        def body(c, carry):
            s_t = scores_t(c)
            m_prev = m_ref[...]
            m_new = jnp.maximum(m_prev, jnp.max(s_t, axis=0, keepdims=True))
            p_t = jnp.exp2(s_t - m_new).astype(BF16)
            acc_ref[...] = (jnp.exp2(m_prev - m_new) * acc_ref[...]
                            + jnp.dot(values_t(c), p_t, preferred_element_type=F32))
            m_ref[...] = m_new
            return carry

        lax.fori_loop(0, n_chunks, body, 0)
    else:
        def body(c, carry):
            start = pl.multiple_of(c * tk, tk)
            kc = k_ref[0, pl.ds(start, tk), :]
            vtc = values_t(c)
            for h in range(B_GROUP):
                cols = slice(h * tq, (h + 1) * tq)
                p_t = jnp.exp2(jnp.dot(kc, qst_ref[:, cols],
                                       preferred_element_type=F32)).astype(BF16)
                acc_ref[:, cols] += jnp.dot(vtc, p_t, preferred_element_type=F32)
            return carry

        lax.fori_loop(0, n_chunks, body, 0)
    acc = acc_ref[...]
    o_t = acc[0:HEAD_DIM, :] * (1.0 / acc[_ONE_LANE:_ONE_LANE + 1, :])
    for h in range(B_GROUP):
        o_ref[0, h * HEAD_DIM:(h + 1) * HEAD_DIM, :] = o_t[:, h * tq:(h + 1) * tq].astype(BF16)


def _attn_b_call(kmax, bqt, qn2, bk, bvt, *, online):
    b, _, s = bqt.shape
    qw = B_GROUP * HEAD_DIM
    cols = B_GROUP * B_TQ
    scratch = [pltpu.VMEM((LANES, cols), BF16), pltpu.VMEM((_VT_ROWS, cols), F32)]
    if online:
        scratch.append(pltpu.VMEM((1, cols), F32))
    return pl.pallas_call(
        functools.partial(_attn_b_kernel, online=online),
        out_shape=jax.ShapeDtypeStruct((b, B_WIDTH, s), BF16),
        grid_spec=pltpu.PrefetchScalarGridSpec(
            num_scalar_prefetch=1,
            grid=(b, B_KV_HEADS, s // B_TQ),
            in_specs=[
                pl.BlockSpec((1, qw, B_TQ), lambda bi, h, i, km: (bi, h, i)),
                pl.BlockSpec((1, 1, B_GROUP, B_TQ), lambda bi, h, i, km: (bi, h, 0, i)),
                pl.BlockSpec((1, s, LANES), lambda bi, h, i, km: (bi, 0, h)),
                pl.BlockSpec((1, LANES, s), lambda bi, h, i, km: (bi, h, 0)),
            ],
            out_specs=pl.BlockSpec((1, qw, B_TQ), lambda bi, h, i, km: (bi, h, i)),
            scratch_shapes=scratch,
        ),
        compiler_params=pltpu.CompilerParams(
            dimension_semantics=("arbitrary", "arbitrary", "arbitrary"),
            vmem_limit_bytes=VMEM_LIMIT_BYTES),
        name="attn_b_online" if online else "attn_b",
    )(kmax, bqt, qn2.reshape(b, B_KV_HEADS, B_GROUP, s), bk, bvt)


def _outproj_kernel(x_ref, gate_ref, o0_ref, l0_ref, o1_ref, l1_ref, o2_ref, l2_ref,
                    ybt_ref, gz_ref, wpa_ref, wpb_ref, wo_ref, lng_ref, lnb_ref,
                    out_ref, so1, sl1, so2, sl2, *, alpha):
    tm = x_ref.shape[1]
    d_model = x_ref.shape[2]
    for (o_ref, l_ref, so, sl, g) in ((o1_ref, l1_ref, so1, sl1, 1), (o2_ref, l2_ref, so2, sl2, 2)):
        dil = A_GROUPS[g][1]
        for r in range(dil):
            for ch in range(A_WIDTH // LANES):
                cs = slice(ch * LANES, (ch + 1) * LANES)
                so[ch, pl.ds(r, tm // dil, stride=dil), :] = o_ref[0, r, :, cs].astype(F32)
                sl[ch, pl.ds(r, tm // dil, stride=dil), :] = l_ref[0, r, :, cs]
    rows = tm // OUT_SPLIT
    for part in range(OUT_SPLIT):
        rs = slice(part * rows, (part + 1) * rows)
        cat = lambda ref: jnp.concatenate([ref[ch, rs, :] for ch in range(A_WIDTH // LANES)], axis=1)
        l0, l1, l2 = l0_ref[0, rs, :], cat(sl1), cat(sl2)
        mx = jnp.maximum(jnp.maximum(l0, l1), l2)
        e0, e1, e2 = jnp.exp2(l0 - mx), jnp.exp2(l1 - mx), jnp.exp2(l2 - mx)
        y_a = ((e0 * o0_ref[0, rs, :].astype(F32) + e1 * cat(so1) + e2 * cat(so2))
               * (1.0 / (e0 + e1 + e2)))
        gz = gz_ref[0, rs, :]
        ya = (y_a * gz[:, 0:A_WIDTH].astype(F32)).astype(BF16)
        y_b = ybt_ref[0, :, rs].astype(F32).T
        yb = (y_b * gz[:, A_WIDTH:A_WIDTH + B_WIDTH].astype(F32)).astype(BF16)
        pa = jnp.dot(ya, wpa_ref[...], preferred_element_type=F32)
        pb = jnp.dot(yb, wpb_ref[...], preferred_element_type=F32)
        o = A_WIDTH + B_WIDTH
        g_a = gz[:, o:o + d_model].astype(F32)
        g_b = gz[:, o + d_model:o + 2 * d_model].astype(F32)
        merged = (g_a * pa + g_b * pb).astype(BF16)
        out = jnp.dot(merged, wo_ref[...], preferred_element_type=F32)
        h = alpha * x_ref[0, rs, :] + gate_ref[0] * out
        mu = jnp.mean(h, axis=-1, keepdims=True)
        hc = h - mu
        var = jnp.mean(hc * hc, axis=-1, keepdims=True)
        out_ref[0, rs, :] = hc * lax.rsqrt(var + LN_EPS) * lng_ref[...] + lnb_ref[...]


def _outproj_call(x, gate, oa, ybt, gz, w_pa, w_pb, w_o, ln_g, ln_b, alpha):
    b, s, d = x.shape
    tm = OUT_TM
    (o0, l0), (o1, l1), (o2, l2) = oa
    d1, d2 = A_GROUPS[1][1], A_GROUPS[2][1]
    row = lambda bi, i: (bi, i, 0)
    ph = lambda bi, i: (bi, 0, i, 0)
    fixed = lambda bi, i: (0, 0)
    s0 = pl.BlockSpec((1, 1, tm, A_WIDTH), ph)
    s1 = pl.BlockSpec((1, d1, tm // d1, A_WIDTH), ph)
    s2 = pl.BlockSpec((1, d2, tm // d2, A_WIDTH), ph)

    def kern(x_ref, gate_ref, o0_ref, l0_ref, *rest):
        return _outproj_kernel(x_ref, gate_ref, o0_ref.at[0], l0_ref.at[0], *rest, alpha=alpha)

    return pl.pallas_call(
        kern,
        out_shape=jax.ShapeDtypeStruct((b, s, d), F32),
        grid=(b, s // tm),
        in_specs=[
            pl.BlockSpec((1, tm, d), row),
            pl.BlockSpec((1, 1, d), lambda bi, i: (bi, 0, 0)),
            s0, s0, s1, s1, s2, s2,
            pl.BlockSpec((1, B_WIDTH, tm), lambda bi, i: (bi, 0, i)),
            pl.BlockSpec((1, tm, gz.shape[2]), row),
            pl.BlockSpec(w_pa.shape, fixed),
            pl.BlockSpec(w_pb.shape, fixed),
            pl.BlockSpec(w_o.shape, fixed),
            pl.BlockSpec((1, d), fixed),
            pl.BlockSpec((1, d), fixed),
        ],
        out_specs=pl.BlockSpec((1, tm, d), row),
        scratch_shapes=[pltpu.VMEM((A_WIDTH // LANES, tm, LANES), F32)] * 4,
        compiler_params=pltpu.CompilerParams(
            dimension_semantics=("arbitrary", "arbitrary"),
            vmem_limit_bytes=VMEM_LIMIT_BYTES),
        name="outproj",
    )(x, gate, o0, l0, o1, l1, o2, l2, ybt, gz, w_pa, w_pb, w_o, ln_g, ln_b)


def _t5_bucket(rel):
    half = REL_BUCKETS // 2
    max_exact = half // 2
    ret = jnp.where(rel > 0, half, 0)
    a = jnp.abs(rel)
    af = jnp.maximum(a, 1).astype(F32)
    large = max_exact + (jnp.log(af / max_exact) / math.log(REL_MAX_DISTANCE / max_exact)
                         * (half - max_exact)).astype(jnp.int32)
    large = jnp.minimum(large, half - 1)
    return ret + jnp.where(a < max_exact, a, large)


def _window_bias(rel_table, g):
    dil = A_GROUPS[g][1]
    tk = A_SUB + 2 * A_RADIUS
    rel = jnp.arange(tk)[None, :] - A_RADIUS - jnp.arange(A_SUB)[:, None]
    table_g = rel_table[:, g * A_HEADS_PER_GROUP:(g + 1) * A_HEADS_PER_GROUP]
    onehot = (_t5_bucket(rel * dil)[..., None] == jnp.arange(REL_BUCKETS)).astype(F32)
    bias = jnp.einsum("qkb,bh->hqk", onehot, table_g.astype(F32), precision=lax.Precision.HIGHEST)
    bias = jnp.where((jnp.abs(rel) <= A_RADIUS)[None], bias * LOG2_E, NEG_INF)
    return bias.reshape(A_HEADS_PER_GROUP // 2, 2 * A_SUB, tk)


def _rope_tables(seq):
    t = jnp.arange(seq)
    row = (t // GRID_W).astype(F32)
    col = (t % GRID_W).astype(F32)
    half = HEAD_DIM // 2
    inv = ROPE_THETA ** (-jnp.arange(0, half, 2, dtype=F32) / half)
    ar, ac = row[:, None] * inv[None], col[:, None] * inv[None]
    cos = jnp.concatenate([jnp.cos(ar), jnp.cos(ar), jnp.cos(ac), jnp.cos(ac)], axis=1)
    sin = jnp.concatenate([-jnp.sin(ar), jnp.sin(ar), -jnp.sin(ac), jnp.sin(ac)], axis=1)
    return jnp.tile(cos, (1, LANES // HEAD_DIM)), jnp.tile(sin, (1, LANES // HEAD_DIM))


def _padded_kv_weights(w):
    d = w.shape[0]
    off = _OFF_BQ + B_WIDTH
    zeros = jnp.zeros((d, LANES - HEAD_DIM), w.dtype)
    blocks = []
    for h in range(2 * B_KV_HEADS):
        blocks += [w[:, off + h * HEAD_DIM:off + (h + 1) * HEAD_DIM], zeros]
    return jnp.concatenate(blocks, axis=1).astype(BF16)


def _pad_heads(g):
    return jnp.concatenate([g, jnp.zeros((LANES - HEAD_DIM,), g.dtype)])


def kernel(x, c, rel_table, ln_g, ln_b, w_ada, b_ada, w_in, b_gate, q_norm_g, k_norm_g, w_pa, w_pb, w_o):
    depth = w_in.shape[0]
    b, s, d = x.shape
    alpha = float((2 * depth) ** 0.25)

    c_pad = jnp.zeros((SUBLANES, d), F32).at[:b].set(c)
    mod = _ada_call(c_pad, w_ada, b_ada)[:, :b]
    cos_t, sin_t = _rope_tables(s)
    idx = jnp.arange(_BD_W) // HEAD_DIM
    bd = (idx[:, None] == idx[None, :]).astype(BF16)
    biases = [_window_bias(rel_table, g) for g in range(len(A_GROUPS))]
    w_all = w_in.astype(BF16)

    for l in range(depth):
        shift = mod[l, :, 0:d].reshape(b, 1, d)
        scale = mod[l, :, d:2 * d].reshape(b, 1, d)
        gate = mod[l, :, 2 * d:3 * d].reshape(b, 1, d)
        gqk = jnp.concatenate([jnp.tile(q_norm_g[l], B_Q_HEADS),
                               jnp.tile(_pad_heads(k_norm_g[l]), B_KV_HEADS)]).reshape(1, _QK_W)
        a0, a1, a2, bqt, qn2, bk, bvt, gz, st = _inproj_call(
            x, shift, scale, w_all, l, _padded_kv_weights(w_in[l]), b_gate[l].reshape(1, 2 * d), gqk,
            cos_t, sin_t, bd)
        a0 = a0.reshape(b, 1, s, _A_SEG)
        oa = [_attn_a(a_g, biases[g]) for g, a_g in enumerate((a0, a1, a2))]
        kmax = jnp.sqrt(jnp.max(st[:, :, 0:B_KV_HEADS, 0], axis=1))
        qmax = jnp.sqrt(jnp.max(st[:, :, B_KV_HEADS, 0]))
        bound_ok = qmax * jnp.max(kmax) * B_BOUND_SLACK <= B_MAX_BOUND
        ybt = lax.cond(
            bound_ok,
            functools.partial(_attn_b_call, online=False),
            functools.partial(_attn_b_call, online=True),
            kmax.reshape(-1), bqt, qn2, bk, bvt)
        x = _outproj_call(x, gate, oa, ybt, gz, w_pa[l].astype(BF16), w_pb[l].astype(BF16),
                          w_o[l].astype(BF16), ln_g[l].reshape(1, d), ln_b[l].reshape(1, d), alpha)
    return x
```

```python
import functools
import math

import jax
import jax.numpy as jnp
from jax import lax
from jax.experimental import pallas as pl
from jax.experimental.pallas import tpu as pltpu

HEAD_DIM = 64
A_GROUPS = ((128, 1), (512, 4), (2048, 16))
A_HEADS_PER_GROUP = 8
A_WIDTH = A_HEADS_PER_GROUP * HEAD_DIM
A_RADIUS = 64
B_Q_HEADS = 8
B_KV_HEADS = 2
B_GROUP = B_Q_HEADS // B_KV_HEADS
B_WIDTH = B_Q_HEADS * HEAD_DIM
GRID_W = 64
ROPE_THETA = 10000.0
REL_BUCKETS = 32
REL_MAX_DISTANCE = 1024
LN_EPS = 1e-5
QK_EPS = 1e-6
NEG_INF = -1e30
LOG2_E = math.log2(math.e)
_Q_SCALE = HEAD_DIM ** -0.5 * LOG2_E

LANES = 128
SUBLANES = 8
VMEM_LIMIT_BYTES = 56 * 1024 * 1024

ADA_TN = 1024
IN_TM = 256
A_TQ = 2048
A_SUB = 128
B_TQ = 2048
B_TK = 2048
B_TK_ONLINE = 512
OUT_TM = 512

B_MAX_BOUND = 50.0
A_MAX_ABS_LOG2_DENOM = 90.0
B_BOUND_SLACK = 1.0 + 2.0 ** -6

BF16 = jnp.bfloat16
F32 = jnp.float32

_NT = (((1,), (1,)), ((), ()))


def _ada_kernel(c_ref, w_ref, b_ref, o_ref):
    c = c_ref[...]
    h = c * jax.nn.sigmoid(c)
    o_ref[0] = jnp.dot(h, w_ref[0], preferred_element_type=F32,
                       precision=lax.Precision.HIGHEST) + b_ref[0]


def _ada_call(c_pad, w_ada, b_ada):
    depth, d, n3 = w_ada.shape
    rows = c_pad.shape[0]
    return pl.pallas_call(
        _ada_kernel,
        out_shape=jax.ShapeDtypeStruct((depth, rows, n3), F32),
        grid=(depth, n3 // ADA_TN),
        in_specs=[
            pl.BlockSpec((rows, d), lambda l, j: (0, 0)),
            pl.BlockSpec((1, d, ADA_TN), lambda l, j: (l, 0, j)),
            pl.BlockSpec((1, 1, ADA_TN), lambda l, j: (l, 0, j)),
        ],
        out_specs=pl.BlockSpec((1, rows, ADA_TN), lambda l, j: (l, 0, j)),
        compiler_params=pltpu.CompilerParams(
            dimension_semantics=("arbitrary", "arbitrary")),
        name="ada_mod",
    )(c_pad, w_ada, b_ada.reshape(depth, 1, n3))


_A_SEG = 3 * A_WIDTH
_A_PART = len(A_GROUPS) * A_WIDTH
_OFF_AZ = 3 * _A_PART
_OFF_BQ = _OFF_AZ + A_WIDTH
_KV_PAD = B_KV_HEADS * LANES
_OFF_BZ = _OFF_BQ + B_WIDTH + 2 * B_KV_HEADS * HEAD_DIM
_OFF_GL = _OFF_BZ + B_WIDTH
_QK_W = B_WIDTH + _KV_PAD
_BD_W = 2 * LANES
_ONE_LANE = HEAD_DIM


def _inproj_kernel(x_ref, shift_ref, scale_ref, w_ref, wkv_ref, bgate_ref, gqk_ref,
                   cos_ref, sin_ref, bd_ref,
                   a0_ref, a1_ref, a2_ref, bqt_ref, qn_ref, bk_ref, bvt_ref, gz_ref, st_ref,
                   scr_ref):
    tm = x_ref.shape[1]
    d_model = x_ref.shape[2]
    u = (x_ref[0] * (1.0 + scale_ref[0]) + shift_ref[0]).astype(BF16)

    def proj(off, width):
        return jnp.dot(u, w_ref[:, off:off + width], preferred_element_type=F32)

    kv_lane = lax.broadcasted_iota(jnp.int32, (1, _KV_PAD), 1)
    one_col = ((kv_lane & (LANES - 1)) == _ONE_LANE).astype(F32)

    qk = jnp.concatenate(
        [proj(_OFF_BQ, B_WIDTH),
         jnp.dot(u, wkv_ref[:, 0:_KV_PAD], preferred_element_type=F32)], axis=1)
    sq = qk * qk
    sq_hi = sq.astype(BF16)
    sq_lo = (sq - sq_hi.astype(F32)).astype(BF16)
    def head_sums(part):
        return jnp.concatenate(
            [jnp.dot(part[:, c:c + _BD_W], bd_ref[...], preferred_element_type=F32)
             for c in range(0, _QK_W, _BD_W)], axis=1)
    ss = head_sums(sq_hi) + head_sums(sq_lo)
    y = qk * lax.rsqrt(ss * (1.0 / HEAD_DIM) + QK_EPS) * gqk_ref[...]
    reps = _QK_W // LANES
    cos = jnp.concatenate([cos_ref[...]] * reps, axis=1)
    sin = jnp.concatenate([sin_ref[...]] * reps, axis=1)
    lane = lax.broadcasted_iota(jnp.int32, y.shape, 1)
    first = (lane & 31) < 16
    swapped = jnp.where(first, pltpu.roll(y, _QK_W - 16, 1), pltpu.roll(y, 16, 1))
    rot = y * cos + swapped * sin
    q = rot[:, :B_WIDTH] * _Q_SCALE
    k = rot[:, B_WIDTH:]
    bk_ref[0] = (k + one_col).astype(BF16)
    q_t = q.T
    bqt_ref[0] = q_t.astype(BF16)
    qn2 = jnp.concatenate(
        [jnp.sum(jnp.square(q_t[h * HEAD_DIM:(h + 1) * HEAD_DIM]), axis=0, keepdims=True)
         for h in range(B_Q_HEADS)], axis=0)
    qn_ref[0] = qn2

    def tile_max(v2):
        return jnp.max(jnp.sum(v2, axis=1, keepdims=True), axis=0, keepdims=True)
    k2 = k * k
    kmax = [tile_max(k2[:, h * LANES:(h + 1) * LANES]) for h in range(B_KV_HEADS)]
    qmax = jnp.max(jnp.max(qn2, axis=1, keepdims=True), axis=0, keepdims=True)
    sub = lax.broadcasted_iota(jnp.int32, (SUBLANES, LANES), 0)
    st_ref[0, 0] = jnp.where(sub == 0, kmax[0], jnp.where(sub == 1, kmax[1], qmax))

    def proj_a(g):
        chunks = []
        for part in range(3):
            res = proj(part * _A_PART + g * A_WIDTH, A_WIDTH)
            if part == 0:
                res = res * _Q_SCALE
            chunks += [res[:, ch * LANES:(ch + 1) * LANES] for ch in range(A_WIDTH // LANES)]
        return chunks

    for g, a_ref in ((1, a1_ref), (2, a2_ref)):
        dil = A_GROUPS[g][1]
        for ch, chunk in enumerate(proj_a(g)):
            scr_ref[ch] = chunk
        for r in range(dil):
            for ch in range(_A_SEG // LANES):
                a_ref[0, r, :, ch * LANES:(ch + 1) * LANES] = (
                    scr_ref[ch, pl.ds(r, tm // dil, stride=dil), :].astype(BF16))

    az = proj(_OFF_AZ, A_WIDTH)
    gz_ref[0, :, 0:A_WIDTH] = (az * jax.nn.sigmoid(az)).astype(BF16)
    bz = proj(_OFF_BZ, B_WIDTH)
    gz_ref[0, :, A_WIDTH:A_WIDTH + B_WIDTH] = (bz * jax.nn.sigmoid(bz)).astype(BF16)
    for j in range(2):
        gl = proj(_OFF_GL + j * d_model, d_model) + bgate_ref[:, j * d_model:(j + 1) * d_model]
        o = A_WIDTH + B_WIDTH + j * d_model
        gz_ref[0, :, o:o + d_model] = jax.nn.sigmoid(gl).astype(BF16)

    a0_ref[0] = jnp.concatenate(proj_a(0), axis=1).astype(BF16)
    v_pad = jnp.dot(u, wkv_ref[:, _KV_PAD:2 * _KV_PAD], preferred_element_type=F32)
    bvt_ref[0] = (v_pad + one_col).T.astype(BF16)


def _inproj_call(x, shift, scale, w_all, layer, w_kv, b_gate, gqk, cos_t, sin_t, bd):
    b, s, d = x.shape
    tm = IN_TM
    n_cols = w_all.shape[2]
    d1, d2 = A_GROUPS[1][1], A_GROUPS[2][1]
    const = dict(pipeline_mode=pl.Buffered(1))
    out_shape = (
        jax.ShapeDtypeStruct((b, s, _A_SEG), BF16),
        jax.ShapeDtypeStruct((b, d1, s // d1, _A_SEG), BF16),
        jax.ShapeDtypeStruct((b, d2, s // d2, _A_SEG), BF16),
        jax.ShapeDtypeStruct((b, B_WIDTH, s), BF16),
        jax.ShapeDtypeStruct((b, B_Q_HEADS, s), F32),
        jax.ShapeDtypeStruct((b, s, _KV_PAD), BF16),
        jax.ShapeDtypeStruct((b, _KV_PAD, s), BF16),
        jax.ShapeDtypeStruct((b, s, A_WIDTH + B_WIDTH + 2 * d), BF16),
        jax.ShapeDtypeStruct((b, s // tm, SUBLANES, LANES), F32),
    )
    row = lambda bi, i: (bi, i, 0)
    out_specs = (
        pl.BlockSpec((1, tm, _A_SEG), row),
        pl.BlockSpec((1, d1, tm // d1, _A_SEG), lambda bi, i: (bi, 0, i, 0)),
        pl.BlockSpec((1, d2, tm // d2, _A_SEG), lambda bi, i: (bi, 0, i, 0)),
        pl.BlockSpec((1, B_WIDTH, tm), lambda bi, i: (bi, 0, i)),
        pl.BlockSpec((1, B_Q_HEADS, tm), lambda bi, i: (bi, 0, i)),
        pl.BlockSpec((1, tm, _KV_PAD), row),
        pl.BlockSpec((1, _KV_PAD, tm), lambda bi, i: (bi, 0, i)),
        pl.BlockSpec((1, tm, A_WIDTH + B_WIDTH + 2 * d), row),
        pl.BlockSpec((1, 1, SUBLANES, LANES), lambda bi, i: (bi, i, 0, 0)),
    )
    in_specs = [
        pl.BlockSpec((1, tm, d), row),
        pl.BlockSpec((1, 1, d), lambda bi, i: (bi, 0, 0)),
        pl.BlockSpec((1, 1, d), lambda bi, i: (bi, 0, 0)),
        pl.BlockSpec((None, d, n_cols), lambda bi, i: (layer, 0, 0), **const),
        pl.BlockSpec((d, 2 * _KV_PAD), lambda bi, i: (0, 0), **const),
        pl.BlockSpec((1, 2 * d), lambda bi, i: (0, 0), **const),
        pl.BlockSpec((1, _QK_W), lambda bi, i: (0, 0), **const),
        pl.BlockSpec((tm, LANES), lambda bi, i: (i, 0)),
        pl.BlockSpec((tm, LANES), lambda bi, i: (i, 0)),
        pl.BlockSpec((_BD_W, _BD_W), lambda bi, i: (0, 0), **const),
    ]
    return pl.pallas_call(
        _inproj_kernel,
        out_shape=out_shape,
        grid=(b, s // tm),
        in_specs=in_specs,
        out_specs=out_specs,
        scratch_shapes=[pltpu.VMEM((_A_SEG // LANES, tm, LANES), F32)],
        compiler_params=pltpu.CompilerParams(
            dimension_semantics=("arbitrary", "arbitrary"),
            vmem_limit_bytes=VMEM_LIMIT_BYTES),
        name="inproj",
    )(x, shift, scale, w_all, w_kv, b_gate, gqk, cos_t, sin_t, bd)


def _attn_a_kernel(q_ref, kp_ref, kc_ref, kn_ref, vp_ref, vc_ref, vn_ref, bias_ref,
                   o_ref, lse_ref, range_ref, *, phase_len, stabilise):
    n_ph, tq = q_ref.shape[1], q_ref.shape[2]
    i = pl.program_id(2)
    tk = A_SUB + 2 * A_RADIUS
    n_sub = tq // A_SUB
    lane = lax.broadcasted_iota(jnp.int32, (A_SUB, LANES), 1)
    low = lane < HEAD_DIM
    kcol = lax.broadcasted_iota(jnp.int32, (1, tk), 1)
    worst = jnp.zeros((A_SUB, LANES), F32)
    for ph, sub in [(ph, sub) for ph in range(n_ph) for sub in range(n_sub)]:
        if sub == 0:
            q = q_ref[0, ph]
            k_all = jnp.concatenate([kp_ref[0, ph], kc_ref[0, ph], kn_ref[0, ph]], axis=0)
            v_all = jnp.concatenate([vp_ref[0, ph], vc_ref[0, ph], vn_ref[0, ph]], axis=0)
        r0 = sub * A_SUB
        at_edge = sub == 0 or sub == n_sub - 1
        if at_edge:
            krow = i * tq + (r0 - A_RADIUS) + kcol
            valid = (krow >= 0) & (krow < phase_len)
        for pair in range(A_HEADS_PER_GROUP // 2):
            c0 = pair * LANES
            qp = q[r0:r0 + A_SUB, c0:c0 + LANES]
            kp = k_all[r0:r0 + tk, c0:c0 + LANES]
            vp = v_all[r0:r0 + tk, c0:c0 + LANES]
            zero = jnp.zeros_like(qp)
            qs = jnp.concatenate([jnp.where(low, qp, zero), jnp.where(low, zero, qp)], axis=0)
            s = lax.dot_general(qs, kp, _NT, preferred_element_type=F32) + bias_ref[pair]
            if at_edge:
                s = jnp.where(valid, s, NEG_INF)
            if stabilise:
                m = jnp.max(s, axis=1, keepdims=True)
                s = s - m
            p = jnp.exp2(s)
            l = jnp.sum(p, axis=1, keepdims=True)
            o = jnp.dot(p.astype(BF16), vp, preferred_element_type=F32)
            pick = lambda a: jnp.where(low, a[:A_SUB], a[A_SUB:])
            l = pick(l)
            lse = jnp.log2(l)
            worst = jnp.maximum(worst, jnp.abs(lse))
            if stabilise:
                lse = lse + pick(m)
            o_ref[0, ph, r0:r0 + A_SUB, c0:c0 + LANES] = (pick(o) * (1.0 / l)).astype(BF16)
            lse_ref[0, ph, r0:r0 + A_SUB, c0:c0 + LANES] = lse
    range_ref[0, 0, 0] = jnp.broadcast_to(
        jnp.max(jnp.max(worst, axis=1, keepdims=True), axis=0, keepdims=True), (SUBLANES, LANES))


def _attn_a_call(a_g, bias_g, *, stabilise):
    b, dil, phase_len, _ = a_g.shape
    tq = min(A_TQ, phase_len)
    n_steps = phase_len // tq
    n_ph = min(dil, A_TQ // tq)
    halo = A_RADIUS
    nh = phase_len // halo
    per = tq // halo
    cur = lambda col: (lambda bi, r, i: (bi, r, i, col))
    prev = lambda col: (lambda bi, r, i: (bi, r, jnp.maximum(i * per - 1, 0), col))
    nxt = lambda col: (lambda bi, r, i: (bi, r, jnp.minimum((i + 1) * per, nh - 1), col))
    blk = (1, n_ph, tq, A_WIDTH)
    hblk = (1, n_ph, halo, A_WIDTH)
    o_sds = jax.ShapeDtypeStruct((b, dil, phase_len, A_WIDTH), BF16)
    lse_sds = jax.ShapeDtypeStruct((b, dil, phase_len, A_WIDTH), F32)
    range_sds = jax.ShapeDtypeStruct((b, dil // n_ph, n_steps, SUBLANES, LANES), F32)
    return pl.pallas_call(
        functools.partial(_attn_a_kernel, phase_len=phase_len, stabilise=stabilise),
        out_shape=(o_sds, lse_sds, range_sds),
        grid=(b, dil // n_ph, n_steps),
        in_specs=[
            pl.BlockSpec(blk, cur(0)),
            pl.BlockSpec(hblk, prev(1)), pl.BlockSpec(blk, cur(1)), pl.BlockSpec(hblk, nxt(1)),
            pl.BlockSpec(hblk, prev(2)), pl.BlockSpec(blk, cur(2)), pl.BlockSpec(hblk, nxt(2)),
            pl.BlockSpec(bias_g.shape, lambda bi, r, i: (0, 0, 0)),
        ],
        out_specs=(pl.BlockSpec(blk, cur(0)), pl.BlockSpec(blk, cur(0)),
                   pl.BlockSpec((1, 1, 1, SUBLANES, LANES), lambda bi, r, i: (bi, r, i, 0, 0))),
        compiler_params=pltpu.CompilerParams(
            dimension_semantics=("arbitrary", "arbitrary", "arbitrary"),
            vmem_limit_bytes=VMEM_LIMIT_BYTES),
        name=f"attn_a_d{dil}" + ("_stab" if stabilise else ""),
    )(a_g, a_g, a_g, a_g, a_g, a_g, a_g, bias_g)


def _attn_a(a_g, bias_g):
    o, lse, worst = _attn_a_call(a_g, bias_g, stabilise=False)
    in_range = jnp.max(worst) <= A_MAX_ABS_LOG2_DENOM
    return lax.cond(in_range,
                    lambda: (o, lse),
                    lambda: _attn_a_call(a_g, bias_g, stabilise=True)[:2])


_VT_ROWS = 80


def _attn_b_kernel(kmax_ref, qt_ref, qn_ref, k_ref, vt_ref, o_ref, qst_ref, acc_ref, *m_scratch,
                   online):
    tq = qt_ref.shape[2]
    seq = k_ref.shape[1]
    kmax = kmax_ref[pl.program_id(0) * B_KV_HEADS + pl.program_id(1)]
    tail_row = lax.broadcasted_iota(jnp.int32, (LANES - HEAD_DIM, tq), 0)
    for h in range(B_GROUP):
        cols = slice(h * tq, (h + 1) * tq)
        qst_ref[0:HEAD_DIM, cols] = qt_ref[0, h * HEAD_DIM:(h + 1) * HEAD_DIM, :]
        if online:
            tail = jnp.zeros(tail_row.shape, F32)
        else:
            bound = jnp.sqrt(qn_ref[0, 0, h:h + 1, :]) * (kmax * B_BOUND_SLACK)
            tail = jnp.where(tail_row == _ONE_LANE - HEAD_DIM, -bound, 0.0)
        qst_ref[HEAD_DIM:LANES, cols] = tail.astype(BF16)
    acc_ref[...] = jnp.zeros(acc_ref.shape, F32)
    if online:
        m_ref, = m_scratch
        m_ref[...] = jnp.full(m_ref.shape, NEG_INF, F32)

    tk = B_TK_ONLINE if online else B_TK
    n_chunks = seq // tk

    def scores_t(c):
        start = pl.multiple_of(c * tk, tk)
        return jnp.dot(k_ref[0, pl.ds(start, tk), :], qst_ref[...],
                       preferred_element_type=F32)

    def values_t(c):
        start = pl.multiple_of(c * tk, tk)
        return vt_ref[0, 0:_VT_ROWS, pl.ds(start, tk)]

    if online:
        def body(c, carry):
            s_t = scores_t(c)
            m_prev = m_ref[...]
            m_new = jnp.maximum(m_prev, jnp.max(s_t, axis=0, keepdims=True))
            p_t = jnp.exp2(s_t - m_new).astype(BF16)
            acc_ref[...] = (jnp.exp2(m_prev - m_new) * acc_ref[...]
                            + jnp.dot(values_t(c), p_t, preferred_element_type=F32))
            m_ref[...] = m_new
            return carry

        lax.fori_loop(0, n_chunks, body, 0)
    else:
        def body(c, carry):
            start = pl.multiple_of(c * tk, tk)
            kc = k_ref[0, pl.ds(start, tk), :]
            vtc = values_t(c)
            for h in range(B_GROUP):
                cols = slice(h * tq, (h + 1) * tq)
                p_t = jnp.exp2(jnp.dot(kc, qst_ref[:, cols],
                                       preferred_element_type=F32)).astype(BF16)
                acc_ref[:, cols] += jnp.dot(vtc, p_t, preferred_element_type=F32)
            return carry

        lax.fori_loop(0, n_chunks, body, 0)
    acc = acc_ref[...]
    o_t = acc[0:HEAD_DIM, :] * (1.0 / acc[_ONE_LANE:_ONE_LANE + 1, :])
    for h in range(B_GROUP):
        o_ref[0, h * HEAD_DIM:(h + 1) * HEAD_DIM, :] = o_t[:, h * tq:(h + 1) * tq].astype(BF16)


def _attn_b_call(kmax, bqt, qn2, bk, bvt, *, online):
    b, _, s = bqt.shape
    qw = B_GROUP * HEAD_DIM
    cols = B_GROUP * B_TQ
    scratch = [pltpu.VMEM((LANES, cols), BF16), pltpu.VMEM((_VT_ROWS, cols), F32)]
    if online:
        scratch.append(pltpu.VMEM((1, cols), F32))
    return pl.pallas_call(
        functools.partial(_attn_b_kernel, online=online),
        out_shape=jax.ShapeDtypeStruct((b, B_WIDTH, s), BF16),
        grid_spec=pltpu.PrefetchScalarGridSpec(
            num_scalar_prefetch=1,
            grid=(b, B_KV_HEADS, s // B_TQ),
            in_specs=[
                pl.BlockSpec((1, qw, B_TQ), lambda bi, h, i, km: (bi, h, i)),
                pl.BlockSpec((1, 1, B_GROUP, B_TQ), lambda bi, h, i, km: (bi, h, 0, i)),
                pl.BlockSpec((1, s, LANES), lambda bi, h, i, km: (bi, 0, h)),
                pl.BlockSpec((1, LANES, s), lambda bi, h, i, km: (bi, h, 0)),
            ],
            out_specs=pl.BlockSpec((1, qw, B_TQ), lambda bi, h, i, km: (bi, h, i)),
            scratch_shapes=scratch,
        ),
        compiler_params=pltpu.CompilerParams(
            dimension_semantics=("arbitrary", "arbitrary", "arbitrary"),
            vmem_limit_bytes=VMEM_LIMIT_BYTES),
        name="attn_b_online" if online else "attn_b",
    )(kmax, bqt, qn2.reshape(b, B_KV_HEADS, B_GROUP, s), bk, bvt)


def _outproj_kernel(x_ref, gate_ref, o0_ref, l0_ref, o1_ref, l1_ref, o2_ref, l2_ref,
                    ybt_ref, gz_ref, wpa_ref, wpb_ref, wo_ref, lng_ref, lnb_ref,
                    out_ref, so1, sl1, so2, sl2, *, alpha):
    tm = x_ref.shape[1]
    d_model = x_ref.shape[2]
    for (o_ref, l_ref, so, sl, g) in ((o1_ref, l1_ref, so1, sl1, 1), (o2_ref, l2_ref, so2, sl2, 2)):
        dil = A_GROUPS[g][1]
        for r in range(dil):
            for ch in range(A_WIDTH // LANES):
                cs = slice(ch * LANES, (ch + 1) * LANES)
                so[ch, pl.ds(r, tm // dil, stride=dil), :] = o_ref[0, r, :, cs].astype(F32)
                sl[ch, pl.ds(r, tm // dil, stride=dil), :] = l_ref[0, r, :, cs]
    cat = lambda ref: jnp.concatenate([ref[ch] for ch in range(A_WIDTH // LANES)], axis=1)
    l0, l1, l2 = l0_ref[0], cat(sl1), cat(sl2)
    mx = jnp.maximum(jnp.maximum(l0, l1), l2)
    e0, e1, e2 = jnp.exp2(l0 - mx), jnp.exp2(l1 - mx), jnp.exp2(l2 - mx)
    y_a = ((e0 * o0_ref[0].astype(F32) + e1 * cat(so1) + e2 * cat(so2))
           * (1.0 / (e0 + e1 + e2)))
    gz = gz_ref[0]
    ya = (y_a * gz[:, 0:A_WIDTH].astype(F32)).astype(BF16)
    y_b = ybt_ref[0].astype(F32).T
    yb = (y_b * gz[:, A_WIDTH:A_WIDTH + B_WIDTH].astype(F32)).astype(BF16)
    pa = jnp.dot(ya, wpa_ref[...], preferred_element_type=F32)
    pb = jnp.dot(yb, wpb_ref[...], preferred_element_type=F32)
    o = A_WIDTH + B_WIDTH
    g_a = gz[:, o:o + d_model].astype(F32)
    g_b = gz[:, o + d_model:o + 2 * d_model].astype(F32)
    merged = (g_a * pa + g_b * pb).astype(BF16)
    out = jnp.dot(merged, wo_ref[...], preferred_element_type=F32)
    h = alpha * x_ref[0] + gate_ref[0] * out
    mu = jnp.mean(h, axis=-1, keepdims=True)
    hc = h - mu
    var = jnp.mean(hc * hc, axis=-1, keepdims=True)
    out_ref[0] = hc * lax.rsqrt(var + LN_EPS) * lng_ref[...] + lnb_ref[...]


def _outproj_call(x, gate, oa, ybt, gz, w_pa, w_pb, w_o, ln_g, ln_b, alpha):
    b, s, d = x.shape
    tm = OUT_TM
    (o0, l0), (o1, l1), (o2, l2) = oa
    d1, d2 = A_GROUPS[1][1], A_GROUPS[2][1]
    row = lambda bi, i: (bi, i, 0)
    ph = lambda bi, i: (bi, 0, i, 0)
    fixed = lambda bi, i: (0, 0)
    s0 = pl.BlockSpec((1, 1, tm, A_WIDTH), ph)
    s1 = pl.BlockSpec((1, d1, tm // d1, A_WIDTH), ph)
    s2 = pl.BlockSpec((1, d2, tm // d2, A_WIDTH), ph)

    def kern(x_ref, gate_ref, o0_ref, l0_ref, *rest):
        return _outproj_kernel(x_ref, gate_ref, o0_ref.at[0], l0_ref.at[0], *rest, alpha=alpha)

    return pl.pallas_call(
        kern,
        out_shape=jax.ShapeDtypeStruct((b, s, d), F32),
        grid=(b, s // tm),
        in_specs=[
            pl.BlockSpec((1, tm, d), row),
            pl.BlockSpec((1, 1, d), lambda bi, i: (bi, 0, 0)),
            s0, s0, s1, s1, s2, s2,
            pl.BlockSpec((1, B_WIDTH, tm), lambda bi, i: (bi, 0, i)),
            pl.BlockSpec((1, tm, gz.shape[2]), row),
            pl.BlockSpec(w_pa.shape, fixed),
            pl.BlockSpec(w_pb.shape, fixed),
            pl.BlockSpec(w_o.shape, fixed),
            pl.BlockSpec((1, d), fixed),
            pl.BlockSpec((1, d), fixed),
        ],
        out_specs=pl.BlockSpec((1, tm, d), row),
        scratch_shapes=[pltpu.VMEM((A_WIDTH // LANES, tm, LANES), F32)] * 4,
        compiler_params=pltpu.CompilerParams(
            dimension_semantics=("arbitrary", "arbitrary"),
            vmem_limit_bytes=VMEM_LIMIT_BYTES),
        name="outproj",
    )(x, gate, o0, l0, o1, l1, o2, l2, ybt, gz, w_pa, w_pb, w_o, ln_g, ln_b)


def _t5_bucket(rel):
    half = REL_BUCKETS // 2
    max_exact = half // 2
    ret = jnp.where(rel > 0, half, 0)
    a = jnp.abs(rel)
    af = jnp.maximum(a, 1).astype(F32)
    large = max_exact + (jnp.log(af / max_exact) / math.log(REL_MAX_DISTANCE / max_exact)
                         * (half - max_exact)).astype(jnp.int32)
    large = jnp.minimum(large, half - 1)
    return ret + jnp.where(a < max_exact, a, large)


def _window_bias(rel_table, g):
    dil = A_GROUPS[g][1]
    tk = A_SUB + 2 * A_RADIUS
    rel = jnp.arange(tk)[None, :] - A_RADIUS - jnp.arange(A_SUB)[:, None]
    table_g = rel_table[:, g * A_HEADS_PER_GROUP:(g + 1) * A_HEADS_PER_GROUP]
    onehot = (_t5_bucket(rel * dil)[..., None] == jnp.arange(REL_BUCKETS)).astype(F32)
    bias = jnp.einsum("qkb,bh->hqk", onehot, table_g.astype(F32), precision=lax.Precision.HIGHEST)
    bias = jnp.where((jnp.abs(rel) <= A_RADIUS)[None], bias * LOG2_E, NEG_INF)
    return bias.reshape(A_HEADS_PER_GROUP // 2, 2 * A_SUB, tk)


def _rope_tables(seq):
    t = jnp.arange(seq)
    row = (t // GRID_W).astype(F32)
    col = (t % GRID_W).astype(F32)
    half = HEAD_DIM // 2
    inv = ROPE_THETA ** (-jnp.arange(0, half, 2, dtype=F32) / half)
    ar, ac = row[:, None] * inv[None], col[:, None] * inv[None]
    cos = jnp.concatenate([jnp.cos(ar), jnp.cos(ar), jnp.cos(ac), jnp.cos(ac)], axis=1)
    sin = jnp.concatenate([-jnp.sin(ar), jnp.sin(ar), -jnp.sin(ac), jnp.sin(ac)], axis=1)
    return jnp.tile(cos, (1, LANES // HEAD_DIM)), jnp.tile(sin, (1, LANES // HEAD_DIM))


def _padded_kv_weights(w):
    d = w.shape[0]
    off = _OFF_BQ + B_WIDTH
    zeros = jnp.zeros((d, LANES - HEAD_DIM), w.dtype)
    blocks = []
    for h in range(2 * B_KV_HEADS):
        blocks += [w[:, off + h * HEAD_DIM:off + (h + 1) * HEAD_DIM], zeros]
    return jnp.concatenate(blocks, axis=1).astype(BF16)


def _pad_heads(g):
    return jnp.concatenate([g, jnp.zeros((LANES - HEAD_DIM,), g.dtype)])


def kernel(x, c, rel_table, ln_g, ln_b, w_ada, b_ada, w_in, b_gate, q_norm_g, k_norm_g, w_pa, w_pb, w_o):
    depth = w_in.shape[0]
    b, s, d = x.shape
    alpha = float((2 * depth) ** 0.25)

    c_pad = jnp.zeros((SUBLANES, d), F32).at[:b].set(c)
    mod = _ada_call(c_pad, w_ada, b_ada)[:, :b]
    cos_t, sin_t = _rope_tables(s)
    idx = jnp.arange(_BD_W) // HEAD_DIM
    bd = (idx[:, None] == idx[None, :]).astype(BF16)
    biases = [_window_bias(rel_table, g) for g in range(len(A_GROUPS))]
    w_all = w_in.astype(BF16)

    for l in range(depth):
        shift = mod[l, :, 0:d].reshape(b, 1, d)
        scale = mod[l, :, d:2 * d].reshape(b, 1, d)
        gate = mod[l, :, 2 * d:3 * d].reshape(b, 1, d)
        gqk = jnp.concatenate([jnp.tile(q_norm_g[l], B_Q_HEADS),
                               jnp.tile(_pad_heads(k_norm_g[l]), B_KV_HEADS)]).reshape(1, _QK_W)
        a0, a1, a2, bqt, qn2, bk, bvt, gz, st = _inproj_call(
            x, shift, scale, w_all, l, _padded_kv_weights(w_in[l]), b_gate[l].reshape(1, 2 * d), gqk,
            cos_t, sin_t, bd)
        a0 = a0.reshape(b, 1, s, _A_SEG)
        oa = [_attn_a(a_g, biases[g]) for g, a_g in enumerate((a0, a1, a2))]
        kmax = jnp.sqrt(jnp.max(st[:, :, 0:B_KV_HEADS, 0], axis=1))
        qmax = jnp.sqrt(jnp.max(st[:, :, B_KV_HEADS, 0]))
        bound_ok = qmax * jnp.max(kmax) * B_BOUND_SLACK <= B_MAX_BOUND
        ybt = lax.cond(
            bound_ok,
            functools.partial(_attn_b_call, online=False),
            functools.partial(_attn_b_call, online=True),
            kmax.reshape(-1), bqt, qn2, bk, bvt)
        x = _outproj_call(x, gate, oa, ybt, gz, w_pa[l].astype(BF16), w_pb[l].astype(BF16),
                          w_o[l].astype(BF16), ln_g[l].reshape(1, d), ln_b[l].reshape(1, d), alpha)
    return x
```

```python
import functools
import math

import jax
import jax.numpy as jnp
from jax import lax
from jax.experimental import pallas as pl
from jax.experimental.pallas import tpu as pltpu

HEAD_DIM = 64
A_GROUPS = ((128, 1), (512, 4), (2048, 16))
A_HEADS_PER_GROUP = 8
A_WIDTH = A_HEADS_PER_GROUP * HEAD_DIM
A_RADIUS = 64
B_Q_HEADS = 8
B_KV_HEADS = 2
B_GROUP = B_Q_HEADS // B_KV_HEADS
B_WIDTH = B_Q_HEADS * HEAD_DIM
GRID_W = 64
ROPE_THETA = 10000.0
REL_BUCKETS = 32
REL_MAX_DISTANCE = 1024
LN_EPS = 1e-5
QK_EPS = 1e-6
NEG_INF = -1e30
LOG2_E = math.log2(math.e)
_Q_SCALE = HEAD_DIM ** -0.5 * LOG2_E

LANES = 128
SUBLANES = 8
VMEM_LIMIT_BYTES = 56 * 1024 * 1024

ADA_TN = 1024
IN_TM = 256
A_TQ = 2048
A_SUB = 128
B_TQ = 2048
B_TK = 2048
B_TK_ONLINE = 512
OUT_TM = 512

B_MAX_BOUND = 50.0
A_MAX_ABS_LOG2_DENOM = 90.0
B_BOUND_SLACK = 1.0 + 2.0 ** -6

BF16 = jnp.bfloat16
F32 = jnp.float32

_NT = (((1,), (1,)), ((), ()))


def _ada_kernel(c_ref, w_ref, b_ref, o_ref):
    c = c_ref[...]
    h = c * jax.nn.sigmoid(c)
    o_ref[0] = jnp.dot(h, w_ref[0], preferred_element_type=F32,
                       precision=lax.Precision.HIGHEST) + b_ref[0]


def _ada_call(c_pad, w_ada, b_ada):
    depth, d, n3 = w_ada.shape
    rows = c_pad.shape[0]
    return pl.pallas_call(
        _ada_kernel,
        out_shape=jax.ShapeDtypeStruct((depth, rows, n3), F32),
        grid=(depth, n3 // ADA_TN),
        in_specs=[
            pl.BlockSpec((rows, d), lambda l, j: (0, 0)),
            pl.BlockSpec((1, d, ADA_TN), lambda l, j: (l, 0, j)),
            pl.BlockSpec((1, 1, ADA_TN), lambda l, j: (l, 0, j)),
        ],
        out_specs=pl.BlockSpec((1, rows, ADA_TN), lambda l, j: (l, 0, j)),
        compiler_params=pltpu.CompilerParams(
            dimension_semantics=("arbitrary", "arbitrary")),
        name="ada_mod",
    )(c_pad, w_ada, b_ada.reshape(depth, 1, n3))


_A_SEG = 3 * A_WIDTH
_A_PART = len(A_GROUPS) * A_WIDTH
_OFF_AZ = 3 * _A_PART
_OFF_BQ = _OFF_AZ + A_WIDTH
_KV_PAD = B_KV_HEADS * LANES
_OFF_BZ = _OFF_BQ + B_WIDTH + 2 * B_KV_HEADS * HEAD_DIM
_OFF_GL = _OFF_BZ + B_WIDTH
_QK_W = B_WIDTH + _KV_PAD
_BD_W = 2 * LANES
_ONE_LANE = HEAD_DIM


def _inproj_kernel(x_ref, shift_ref, scale_ref, w_ref, wkv_ref, bgate_ref, gqk_ref,
                   cos_ref, sin_ref, bd_ref,
                   a0_ref, a1_ref, a2_ref, bqt_ref, qn_ref, bk_ref, bvt_ref, gz_ref, st_ref,
                   scr_ref):
    tm = x_ref.shape[1]
    d_model = x_ref.shape[2]
    u = (x_ref[0] * (1.0 + scale_ref[0]) + shift_ref[0]).astype(BF16)

    def proj(off, width):
        return jnp.dot(u, w_ref[:, off:off + width], preferred_element_type=F32)

    kv_lane = lax.broadcasted_iota(jnp.int32, (1, _KV_PAD), 1)
    one_col = ((kv_lane & (LANES - 1)) == _ONE_LANE).astype(F32)

    qk = jnp.concatenate(
        [proj(_OFF_BQ, B_WIDTH),
         jnp.dot(u, wkv_ref[:, 0:_KV_PAD], preferred_element_type=F32)], axis=1)
    sq = qk * qk
    sq_hi = sq.astype(BF16)
    sq_lo = (sq - sq_hi.astype(F32)).astype(BF16)
    def head_sums(part):
        return jnp.concatenate(
            [jnp.dot(part[:, c:c + _BD_W], bd_ref[...], preferred_element_type=F32)
             for c in range(0, _QK_W, _BD_W)], axis=1)
    ss = head_sums(sq_hi) + head_sums(sq_lo)
    y = qk * lax.rsqrt(ss * (1.0 / HEAD_DIM) + QK_EPS) * gqk_ref[...]
    reps = _QK_W // LANES
    cos = jnp.concatenate([cos_ref[...]] * reps, axis=1)
    sin = jnp.concatenate([sin_ref[...]] * reps, axis=1)
    lane = lax.broadcasted_iota(jnp.int32, y.shape, 1)
    first = (lane & 31) < 16
    swapped = jnp.where(first, pltpu.roll(y, _QK_W - 16, 1), pltpu.roll(y, 16, 1))
    rot = y * cos + swapped * sin
    q = rot[:, :B_WIDTH] * _Q_SCALE
    k = rot[:, B_WIDTH:]
    bk_ref[0] = (k + one_col).astype(BF16)
    q_t = q.T
    bqt_ref[0] = q_t.astype(BF16)
    qn2 = jnp.concatenate(
        [jnp.sum(jnp.square(q_t[h * HEAD_DIM:(h + 1) * HEAD_DIM]), axis=0, keepdims=True)
         for h in range(B_Q_HEADS)], axis=0)
    qn_ref[0] = qn2

    def tile_max(v2):
        return jnp.max(jnp.sum(v2, axis=1, keepdims=True), axis=0, keepdims=True)
    k2 = k * k
    kmax = [tile_max(k2[:, h * LANES:(h + 1) * LANES]) for h in range(B_KV_HEADS)]
    qmax = jnp.max(jnp.max(qn2, axis=1, keepdims=True), axis=0, keepdims=True)
    sub = lax.broadcasted_iota(jnp.int32, (SUBLANES, LANES), 0)
    st_ref[0, 0] = jnp.where(sub == 0, kmax[0], jnp.where(sub == 1, kmax[1], qmax))

    def proj_a(g):
        chunks = []
        for part in range(3):
            res = proj(part * _A_PART + g * A_WIDTH, A_WIDTH)
            if part == 0:
                res = res * _Q_SCALE
            chunks += [res[:, ch * LANES:(ch + 1) * LANES] for ch in range(A_WIDTH // LANES)]
        return chunks

    for g, a_ref in ((1, a1_ref), (2, a2_ref)):
        dil = A_GROUPS[g][1]
        for ch, chunk in enumerate(proj_a(g)):
            scr_ref[ch] = chunk
        for r in range(dil):
            for ch in range(_A_SEG // LANES):
                a_ref[0, r, :, ch * LANES:(ch + 1) * LANES] = (
                    scr_ref[ch, pl.ds(r, tm // dil, stride=dil), :].astype(BF16))

    az = proj(_OFF_AZ, A_WIDTH)
    gz_ref[0, :, 0:A_WIDTH] = (az * jax.nn.sigmoid(az)).astype(BF16)
    bz = proj(_OFF_BZ, B_WIDTH)
    gz_ref[0, :, A_WIDTH:A_WIDTH + B_WIDTH] = (bz * jax.nn.sigmoid(bz)).astype(BF16)
    for j in range(2):
        gl = proj(_OFF_GL + j * d_model, d_model) + bgate_ref[:, j * d_model:(j + 1) * d_model]
        o = A_WIDTH + B_WIDTH + j * d_model
        gz_ref[0, :, o:o + d_model] = jax.nn.sigmoid(gl).astype(BF16)

    a0_ref[0] = jnp.concatenate(proj_a(0), axis=1).astype(BF16)
    v_pad = jnp.dot(u, wkv_ref[:, _KV_PAD:2 * _KV_PAD], preferred_element_type=F32)
    bvt_ref[0] = (v_pad + one_col).T.astype(BF16)


def _inproj_call(x, shift, scale, w_all, layer, w_kv, b_gate, gqk, cos_t, sin_t, bd):
    b, s, d = x.shape
    tm = IN_TM
    n_cols = w_all.shape[2]
    d1, d2 = A_GROUPS[1][1], A_GROUPS[2][1]
    const = dict(pipeline_mode=pl.Buffered(1))
    out_shape = (
        jax.ShapeDtypeStruct((b, s, _A_SEG), BF16),
        jax.ShapeDtypeStruct((b, d1, s // d1, _A_SEG), BF16),
        jax.ShapeDtypeStruct((b, d2, s // d2, _A_SEG), BF16),
        jax.ShapeDtypeStruct((b, B_WIDTH, s), BF16),
        jax.ShapeDtypeStruct((b, B_Q_HEADS, s), F32),
        jax.ShapeDtypeStruct((b, s, _KV_PAD), BF16),
        jax.ShapeDtypeStruct((b, _KV_PAD, s), BF16),
        jax.ShapeDtypeStruct((b, s, A_WIDTH + B_WIDTH + 2 * d), BF16),
        jax.ShapeDtypeStruct((b, s // tm, SUBLANES, LANES), F32),
    )
    row = lambda bi, i: (bi, i, 0)
    out_specs = (
        pl.BlockSpec((1, tm, _A_SEG), row),
        pl.BlockSpec((1, d1, tm // d1, _A_SEG), lambda bi, i: (bi, 0, i, 0)),
        pl.BlockSpec((1, d2, tm // d2, _A_SEG), lambda bi, i: (bi, 0, i, 0)),
        pl.BlockSpec((1, B_WIDTH, tm), lambda bi, i: (bi, 0, i)),
        pl.BlockSpec((1, B_Q_HEADS, tm), lambda bi, i: (bi, 0, i)),
        pl.BlockSpec((1, tm, _KV_PAD), row),
        pl.BlockSpec((1, _KV_PAD, tm), lambda bi, i: (bi, 0, i)),
        pl.BlockSpec((1, tm, A_WIDTH + B_WIDTH + 2 * d), row),
        pl.BlockSpec((1, 1, SUBLANES, LANES), lambda bi, i: (bi, i, 0, 0)),
    )
    in_specs = [
        pl.BlockSpec((1, tm, d), row),
        pl.BlockSpec((1, 1, d), lambda bi, i: (bi, 0, 0)),
        pl.BlockSpec((1, 1, d), lambda bi, i: (bi, 0, 0)),
        pl.BlockSpec((None, d, n_cols), lambda bi, i: (layer, 0, 0), **const),
        pl.BlockSpec((d, 2 * _KV_PAD), lambda bi, i: (0, 0), **const),
        pl.BlockSpec((1, 2 * d), lambda bi, i: (0, 0), **const),
        pl.BlockSpec((1, _QK_W), lambda bi, i: (0, 0), **const),
        pl.BlockSpec((tm, LANES), lambda bi, i: (i, 0)),
        pl.BlockSpec((tm, LANES), lambda bi, i: (i, 0)),
        pl.BlockSpec((_BD_W, _BD_W), lambda bi, i: (0, 0), **const),
    ]
    return pl.pallas_call(
        _inproj_kernel,
        out_shape=out_shape,
        grid=(b, s // tm),
        in_specs=in_specs,
        out_specs=out_specs,
        scratch_shapes=[pltpu.VMEM((_A_SEG // LANES, tm, LANES), F32)],
        compiler_params=pltpu.CompilerParams(
            dimension_semantics=("arbitrary", "arbitrary"),
            vmem_limit_bytes=VMEM_LIMIT_BYTES),
        name="inproj",
    )(x, shift, scale, w_all, w_kv, b_gate, gqk, cos_t, sin_t, bd)


def _attn_a_kernel(q_ref, kp_ref, kc_ref, kn_ref, vp_ref, vc_ref, vn_ref, bias_ref,
                   o_ref, lse_ref, range_ref, *, phase_len, stabilise):
    n_ph, tq = q_ref.shape[1], q_ref.shape[2]
    i = pl.program_id(2)
    tk = A_SUB + 2 * A_RADIUS
    n_sub = tq // A_SUB
    lane = lax.broadcasted_iota(jnp.int32, (A_SUB, LANES), 1)
    low = lane < HEAD_DIM
    kcol = lax.broadcasted_iota(jnp.int32, (1, tk), 1)
    worst = jnp.zeros((A_SUB, LANES), F32)
    for ph, sub in [(ph, sub) for ph in range(n_ph) for sub in range(n_sub)]:
        if sub == 0:
            q = q_ref[0, ph]
            k_all = jnp.concatenate([kp_ref[0, ph], kc_ref[0, ph], kn_ref[0, ph]], axis=0)
            v_all = jnp.concatenate([vp_ref[0, ph], vc_ref[0, ph], vn_ref[0, ph]], axis=0)
        r0 = sub * A_SUB
        at_edge = sub == 0 or sub == n_sub - 1
        if at_edge:
            krow = i * tq + (r0 - A_RADIUS) + kcol
            valid = (krow >= 0) & (krow < phase_len)
        for pair in range(A_HEADS_PER_GROUP // 2):
            c0 = pair * LANES
            qp = q[r0:r0 + A_SUB, c0:c0 + LANES]
            kp = k_all[r0:r0 + tk, c0:c0 + LANES]
            vp = v_all[r0:r0 + tk, c0:c0 + LANES]
            zero = jnp.zeros_like(qp)
            qs = jnp.concatenate([jnp.where(low, qp, zero), jnp.where(low, zero, qp)], axis=0)
            s = lax.dot_general(qs, kp, _NT, preferred_element_type=F32) + bias_ref[pair]
            if at_edge:
                s = jnp.where(valid, s, NEG_INF)
            if stabilise:
                m = jnp.max(s, axis=1, keepdims=True)
                s = s - m
            p = jnp.exp2(s)
            l = jnp.sum(p, axis=1, keepdims=True)
            o = jnp.dot(p.astype(BF16), vp, preferred_element_type=F32)
            pick = lambda a: jnp.where(low, a[:A_SUB], a[A_SUB:])
            l = pick(l)
            lse = jnp.log2(l)
            worst = jnp.maximum(worst, jnp.abs(lse))
            if stabilise:
                lse = lse + pick(m)
            o_ref[0, ph, r0:r0 + A_SUB, c0:c0 + LANES] = (pick(o) * (1.0 / l)).astype(BF16)
            lse_ref[0, ph, r0:r0 + A_SUB, c0:c0 + LANES] = lse
    range_ref[0, 0, 0] = jnp.broadcast_to(
        jnp.max(jnp.max(worst, axis=1, keepdims=True), axis=0, keepdims=True), (SUBLANES, LANES))


def _attn_a_call(a_g, bias_g, *, stabilise):
    b, dil, phase_len, _ = a_g.shape
    tq = min(A_TQ, phase_len)
    n_steps = phase_len // tq
    n_ph = min(dil, A_TQ // tq)
    halo = A_RADIUS
    nh = phase_len // halo
    per = tq // halo
    cur = lambda col: (lambda bi, r, i: (bi, r, i, col))
    prev = lambda col: (lambda bi, r, i: (bi, r, jnp.maximum(i * per - 1, 0), col))
    nxt = lambda col: (lambda bi, r, i: (bi, r, jnp.minimum((i + 1) * per, nh - 1), col))
    blk = (1, n_ph, tq, A_WIDTH)
    hblk = (1, n_ph, halo, A_WIDTH)
    o_sds = jax.ShapeDtypeStruct((b, dil, phase_len, A_WIDTH), BF16)
    lse_sds = jax.ShapeDtypeStruct((b, dil, phase_len, A_WIDTH), F32)
    range_sds = jax.ShapeDtypeStruct((b, dil // n_ph, n_steps, SUBLANES, LANES), F32)
    return pl.pallas_call(
        functools.partial(_attn_a_kernel, phase_len=phase_len, stabilise=stabilise),
        out_shape=(o_sds, lse_sds, range_sds),
        grid=(b, dil // n_ph, n_steps),
        in_specs=[
            pl.BlockSpec(blk, cur(0)),
            pl.BlockSpec(hblk, prev(1)), pl.BlockSpec(blk, cur(1)), pl.BlockSpec(hblk, nxt(1)),
            pl.BlockSpec(hblk, prev(2)), pl.BlockSpec(blk, cur(2)), pl.BlockSpec(hblk, nxt(2)),
            pl.BlockSpec(bias_g.shape, lambda bi, r, i: (0, 0, 0)),
        ],
        out_specs=(pl.BlockSpec(blk, cur(0)), pl.BlockSpec(blk, cur(0)),
                   pl.BlockSpec((1, 1, 1, SUBLANES, LANES), lambda bi, r, i: (bi, r, i, 0, 0))),
        compiler_params=pltpu.CompilerParams(
            dimension_semantics=("arbitrary", "arbitrary", "arbitrary"),
            vmem_limit_bytes=VMEM_LIMIT_BYTES),
        name=f"attn_a_d{dil}" + ("_stab" if stabilise else ""),
    )(a_g, a_g, a_g, a_g, a_g, a_g, a_g, bias_g)


def _attn_a(a_g, bias_g):
    o, lse, worst = _attn_a_call(a_g, bias_g, stabilise=False)
    in_range = jnp.max(worst) <= A_MAX_ABS_LOG2_DENOM
    return lax.cond(in_range,
                    lambda: (o, lse),
                    lambda: _attn_a_call(a_g, bias_g, stabilise=True)[:2])


_VT_ROWS = 80


def _attn_b_kernel(kmax_ref, qt_ref, qn_ref, k_ref, vt_ref, o_ref, qst_ref, acc_ref, *m_scratch,
                   online):
    tq = qt_ref.shape[2]
    seq = k_ref.shape[1]
    kmax = kmax_ref[pl.program_id(0) * B_KV_HEADS + pl.program_id(1)]
    tail_row = lax.broadcasted_iota(jnp.int32, (LANES - HEAD_DIM, tq), 0)
    for h in range(B_GROUP):
        cols = slice(h * tq, (h + 1) * tq)
        qst_ref[0:HEAD_DIM, cols] = qt_ref[0, h * HEAD_DIM:(h + 1) * HEAD_DIM, :]
        if online:
            tail = jnp.zeros(tail_row.shape, F32)
        else:
            bound = jnp.sqrt(qn_ref[0, 0, h:h + 1, :]) * (kmax * B_BOUND_SLACK)
            tail = jnp.where(tail_row == _ONE_LANE - HEAD_DIM, -bound, 0.0)
        qst_ref[HEAD_DIM:LANES, cols] = tail.astype(BF16)
    acc_ref[...] = jnp.zeros(acc_ref.shape, F32)
    if online:
        m_ref, = m_scratch
        m_ref[...] = jnp.full(m_ref.shape, NEG_INF, F32)

    tk = B_TK_ONLINE if online else B_TK
    n_chunks = seq // tk

    def scores_t(c):
        start = pl.multiple_of(c * tk, tk)
        return jnp.dot(k_ref[0, pl.ds(start, tk), :], qst_ref[...],
                       preferred_element_type=F32)

    def values_t(c):
        start = pl.multiple_of(c * tk, tk)
        return vt_ref[0, 0:_VT_ROWS, pl.ds(start, tk)]

    if online:
        def body(c, carry):
            s_t = scores_t(c)
            m_prev = m_ref[...]
            m_new = jnp.maximum(m_prev, jnp.max(s_t, axis=0, keepdims=True))
            p_t = jnp.exp2(s_t - m_new).astype(BF16)
            acc_ref[...] = (jnp.exp2(m_prev - m_new) * acc_ref[...]
                            + jnp.dot(values_t(c), p_t, preferred_element_type=F32))
            m_ref[...] = m_new
            return carry

        lax.fori_loop(0, n_chunks, body, 0)
    else:
        def body(c, carry):
            start = pl.multiple_of(c * tk, tk)
            kc = k_ref[0, pl.ds(start, tk), :]
            vtc = values_t(c)
            for h in range(B_GROUP):
                cols = slice(h * tq, (h + 1) * tq)
                p_t = jnp.exp2(jnp.dot(kc, qst_ref[:, cols],
                                       preferred_element_type=F32)).astype(BF16)
                acc_ref[:, cols] += jnp.dot(vtc, p_t, preferred_element_type=F32)
            return carry

        lax.fori_loop(0, n_chunks, body, 0)
    acc = acc_ref[...]
    o_t = acc[0:HEAD_DIM, :] * (1.0 / acc[_ONE_LANE:_ONE_LANE + 1, :])
    for h in range(B_GROUP):
        o_ref[0, h * HEAD_DIM:(h + 1) * HEAD_DIM, :] = o_t[:, h * tq:(h + 1) * tq].astype(BF16)


def _attn_b_call(kmax, bqt, qn2, bk, bvt, *, online):
    b, _, s = bqt.shape
    qw = B_GROUP * HEAD_DIM
    cols = B_GROUP * B_TQ
    scratch = [pltpu.VMEM((LANES, cols), BF16), pltpu.VMEM((_VT_ROWS, cols), F32)]
    if online:
        scratch.append(pltpu.VMEM((1, cols), F32))
    return pl.pallas_call(
        functools.partial(_attn_b_kernel, online=online),
        out_shape=jax.ShapeDtypeStruct((b, B_WIDTH, s), BF16),
        grid_spec=pltpu.PrefetchScalarGridSpec(
            num_scalar_prefetch=1,
            grid=(b, B_KV_HEADS, s // B_TQ),
            in_specs=[
                pl.BlockSpec((1, qw, B_TQ), lambda bi, h, i, km: (bi, h, i)),
                pl.BlockSpec((1, 1, B_GROUP, B_TQ), lambda bi, h, i, km: (bi, h, 0, i)),
                pl.BlockSpec((1, s, LANES), lambda bi, h, i, km: (bi, 0, h)),
                pl.BlockSpec((1, LANES, s), lambda bi, h, i, km: (bi, h, 0)),
            ],
            out_specs=pl.BlockSpec((1, qw, B_TQ), lambda bi, h, i, km: (bi, h, i)),
            scratch_shapes=scratch,
        ),
        compiler_params=pltpu.CompilerParams(
            dimension_semantics=("arbitrary", "arbitrary", "arbitrary"),
            vmem_limit_bytes=VMEM_LIMIT_BYTES),
        name="attn_b_online" if online else "attn_b",
    )(kmax, bqt, qn2.reshape(b, B_KV_HEADS, B_GROUP, s), bk, bvt)


def _outproj_kernel(x_ref, gate_ref, o0_ref, l0_ref, o1_ref, l1_ref, o2_ref, l2_ref,
                    ybt_ref, gz_ref, wpa_ref, wpb_ref, wo_ref, lng_ref, lnb_ref,
                    out_ref, so1, sl1, so2, sl2, *, alpha):
    tm = x_ref.shape[1]
    d_model = x_ref.shape[2]
    for (o_ref, l_ref, so, sl, g) in ((o1_ref, l1_ref, so1, sl1, 1), (o2_ref, l2_ref, so2, sl2, 2)):
        dil = A_GROUPS[g][1]
        for r in range(dil):
            for ch in range(A_WIDTH // LANES):
                cs = slice(ch * LANES, (ch + 1) * LANES)
                so[ch, pl.ds(r, tm // dil, stride=dil), :] = o_ref[0, r, :, cs].astype(F32)
                sl[ch, pl.ds(r, tm // dil, stride=dil), :] = l_ref[0, r, :, cs]
    cat = lambda ref: jnp.concatenate([ref[ch] for ch in range(A_WIDTH // LANES)], axis=1)
    l0, l1, l2 = l0_ref[0], cat(sl1), cat(sl2)
    mx = jnp.maximum(jnp.maximum(l0, l1), l2)
    e0, e1, e2 = jnp.exp2(l0 - mx), jnp.exp2(l1 - mx), jnp.exp2(l2 - mx)
    y_a = ((e0 * o0_ref[0].astype(F32) + e1 * cat(so1) + e2 * cat(so2))
           * (1.0 / (e0 + e1 + e2)))
    gz = gz_ref[0]
    ya = (y_a * gz[:, 0:A_WIDTH].astype(F32)).astype(BF16)
    y_b = ybt_ref[0].astype(F32).T
    yb = (y_b * gz[:, A_WIDTH:A_WIDTH + B_WIDTH].astype(F32)).astype(BF16)
    pa = jnp.dot(ya, wpa_ref[...], preferred_element_type=F32)
    pb = jnp.dot(yb, wpb_ref[...], preferred_element_type=F32)
    o = A_WIDTH + B_WIDTH
    g_a = gz[:, o:o + d_model].astype(F32)
    g_b = gz[:, o + d_model:o + 2 * d_model].astype(F32)
    merged = (g_a * pa + g_b * pb).astype(BF16)
    out = jnp.dot(merged, wo_ref[...], preferred_element_type=F32)
    h = alpha * x_ref[0] + gate_ref[0] * out
    mu = jnp.mean(h, axis=-1, keepdims=True)
    hc = h - mu
    var = jnp.mean(hc * hc, axis=-1, keepdims=True)
    out_ref[0] = hc * lax.rsqrt(var + LN_EPS) * lng_ref[...] + lnb_ref[...]


def _outproj_call(x, gate, oa, ybt, gz, w_pa, w_pb, w_o, ln_g, ln_b, alpha):
    b, s, d = x.shape
    tm = OUT_TM
    (o0, l0), (o1, l1), (o2, l2) = oa
    d1, d2 = A_GROUPS[1][1], A_GROUPS[2][1]
    row = lambda bi, i: (bi, i, 0)
    ph = lambda bi, i: (bi, 0, i, 0)
    fixed = lambda bi, i: (0, 0)
    deep = dict(pipeline_mode=pl.Buffered(3))
    s0 = pl.BlockSpec((1, 1, tm, A_WIDTH), ph, **deep)
    s1 = pl.BlockSpec((1, d1, tm // d1, A_WIDTH), ph, **deep)
    s2 = pl.BlockSpec((1, d2, tm // d2, A_WIDTH), ph, **deep)
    in_specs = [
        pl.BlockSpec((1, tm, d), row, **deep),
        pl.BlockSpec((1, 1, d), lambda bi, i: (bi, 0, 0)),
        s0, s0, s1, s1, s2, s2,
        pl.BlockSpec((1, B_WIDTH, tm), lambda bi, i: (bi, 0, i), **deep),
        pl.BlockSpec((1, tm, gz.shape[2]), row, **deep),
        pl.BlockSpec(w_pa.shape, fixed),
        pl.BlockSpec(w_pb.shape, fixed),
        pl.BlockSpec(w_o.shape, fixed),
        pl.BlockSpec((1, d), fixed),
        pl.BlockSpec((1, d), fixed),
    ]
    out_spec = pl.BlockSpec((1, tm, d), row)
    n_in = len(in_specs)

    def outer(*refs):
        hbm_refs, scratch = refs[:n_in + 1], refs[n_in + 1:]

        def step(x_ref, gate_ref, o0_ref, l0_ref, *rest):
            _outproj_kernel(x_ref, gate_ref, o0_ref.at[0], l0_ref.at[0], *rest, *scratch,
                            alpha=alpha)

        pltpu.emit_pipeline(step, grid=(b, s // tm), in_specs=in_specs,
                            out_specs=[out_spec])(*hbm_refs)

    any_spec = pl.BlockSpec(memory_space=pl.ANY)
    return pl.pallas_call(
        outer,
        out_shape=jax.ShapeDtypeStruct((b, s, d), F32),
        in_specs=[any_spec] * n_in,
        out_specs=any_spec,
        scratch_shapes=[pltpu.VMEM((A_WIDTH // LANES, tm, LANES), F32)] * 4,
        compiler_params=pltpu.CompilerParams(vmem_limit_bytes=VMEM_LIMIT_BYTES),
        name="outproj",
    )(x, gate, o0, l0, o1, l1, o2, l2, ybt, gz, w_pa, w_pb, w_o, ln_g, ln_b)


def _t5_bucket(rel):
    half = REL_BUCKETS // 2
    max_exact = half // 2
    ret = jnp.where(rel > 0, half, 0)
    a = jnp.abs(rel)
    af = jnp.maximum(a, 1).astype(F32)
    large = max_exact + (jnp.log(af / max_exact) / math.log(REL_MAX_DISTANCE / max_exact)
                         * (half - max_exact)).astype(jnp.int32)
    large = jnp.minimum(large, half - 1)
    return ret + jnp.where(a < max_exact, a, large)


def _window_bias(rel_table, g):
    dil = A_GROUPS[g][1]
    tk = A_SUB + 2 * A_RADIUS
    rel = jnp.arange(tk)[None, :] - A_RADIUS - jnp.arange(A_SUB)[:, None]
    table_g = rel_table[:, g * A_HEADS_PER_GROUP:(g + 1) * A_HEADS_PER_GROUP]
    onehot = (_t5_bucket(rel * dil)[..., None] == jnp.arange(REL_BUCKETS)).astype(F32)
    bias = jnp.einsum("qkb,bh->hqk", onehot, table_g.astype(F32), precision=lax.Precision.HIGHEST)
    bias = jnp.where((jnp.abs(rel) <= A_RADIUS)[None], bias * LOG2_E, NEG_INF)
    return bias.reshape(A_HEADS_PER_GROUP // 2, 2 * A_SUB, tk)


def _rope_tables(seq):
    t = jnp.arange(seq)
    row = (t // GRID_W).astype(F32)
    col = (t % GRID_W).astype(F32)
    half = HEAD_DIM // 2
    inv = ROPE_THETA ** (-jnp.arange(0, half, 2, dtype=F32) / half)
    ar, ac = row[:, None] * inv[None], col[:, None] * inv[None]
    cos = jnp.concatenate([jnp.cos(ar), jnp.cos(ar), jnp.cos(ac), jnp.cos(ac)], axis=1)
    sin = jnp.concatenate([-jnp.sin(ar), jnp.sin(ar), -jnp.sin(ac), jnp.sin(ac)], axis=1)
    return jnp.tile(cos, (1, LANES // HEAD_DIM)), jnp.tile(sin, (1, LANES // HEAD_DIM))


def _padded_kv_weights(w):
    d = w.shape[0]
    off = _OFF_BQ + B_WIDTH
    zeros = jnp.zeros((d, LANES - HEAD_DIM), w.dtype)
    blocks = []
    for h in range(2 * B_KV_HEADS):
        blocks += [w[:, off + h * HEAD_DIM:off + (h + 1) * HEAD_DIM], zeros]
    return jnp.concatenate(blocks, axis=1).astype(BF16)


def _pad_heads(g):
    return jnp.concatenate([g, jnp.zeros((LANES - HEAD_DIM,), g.dtype)])


def kernel(x, c, rel_table, ln_g, ln_b, w_ada, b_ada, w_in, b_gate, q_norm_g, k_norm_g, w_pa, w_pb, w_o):
    depth = w_in.shape[0]
    b, s, d = x.shape
    alpha = float((2 * depth) ** 0.25)

    c_pad = jnp.zeros((SUBLANES, d), F32).at[:b].set(c)
    mod = _ada_call(c_pad, w_ada, b_ada)[:, :b]
    cos_t, sin_t = _rope_tables(s)
    idx = jnp.arange(_BD_W) // HEAD_DIM
    bd = (idx[:, None] == idx[None, :]).astype(BF16)
    biases = [_window_bias(rel_table, g) for g in range(len(A_GROUPS))]
    w_all = w_in.astype(BF16)

    for l in range(depth):
        shift = mod[l, :, 0:d].reshape(b, 1, d)
        scale = mod[l, :, d:2 * d].reshape(b, 1, d)
        gate = mod[l, :, 2 * d:3 * d].reshape(b, 1, d)
        gqk = jnp.concatenate([jnp.tile(q_norm_g[l], B_Q_HEADS),
                               jnp.tile(_pad_heads(k_norm_g[l]), B_KV_HEADS)]).reshape(1, _QK_W)
        a0, a1, a2, bqt, qn2, bk, bvt, gz, st = _inproj_call(
            x, shift, scale, w_all, l, _padded_kv_weights(w_in[l]), b_gate[l].reshape(1, 2 * d), gqk,
            cos_t, sin_t, bd)
        a0 = a0.reshape(b, 1, s, _A_SEG)
        oa = [_attn_a(a_g, biases[g]) for g, a_g in enumerate((a0, a1, a2))]
        kmax = jnp.sqrt(jnp.max(st[:, :, 0:B_KV_HEADS, 0], axis=1))
        qmax = jnp.sqrt(jnp.max(st[:, :, B_KV_HEADS, 0]))
        bound_ok = qmax * jnp.max(kmax) * B_BOUND_SLACK <= B_MAX_BOUND
        ybt = lax.cond(
            bound_ok,
            functools.partial(_attn_b_call, online=False),
            functools.partial(_attn_b_call, online=True),
            kmax.reshape(-1), bqt, qn2, bk, bvt)
        x = _outproj_call(x, gate, oa, ybt, gz, w_pa[l].astype(BF16), w_pb[l].astype(BF16),
                          w_o[l].astype(BF16), ln_g[l].reshape(1, d), ln_b[l].reshape(1, d), alpha)
    return x
```
